```python
import jax
import jax.numpy as jnp
from jax import lax
import numpy as np

D_MODEL = 1024
BATCH = 16
SEQ = 2048
DEPTH = 2
DEC_BATCH = 2
DEC_SEQ = 16384
PAST_LEN = 128

HEAD_DIM = 64
DIL_GROUPS = ((128, 1), (512, 4), (2048, 16))
DIL_HEADS_PER_GROUP = 4
DIL_HEADS = DIL_HEADS_PER_GROUP * len(DIL_GROUPS)
DIL_WIDTH = DIL_HEADS * HEAD_DIM
DIL_OUT = DIL_HEADS_PER_GROUP * HEAD_DIM
DIL_SCALE = HEAD_DIM ** -0.5
ROT_DIM = HEAD_DIM // 4
ROPE_THETA = 500000.0
BAND = 64

MLA_HEADS = 8
MLA_NOPE = 64
MLA_ROPE = 32
MLA_QK = MLA_NOPE + MLA_ROPE
MLA_V = 64
MLA_OUT = MLA_HEADS * MLA_V
MLA_SCALE = MLA_QK ** -0.5
Q_LORA = 256
KV_LORA = 128
Q_BLOCK = 128

IN_SIZES = (DIL_WIDTH, DIL_WIDTH, DIL_WIDTH, Q_LORA, KV_LORA, MLA_ROPE, 2 * D_MODEL)
IN_COLS = sum(IN_SIZES)

N_GROUPS = 8
EXPERTS_PER_GROUP = 8
N_EXPERTS = N_GROUPS * EXPERTS_PER_GROUP
TOP_K = 2
D_EXPERT = 512
MOE_BLOCK = 128

EPS = 1e-6

kernel_name = 'hybrid_dilated_mla_hmoe_encoder'


def rms_norm(t, g):
    t32 = t.astype(jnp.float32)
    y = t32 * lax.rsqrt(jnp.mean(t32 * t32, axis=-1, keepdims=True) + EPS)
    return (y * g.astype(jnp.float32)).astype(t.dtype)


def rope_tables(S, dim):
    inv = ROPE_THETA ** (-jnp.arange(0, dim, 2, dtype=jnp.float32) / dim)
    ang = jnp.arange(S, dtype=jnp.float32)[:, None] * inv[None, :]
    return jnp.cos(ang), jnp.sin(ang)


def apply_rope(t, cos, sin):
    t32 = t.astype(jnp.float32)
    half = t.shape[-1] // 2
    a, b = t32[..., :half], t32[..., half:]
    return jnp.concatenate([a * cos - b * sin, a * sin + b * cos], axis=-1).astype(t.dtype)


def split_cols(t):
    out, off = [], 0
    for n in IN_SIZES:
        out.append(t[..., off:off + n])
        off += n
    return out


def dilated_group(q, k, v, dil, half):
    B, S, H, C = q.shape
    L = S // dil
    nb = -(-L // BAND)
    Lp = nb * BAND

    def to_sub(t):
        t = t.reshape(B, L, dil, H, C).transpose(0, 2, 1, 3, 4)
        return jnp.pad(t, ((0, 0), (0, 0), (0, Lp - L), (0, 0), (0, 0)))

    def windows(t):
        t = jnp.pad(to_sub(t), ((0, 0), (0, 0), (BAND, BAND), (0, 0), (0, 0)))
        t = t.reshape(B, dil, nb + 2, BAND, H, C)
        return jnp.concatenate([t[:, :, :-2], t[:, :, 1:-1], t[:, :, 2:]], axis=3)

    qs = to_sub(q).reshape(B, dil, nb, BAND, H, C)
    kw, vw = windows(k), windows(v)
    blk = jnp.arange(nb)[:, None] * BAND
    qpos = blk + jnp.arange(BAND)[None, :]
    kpos = blk - BAND + jnp.arange(3 * BAND)[None, :]
    valid = ((jnp.abs(qpos[:, :, None] - kpos[:, None, :]) <= half)
             & (kpos >= 0)[:, None, :] & (kpos < L)[:, None, :])
    s = jnp.einsum('bdnqhc,bdnkhc->bdnhqk', qs, kw).astype(jnp.float32)
    s = jnp.where(valid[:, None], s, -jnp.inf)
    m = jnp.max(s, axis=-1, keepdims=True)
    e = jnp.exp(s - m)
    den = jnp.sum(e, axis=-1, keepdims=True)
    o = jnp.einsum('bdnhqk,bdnkhc->bdnqhc', (e / den).astype(v.dtype), vw)
    lse = (m + jnp.log(den))[..., 0]
    o = o.reshape(B, dil, Lp, H, C)[:, :, :L].transpose(0, 2, 1, 3, 4).reshape(B, S, H, C)
    lse = lse.transpose(0, 1, 2, 4, 3).reshape(B, dil, Lp, H)[:, :, :L]
    lse = lse.transpose(0, 2, 1, 3).reshape(B, S, H)
    return o, lse


def dilated_branch(qa, ka, va, cos, sin):
    B, S, _ = qa.shape

    def heads_rot(t):
        t = t.reshape(B, S, DIL_HEADS, HEAD_DIM)
        return jnp.concatenate(
            [apply_rope(t[..., :ROT_DIM], cos[:, None], sin[:, None]), t[..., ROT_DIM:]], axis=-1)

    q = heads_rot(qa) * DIL_SCALE
    k = heads_rot(ka)
    v = va.reshape(B, S, DIL_HEADS, HEAD_DIM)
    outs, lses = [], []
    for g, (window, dil) in enumerate(DIL_GROUPS):
        sl = slice(g * DIL_HEADS_PER_GROUP, (g + 1) * DIL_HEADS_PER_GROUP)
        o, lse = dilated_group(q[:, :, sl], k[:, :, sl], v[:, :, sl], dil, window // (2 * dil))
        outs.append(o)
        lses.append(lse)
    alpha = jax.nn.softmax(jnp.stack(lses, axis=0), axis=0)
    o = jnp.sum(alpha[..., None] * jnp.stack(outs, axis=0).astype(jnp.float32), axis=0)
    return o.astype(qa.dtype).reshape(B, S, DIL_OUT)


def mla_branch(cq, ckv, kr, q_norm_g, w_uq, kv_norm_g, w_uk, w_uv, cos, sin):
    B, S, _ = cq.shape
    q = (rms_norm(cq, q_norm_g) @ w_uq).reshape(B, S, MLA_HEADS, MLA_QK) * MLA_SCALE
    q_nope = q[..., :MLA_NOPE]
    q_rope = apply_rope(q[..., MLA_NOPE:], cos[:, None], sin[:, None])
    c = rms_norm(ckv, kv_norm_g)
    k_nope = (c @ w_uk).reshape(B, S, MLA_HEADS, MLA_NOPE)
    v = (c @ w_uv).reshape(B, S, MLA_HEADS, MLA_V)
    k_rope = apply_rope(kr, cos, sin)
    nq = S // Q_BLOCK

    def to_blocks(t):
        return t.reshape(B, nq, Q_BLOCK, *t.shape[2:]).swapaxes(0, 1)

    def attend(blk):
        qn, qr = blk
        s = (jnp.einsum('bqhc,bkhc->bhqk', qn, k_nope)
             + jnp.einsum('bqhr,bkr->bhqk', qr, k_rope)).astype(jnp.float32)
        p = jax.nn.softmax(s, axis=-1).astype(v.dtype)
        return jnp.einsum('bhqk,bkhc->bqhc', p, v)

    o = lax.map(attend, (to_blocks(q_nope), to_blocks(q_rope)))
    return o.swapaxes(0, 1).reshape(B, S, MLA_OUT)


def moe_ffn(h, w_rg, b_rg, w_re, b_re, w_g, w_u, w_d):
    B, S, D = h.shape
    T = B * S
    xf = h.reshape(T, D)
    g_logits = (xf @ w_rg).astype(jnp.float32) + b_rg.astype(jnp.float32)
    g_prob = jax.nn.softmax(g_logits, axis=-1)
    grp = jnp.argmax(g_logits, axis=-1)
    p_grp = jnp.take_along_axis(g_prob, grp[:, None], axis=1)[:, 0]
    e_logits = ((xf @ w_re).astype(jnp.float32) + b_re.astype(jnp.float32))
    e_logits = e_logits.reshape(T, N_GROUPS, EXPERTS_PER_GROUP)
    e_logits = jnp.take_along_axis(e_logits, grp[:, None, None], axis=1)[:, 0]
    top_p, top_i = lax.top_k(jax.nn.softmax(e_logits, axis=-1), TOP_K)
    weights = p_grp[:, None] * top_p / jnp.sum(top_p, axis=-1, keepdims=True)
    expert = grp[:, None] * EXPERTS_PER_GROUP + top_i

    A = T * TOP_K
    flat_e = expert.reshape(A)
    flat_tok = jnp.arange(A, dtype=jnp.int32) // TOP_K
    flat_w = weights.reshape(A)
    order = jnp.argsort(flat_e)
    e_sorted = flat_e[order]
    tok_sorted = flat_tok[order]
    counts = jnp.bincount(flat_e, length=N_EXPERTS)
    padded = (counts + MOE_BLOCK - 1) // MOE_BLOCK * MOE_BLOCK
    pad_end = jnp.cumsum(padded)
    pad_start = pad_end - padded
    start = jnp.cumsum(counts) - counts
    slot = pad_start[e_sorted] + jnp.arange(A, dtype=jnp.int32) - start[e_sorted]
    n_blocks = -(-A // MOE_BLOCK) + N_EXPERTS
    slot_tok = jnp.full((n_blocks * MOE_BLOCK,), T, jnp.int32).at[slot].set(tok_sorted)
    blk_expert = jnp.minimum(
        jnp.searchsorted(pad_end, jnp.arange(n_blocks, dtype=jnp.int32) * MOE_BLOCK, side='right'),
        N_EXPERTS - 1)
    xpad = jnp.concatenate([xf, jnp.zeros((1, D), xf.dtype)], axis=0)

    def expert_block(args):
        tok, e = args
        xb = xpad[tok]
        hb = jax.nn.silu(xb @ w_g[e]) * (xb @ w_u[e])
        return hb @ w_d[e]

    y_slots = lax.map(expert_block, (slot_tok.reshape(n_blocks, MOE_BLOCK), blk_expert))
    contrib = y_slots.reshape(-1, D)[slot] * flat_w[order][:, None].astype(xf.dtype)
    y = jnp.zeros((T, D), xf.dtype).at[tok_sorted].add(contrib)
    return y.reshape(B, S, D)


def encoder_trunk(x, norm1_g, w_in, q_norm_g, w_uq, kv_norm_g, w_uk, w_uv, w_pa, w_pb, w_o,
                  norm2_g, w_rg, b_rg, w_re, b_re, w_e_gate, w_e_up, w_e_down, final_g):
    S = x.shape[1]
    cos_a, sin_a = rope_tables(S, ROT_DIM)
    cos_b, sin_b = rope_tables(S, MLA_ROPE)
    for l in range(DEPTH):
        h = rms_norm(x, norm1_g[l])
        qa, ka, va, cq, ckv, kr, gl = split_cols(h @ w_in[l])
        oa = dilated_branch(qa, ka, va, cos_a, sin_a)
        ob = mla_branch(cq, ckv, kr, q_norm_g[l], w_uq[l], kv_norm_g[l], w_uk[l], w_uv[l],
                        cos_b, sin_b)
        gates = jax.nn.sigmoid(gl)
        merged = gates[..., :D_MODEL] * (oa @ w_pa[l]) + gates[..., D_MODEL:] * (ob @ w_pb[l])
        x = x + merged @ w_o[l]
        x = x + moe_ffn(rms_norm(x, norm2_g[l]), w_rg[l], b_rg[l], w_re[l], b_re[l],
                        w_e_gate[l], w_e_up[l], w_e_down[l])
    return rms_norm(x, final_g)


def setup_inputs(seed: int = 0) -> dict:
    key = jax.random.key(seed)
    ks = jax.random.split(key, 21)
    f32 = jnp.float32

    def w(k, shape, fan_in):
        return jax.random.normal(k, shape, f32) * (fan_in ** -0.5)

    def gain(k, shape):
        return 1.0 + 0.01 * jax.random.normal(k, shape, f32)

    return {
        'x_prompt': jax.random.normal(ks[0], (BATCH, SEQ, D_MODEL), f32),
        'x_sample': jax.random.normal(ks[1], (DEC_BATCH, DEC_SEQ, D_MODEL), f32),
        'norm1_g': gain(ks[2], (DEPTH, D_MODEL)),
        'w_in': w(ks[3], (DEPTH, D_MODEL, IN_COLS), D_MODEL),
        'q_norm_g': gain(ks[4], (DEPTH, Q_LORA)),
        'w_uq': w(ks[5], (DEPTH, Q_LORA, MLA_HEADS * MLA_QK), Q_LORA),
        'kv_norm_g': gain(ks[6], (DEPTH, KV_LORA)),
        'w_uk': w(ks[7], (DEPTH, KV_LORA, MLA_HEADS * MLA_NOPE), KV_LORA),
        'w_uv': w(ks[8], (DEPTH, KV_LORA, MLA_HEADS * MLA_V), KV_LORA),
        'w_pa': w(ks[9], (DEPTH, DIL_OUT, D_MODEL), DIL_OUT),
        'w_pb': w(ks[10], (DEPTH, MLA_OUT, D_MODEL), MLA_OUT),
        'w_o': w(ks[11], (DEPTH, D_MODEL, D_MODEL), D_MODEL),
        'norm2_g': gain(ks[12], (DEPTH, D_MODEL)),
        'w_rg': w(ks[13], (DEPTH, D_MODEL, N_GROUPS), D_MODEL),
        'b_rg': 0.01 * jax.random.normal(ks[14], (DEPTH, N_GROUPS), f32),
        'w_re': w(ks[15], (DEPTH, D_MODEL, N_EXPERTS), D_MODEL),
        'b_re': 0.01 * jax.random.normal(ks[16], (DEPTH, N_EXPERTS), f32),
        'w_e_gate': w(ks[17], (DEPTH, N_EXPERTS, D_MODEL, D_EXPERT), D_MODEL),
        'w_e_up': w(ks[18], (DEPTH, N_EXPERTS, D_MODEL, D_EXPERT), D_MODEL),
        'w_e_down': w(ks[19], (DEPTH, N_EXPERTS, D_EXPERT, D_MODEL), D_EXPERT),
        'final_g': gain(ks[20], (D_MODEL,)),
    }


def reference(x_prompt, x_sample, norm1_g, w_in, q_norm_g, w_uq, kv_norm_g, w_uk, w_uv, w_pa,
              w_pb, w_o, norm2_g, w_rg, b_rg, w_re, b_re, w_e_gate, w_e_up, w_e_down, final_g):
    y_prompt = encoder_trunk(x_prompt, norm1_g, w_in, q_norm_g, w_uq, kv_norm_g, w_uk, w_uv,
                             w_pa, w_pb, w_o, norm2_g, w_rg, b_rg, w_re, b_re,
                             w_e_gate, w_e_up, w_e_down, final_g)
    y_sample = encoder_trunk(x_sample, norm1_g, w_in, q_norm_g, w_uq, kv_norm_g, w_uk, w_uv,
                             w_pa, w_pb, w_o, norm2_g, w_rg, b_rg, w_re, b_re,
                             w_e_gate, w_e_up, w_e_down, final_g)
    return (y_prompt, y_sample)
```

```python
import functools

import jax
import jax.numpy as jnp
from jax import lax
from jax.experimental import pallas as pl
from jax.experimental.pallas import tpu as pltpu

D_MODEL = 1024
DEPTH = 2
HEAD_DIM = 64
DIL_GROUPS = ((128, 1), (512, 4), (2048, 16))
DIL_HEADS_PER_GROUP = 4
DIL_WIDTH = 768
DIL_OUT = 256
DIL_SCALE = HEAD_DIM ** -0.5
ROT_DIM = 16
ROPE_THETA = 500000.0
DIL_HALF = 64

MLA_HEADS = 8
MLA_NOPE = 64
MLA_ROPE = 32
MLA_QK = 96
MLA_V = 64
MLA_OUT = 512
MLA_SCALE = MLA_QK ** -0.5
Q_LORA = 256
KV_LORA = 128

N_GROUPS = 8
EXPERTS_PER_GROUP = 8
N_EXPERTS = 64
TOP_K = 2
D_EXPERT = 512
EPS = 1e-6

LANES = 128
MLA_SLOT = LANES
MLA_PAD = MLA_HEADS * MLA_SLOT
ROUTE_LANES = LANES
VMEM_LIMIT = 56 * 1024 * 1024

BF16 = jnp.bfloat16
F32 = jnp.float32
NEG = -1e30


def _cparams(sem):
    return pltpu.CompilerParams(dimension_semantics=sem, vmem_limit_bytes=VMEM_LIMIT)


def _rms(t, g):
    return t * lax.rsqrt(jnp.mean(t * t, axis=-1, keepdims=True) + EPS) * g


def _rope_chunk(t, c, sa, sb, shift):
    return t * c + pltpu.roll(t, LANES - shift, 1) * sa + pltpu.roll(t, shift, 1) * sb


def _inproj_kernel(x_ref, g1_ref, wq_ref, wk_ref, wv_ref, wcq_ref, wckv_ref, wkr_ref, wg_ref,
                   qn_ref, kvn_ref, wuq_ref, wuk_ref, wuv_ref,
                   ca_ref, saa_ref, sab_ref, cb_ref, sba_ref, sbb_ref,
                   qa_ref, ka_ref, va_ref, qm_ref, km_ref, vm_ref, gate_ref):
    x = x_ref[...]
    h = _rms(x, g1_ref[...]).astype(BF16)

    ca, saa, sab = ca_ref[...], saa_ref[...], sab_ref[...]
    q = jnp.dot(h, wq_ref[...], preferred_element_type=F32)
    k = jnp.dot(h, wk_ref[...], preferred_element_type=F32)
    for j in range(DIL_WIDTH // LANES):
        sl = slice(j * LANES, (j + 1) * LANES)
        qa_ref[:, sl] = (_rope_chunk(q[:, sl], ca, saa, sab, ROT_DIM // 2) * DIL_SCALE).astype(BF16)
        ka_ref[:, sl] = _rope_chunk(k[:, sl], ca, saa, sab, ROT_DIM // 2).astype(BF16)
    va_ref[...] = jnp.dot(h, wv_ref[...], preferred_element_type=F32).astype(BF16)

    cb, sba, sbb = cb_ref[...], sba_ref[...], sbb_ref[...]
    cq = jnp.dot(h, wcq_ref[...], preferred_element_type=F32)
    cqn = _rms(cq, qn_ref[...]).astype(BF16)
    qm = jnp.dot(cqn, wuq_ref[...], preferred_element_type=F32) * MLA_SCALE
    ckv = jnp.dot(h, wckv_ref[...], preferred_element_type=F32)
    c = _rms(ckv, kvn_ref[...]).astype(BF16)
    kn = jnp.dot(c, wuk_ref[...], preferred_element_type=F32)
    vv = jnp.dot(c, wuv_ref[...], preferred_element_type=F32)
    kr = jnp.dot(h, wkr_ref[...], preferred_element_type=F32)
    kr = _rope_chunk(kr, cb, sba, sbb, MLA_ROPE // 2)
    lane = lax.broadcasted_iota(jnp.int32, (1, LANES), 1)
    ones_col = jnp.where(lane == MLA_V, 1.0, 0.0).astype(F32)
    for j in range(MLA_HEADS):
        sl = slice(j * LANES, (j + 1) * LANES)
        qm_ref[:, sl] = _rope_chunk(qm[:, sl], cb, sba, sbb, MLA_ROPE // 2).astype(BF16)
        km_ref[:, sl] = (kn[:, sl] + kr).astype(BF16)
        vm_ref[:, sl] = (vv[:, sl] + ones_col).astype(BF16)

    gate_ref[...] = jax.nn.sigmoid(jnp.dot(h, wg_ref[...], preferred_element_type=F32))


def _inproj(x, lw, tabs, seq, tm):
    T = x.shape[0]
    nt = seq // tm
    row = lambda i: (i, 0)
    const = lambda i: (0, 0)
    tab = lambda i: (i % nt, 0)

    def wspec(a):
        return pl.BlockSpec(a.shape, const)

    weights = [lw['g1'], lw['wq'], lw['wk'], lw['wv'], lw['wcq'], lw['wckv'], lw['wkr'], lw['wg'],
               lw['qn'], lw['kvn'], lw['wuq'], lw['wuk'], lw['wuv']]
    in_specs = ([pl.BlockSpec((tm, D_MODEL), row)] + [wspec(a) for a in weights]
                + [pl.BlockSpec((tm, LANES), tab)] * 6)
    out_shape = [jax.ShapeDtypeStruct((T, DIL_WIDTH), BF16)] * 3 \
        + [jax.ShapeDtypeStruct((T, MLA_PAD), BF16)] * 3 \
        + [jax.ShapeDtypeStruct((T, 2 * D_MODEL), F32)]
    out_specs = [pl.BlockSpec((tm, DIL_WIDTH), row)] * 3 + [pl.BlockSpec((tm, MLA_PAD), row)] * 3 \
        + [pl.BlockSpec((tm, 2 * D_MODEL), row)]
    return pl.pallas_call(
        _inproj_kernel, grid=(T // tm,), in_specs=in_specs, out_specs=out_specs, out_shape=out_shape,
        compiler_params=_cparams(("parallel",)), name="inproj",
    )(x, *weights, *tabs)


def _dil_kernel(q_ref, kp_ref, kc_ref, kn_ref, vp_ref, vc_ref, vn_ref, o_ref, lse_ref,
                kbuf, vbuf, *, tq, sub_len):
    i = pl.program_id(2)
    qb = LANES
    kbuf[0:qb] = kp_ref[0]
    kbuf[qb:qb + tq] = kc_ref[0]
    kbuf[qb + tq:] = kn_ref[0]
    vbuf[0:qb] = vp_ref[0]
    vbuf[qb:qb + tq] = vc_ref[0]
    vbuf[qb + tq:] = vn_ref[0]
    head = lax.broadcasted_iota(jnp.int32, (1, DIL_OUT), 1) // HEAD_DIM
    rel = (lax.broadcasted_iota(jnp.int32, (qb, 2 * qb), 1) - DIL_HALF
           - lax.broadcasted_iota(jnp.int32, (qb, 2 * qb), 0))
    band = jnp.abs(rel) <= DIL_HALF
    kcol = lax.broadcasted_iota(jnp.int32, (1, 2 * qb), 1) - DIL_HALF
    for j in range(tq // qb):
        q = q_ref[0, j * qb:(j + 1) * qb, :]
        k = kbuf[DIL_HALF + j * qb:DIL_HALF + j * qb + 2 * qb, :]
        v = vbuf[DIL_HALF + j * qb:DIL_HALF + j * qb + 2 * qb, :]
        kpos = kcol + (i * tq + j * qb)
        valid = band & (kpos >= 0) & (kpos < sub_len)
        o_acc = jnp.zeros((qb, DIL_OUT), F32)
        l_acc = jnp.zeros((qb, DIL_OUT), F32)
        for hd in range(DIL_HEADS_PER_GROUP):
            hm = head == hd
            qh = jnp.where(hm, q, jnp.zeros_like(q))
            s = lax.dot_general(qh, k, (((1,), (1,)), ((), ())), preferred_element_type=F32)
            s = jnp.where(valid, s, NEG)
            m = jnp.max(s, axis=-1, keepdims=True)
            p = jnp.exp(s - m)
            den = jnp.sum(p, axis=-1, keepdims=True)
            oh = jnp.dot(p.astype(BF16), v, preferred_element_type=F32) * (1.0 / den)
            o_acc = jnp.where(hm, oh, o_acc)
            l_acc = jnp.where(hm, m + jnp.log(den), l_acc)
        o_ref[0, j * qb:(j + 1) * qb, :] = o_acc
        lse_ref[0, j * qb:(j + 1) * qb, :] = l_acc


def _dilated_group(qa, ka, va, batch, seq, g, dil):
    L = seq // dil
    tq = min(L, 512)
    nq = L // tq
    hb = tq // LANES
    nhb = L // LANES
    qv = qa.reshape(batch, L, dil * DIL_WIDTH)
    kv = ka.reshape(batch, L, dil * DIL_WIDTH)
    vv = va.reshape(batch, L, dil * DIL_WIDTH)
    ncol = DIL_WIDTH // DIL_OUT
    cur = lambda b, r, i: (b, i, r * ncol + g)
    prev = lambda b, r, i: (b, jnp.maximum(i * hb - 1, 0), r * ncol + g)
    nxt = lambda b, r, i: (b, jnp.minimum((i + 1) * hb, nhb - 1), r * ncol + g)
    blk = (1, tq, DIL_OUT)
    halo = (1, LANES, DIL_OUT)
    out_map = lambda b, r, i: (b, i, r)
    o, lse = pl.pallas_call(
        functools.partial(_dil_kernel, tq=tq, sub_len=L),
        grid=(batch, dil, nq),
        in_specs=[pl.BlockSpec(blk, cur),
                  pl.BlockSpec(halo, prev), pl.BlockSpec(blk, cur), pl.BlockSpec(halo, nxt),
                  pl.BlockSpec(halo, prev), pl.BlockSpec(blk, cur), pl.BlockSpec(halo, nxt)],
        out_specs=[pl.BlockSpec(blk, out_map), pl.BlockSpec(blk, out_map)],
        out_shape=[jax.ShapeDtypeStruct((batch, L, dil * DIL_OUT), F32)] * 2,
        scratch_shapes=[pltpu.VMEM((tq + 2 * LANES, DIL_OUT), BF16)] * 2,
        compiler_params=_cparams(("parallel", "parallel", "parallel")), name=f"dilated_g{g}",
    )(qv, kv, kv, kv, vv, vv, vv)
    return o.reshape(batch * seq, DIL_OUT), lse.reshape(batch * seq, DIL_OUT)


def _mla_kernel(q_ref, k_ref, v_ref, o_ref, m_sc, acc_sc, *, tk, nk):
    outs = []
    for hh in range(2):
        sl = slice(hh * LANES, (hh + 1) * LANES)
        q = q_ref[0, :, sl]
        m_sc[...] = jnp.full(m_sc.shape, NEG, F32)
        acc_sc[...] = jnp.zeros(acc_sc.shape, F32)

        def body(j, carry):
            off = pl.multiple_of(j * tk, tk)
            k = k_ref[0, pl.ds(off, tk), sl]
            v = v_ref[0, pl.ds(off, tk), sl]
            s = lax.dot_general(q, k, (((1,), (1,)), ((), ())), preferred_element_type=F32)
            m_old = m_sc[...]
            m_new = jnp.maximum(m_old, jnp.max(s, axis=-1, keepdims=True))
            p = jnp.exp(s - m_new).astype(BF16)
            acc_sc[...] = acc_sc[...] * jnp.exp(m_old - m_new) + jnp.dot(p, v, preferred_element_type=F32)
            m_sc[...] = m_new
            return carry

        lax.fori_loop(0, nk, body, 0)
        acc = acc_sc[...]
        outs.append(acc * (1.0 / acc[:, MLA_V:MLA_V + 1]))
    lane = lax.broadcasted_iota(jnp.int32, (1, LANES), 1)
    o_ref[0] = jnp.where(lane < MLA_V, outs[0], pltpu.roll(outs[1], MLA_V, 1)).astype(o_ref.dtype)


def _mla(qm, km, vm, batch, seq):
    tq = min(seq, 256)
    tk = min(seq, 512)
    q3 = qm.reshape(batch, seq, MLA_PAD)
    k3 = km.reshape(batch, seq, MLA_PAD)
    v3 = vm.reshape(batch, seq, MLA_PAD)
    o = pl.pallas_call(
        functools.partial(_mla_kernel, tk=tk, nk=seq // tk),
        grid=(batch, MLA_HEADS // 2, seq // tq),
        in_specs=[pl.BlockSpec((1, tq, 2 * LANES), lambda b, h, i: (b, i, h)),
                  pl.BlockSpec((1, seq, 2 * LANES), lambda b, h, i: (b, 0, h)),
                  pl.BlockSpec((1, seq, 2 * LANES), lambda b, h, i: (b, 0, h))],
        out_specs=pl.BlockSpec((1, tq, LANES), lambda b, h, i: (b, i, h)),
        out_shape=jax.ShapeDtypeStruct((batch, seq, MLA_OUT), BF16),
        scratch_shapes=[pltpu.VMEM((tq, 1), F32), pltpu.VMEM((tq, LANES), F32)],
        compiler_params=_cparams(("parallel", "parallel", "arbitrary")), name="mla",
    )(q3, k3, v3)
    return o.reshape(batch * seq, MLA_OUT)


def _outproj_kernel(x_ref, o0_ref, l0_ref, o1_ref, l1_ref, o2_ref, l2_ref, ob_ref, gate_ref,
                    wpa_ref, wpb_ref, wo_ref, g2_ref, wrh_ref, wrl_ref, br_ref,
                    xm_ref, h2_ref, route_ref):
    l0, l1, l2 = l0_ref[...], l1_ref[...], l2_ref[...]
    m = jnp.maximum(jnp.maximum(l0, l1), l2)
    w0, w1, w2 = jnp.exp(l0 - m), jnp.exp(l1 - m), jnp.exp(l2 - m)
    oa = (w0 * o0_ref[...] + w1 * o1_ref[...] + w2 * o2_ref[...]) / (w0 + w1 + w2)
    pa = jnp.dot(oa.astype(BF16), wpa_ref[...], preferred_element_type=F32)
    pb = jnp.dot(ob_ref[...], wpb_ref[...], preferred_element_type=F32)
    merged = gate_ref[:, :D_MODEL] * pa + gate_ref[:, D_MODEL:] * pb
    xm = x_ref[...] + jnp.dot(merged.astype(BF16), wo_ref[...], preferred_element_type=F32)
    xm_ref[...] = xm
    h2 = _rms(xm, g2_ref[...])
    h2_ref[...] = h2

    hi = h2.astype(BF16)
    lo = (h2 - hi.astype(F32)).astype(BF16)
    lg = (jnp.dot(hi, wrh_ref[...], preferred_element_type=F32)
          + jnp.dot(lo, wrh_ref[...], preferred_element_type=F32)
          + jnp.dot(hi, wrl_ref[...], preferred_element_type=F32)) + br_ref[...]
    lane = lax.broadcasted_iota(jnp.int32, lg.shape, 1)
    lanef = lane.astype(F32)
    big = float(ROUTE_LANES)
    gmask = (lane >= N_EXPERTS) & (lane < N_EXPERTS + N_GROUPS)
    gl = jnp.where(gmask, lg, NEG)
    gmax = jnp.max(gl, axis=-1, keepdims=True)
    gidx = jnp.min(jnp.where(gl == gmax, lanef, big), axis=-1, keepdims=True) - float(N_EXPERTS)
    p_grp = 1.0 / jnp.sum(jnp.where(gmask, jnp.exp(gl - gmax), 0.0), axis=-1, keepdims=True)
    emask = (lane // EXPERTS_PER_GROUP).astype(F32) == gidx
    el = jnp.where(emask, lg, NEG)
    m1 = jnp.max(el, axis=-1, keepdims=True)
    i1 = jnp.min(jnp.where(el == m1, lanef, big), axis=-1, keepdims=True)
    el2 = jnp.where(lanef == i1, NEG, el)
    m2 = jnp.max(el2, axis=-1, keepdims=True)
    i2 = jnp.min(jnp.where(el2 == m2, lanef, big), axis=-1, keepdims=True)
    r = jnp.exp(m2 - m1)
    wa = p_grp / (1.0 + r)
    wb = p_grp * r / (1.0 + r)
    route_ref[...] = jnp.where(lane == 0, i1, jnp.where(lane == 1, i2,
                               jnp.where(lane == 2, wa, jnp.where(lane == 3, wb, 0.0))))


def _outproj(x, dil_outs, ob, gates, lw, tm):
    T = x.shape[0]
    row = lambda i: (i, 0)
    const = lambda i: (0, 0)
    weights = [lw['wpa'], lw['wpb'], lw['wo'], lw['g2'], lw['wrh'], lw['wrl'], lw['br']]
    acts = [x] + list(dil_outs) + [ob, gates]
    in_specs = [pl.BlockSpec((tm, a.shape[1]), row) for a in acts] \
        + [pl.BlockSpec(a.shape, const) for a in weights]
    return pl.pallas_call(
        _outproj_kernel, grid=(T // tm,), in_specs=in_specs,
        out_specs=[pl.BlockSpec((tm, D_MODEL), row), pl.BlockSpec((tm, D_MODEL), row),
                   pl.BlockSpec((tm, ROUTE_LANES), row)],
        out_shape=[jax.ShapeDtypeStruct((T, D_MODEL), F32), jax.ShapeDtypeStruct((T, D_MODEL), F32),
                   jax.ShapeDtypeStruct((T, ROUTE_LANES), F32)],
        compiler_params=_cparams(("parallel",)), name="outproj_router",
    )(*acts, *weights)


GATHER_ROWS = 256


def _gather_kernel(idx_ref, src_ref, dst_ref, sems):
    i = pl.program_id(0)
    n = pl.num_programs(0)
    slot = i % 2

    def row_copy(src_row, dst_row, sem):
        return pltpu.make_async_copy(src_ref.at[pl.ds(src_row, 1), :], dst_ref.at[pl.ds(dst_row, 1), :], sem)

    def issue(r, carry):
        row_copy(idx_ref[0, 0, r], i * GATHER_ROWS + r, sems.at[slot]).start()
        return carry

    lax.fori_loop(0, GATHER_ROWS, issue, 0)

    def drain(sem):
        def w(r, carry):
            row_copy(0, 0, sem).wait()
            return carry
        lax.fori_loop(0, GATHER_ROWS, w, 0)

    @pl.when(i > 0)
    def _():
        drain(sems.at[1 - slot])

    @pl.when(i == n - 1)
    def _():
        drain(sems.at[slot])


def _gather_rows(src, idx):
    m = idx.shape[0]
    nb = m // GATHER_ROWS
    idx3 = idx.reshape(nb, 1, GATHER_ROWS)
    return pl.pallas_call(
        _gather_kernel, grid=(nb,),
        in_specs=[pl.BlockSpec((1, 1, GATHER_ROWS), lambda i: (i, 0, 0), memory_space=pltpu.SMEM),
                  pl.BlockSpec(memory_space=pl.ANY)],
        out_specs=pl.BlockSpec(memory_space=pl.ANY),
        out_shape=jax.ShapeDtypeStruct((m, src.shape[1]), src.dtype),
        scratch_shapes=[pltpu.SemaphoreType.DMA((2,))],
        compiler_params=_cparams(("arbitrary",)), name="gather_rows",
    )(idx3, src)


EXPERT_BLOCK = 256


def _expert_kernel(be_ref, bv_ref, xs_ref, wg_ref, wu_ref, wd_ref, y_ref):
    i = pl.program_id(0)

    @pl.when(bv_ref[i] > 0)
    def _():
        xb = xs_ref[...].astype(BF16)
        g = jnp.dot(xb, wg_ref[0], preferred_element_type=F32)
        u = jnp.dot(xb, wu_ref[0], preferred_element_type=F32)
        hb = (g * jax.nn.sigmoid(g) * u).astype(BF16)
        y_ref[...] = jnp.dot(hb, wd_ref[0], preferred_element_type=F32)

    @pl.when(bv_ref[i] == 0)
    def _():
        y_ref[...] = jnp.zeros(y_ref.shape, F32)


def _experts(xs, blk_expert, blk_valid, wg, wu, wd):
    n_slots = xs.shape[0]
    nb = n_slots // EXPERT_BLOCK
    grid_spec = pltpu.PrefetchScalarGridSpec(
        num_scalar_prefetch=2, grid=(nb,),
        in_specs=[pl.BlockSpec((EXPERT_BLOCK, D_MODEL), lambda i, be, bv: (i, 0)),
                  pl.BlockSpec((1, D_MODEL, D_EXPERT), lambda i, be, bv: (be[i], 0, 0)),
                  pl.BlockSpec((1, D_MODEL, D_EXPERT), lambda i, be, bv: (be[i], 0, 0)),
                  pl.BlockSpec((1, D_EXPERT, D_MODEL), lambda i, be, bv: (be[i], 0, 0))],
        out_specs=pl.BlockSpec((EXPERT_BLOCK, D_MODEL), lambda i, be, bv: (i, 0)))
    return pl.pallas_call(
        _expert_kernel, grid_spec=grid_spec,
        out_shape=jax.ShapeDtypeStruct((n_slots, D_MODEL), F32),
        compiler_params=_cparams(("arbitrary",)), name="experts",
    )(blk_expert, blk_valid, xs, wg, wu, wd)


def _combine_kernel(x_ref, ya_ref, yb_ref, route_ref, g_ref, o_ref, *, final):
    route = route_ref[...]
    wa = route[:, 2:3]
    wb = route[:, 3:4]
    x = x_ref[...] + (ya_ref[...] * wa + yb_ref[...] * wb)
    if final:
        x = _rms(x, g_ref[...])
    o_ref[...] = x


def _combine(x, yg, route, g, final, tm):
    T = x.shape[0]
    nt = T // tm
    return pl.pallas_call(
        functools.partial(_combine_kernel, final=final), grid=(nt,),
        in_specs=[pl.BlockSpec((tm, D_MODEL), lambda i: (i, 0)),
                  pl.BlockSpec((tm, D_MODEL), lambda i: (i, 0)),
                  pl.BlockSpec((tm, D_MODEL), lambda i: (i + nt, 0)),
                  pl.BlockSpec((tm, ROUTE_LANES), lambda i: (i, 0)),
                  pl.BlockSpec((1, D_MODEL), lambda i: (0, 0))],
        out_specs=pl.BlockSpec((tm, D_MODEL), lambda i: (i, 0)),
        out_shape=jax.ShapeDtypeStruct((T, D_MODEL), F32),
        compiler_params=_cparams(("parallel",)), name="combine",
    )(x, yg, yg, route, g)


def _dispatch_plan(route, T):
    A = T * TOP_K
    flat_e = route[:, 0:TOP_K].astype(jnp.int32).reshape(A)
    order = jnp.argsort(flat_e)
    e_sorted = flat_e[order]
    counts = jnp.zeros((N_EXPERTS,), jnp.int32).at[flat_e].add(1)
    padded = (counts + EXPERT_BLOCK - 1) // EXPERT_BLOCK * EXPERT_BLOCK
    pad_end = jnp.cumsum(padded)
    pad_start = pad_end - padded
    start = jnp.cumsum(counts) - counts
    slot_sorted = pad_start[e_sorted] + jnp.arange(A, dtype=jnp.int32) - start[e_sorted]
    n_blocks = A // EXPERT_BLOCK + N_EXPERTS
    n_slots = n_blocks * EXPERT_BLOCK
    slot_tok = jnp.zeros((n_slots,), jnp.int32).at[slot_sorted].set((order // TOP_K).astype(jnp.int32))
    slot_of = jnp.zeros((A,), jnp.int32).at[order].set(slot_sorted)
    blk_start = jnp.arange(n_blocks, dtype=jnp.int32) * EXPERT_BLOCK
    blk_expert = jnp.minimum(jnp.searchsorted(pad_end, blk_start, side='right'),
                             N_EXPERTS - 1).astype(jnp.int32)
    blk_valid = (blk_start < pad_end[-1]).astype(jnp.int32)
    pair = slot_of.reshape(T, TOP_K)
    comb_idx = jnp.concatenate([pair[:, 0], pair[:, 1]])
    return slot_tok, blk_expert, blk_valid, comb_idx


def _rope_tables(seq, dim, period, first):
    half = dim // 2
    inv = ROPE_THETA ** (-jnp.arange(0, dim, 2, dtype=F32) / dim)
    ang = jnp.arange(seq, dtype=F32)[:, None] * inv[None, :]
    cos, sin = jnp.cos(ang), jnp.sin(ang)
    d = jnp.arange(LANES) % period - first
    in_a = (d >= 0) & (d < half)
    in_b = (d >= half) & (d < dim)
    idx = jnp.clip(jnp.where(in_b, d - half, d), 0, half - 1)
    c = jnp.where((in_a | in_b)[None, :], cos[:, idx], 1.0)
    sa = jnp.where(in_a[None, :], -sin[:, idx], 0.0)
    sb = jnp.where(in_b[None, :], sin[:, idx], 0.0)
    return c, sa, sb


def _pad_heads(w, width):
    k = w.shape[0]
    w = w.reshape(k, MLA_HEADS, width)
    return jnp.pad(w, ((0, 0), (0, 0), (0, MLA_SLOT - width))).reshape(k, MLA_PAD)


def _prep_layer(l, norm1_g, w_in, q_norm_g, w_uq, kv_norm_g, w_uk, w_uv, w_pa, w_pb, w_o,
                norm2_g, w_rg, b_rg, w_re, b_re, w_e_gate, w_e_up, w_e_down):
    w = w_in[l]
    o = 0
    parts = []
    for n in (DIL_WIDTH, DIL_WIDTH, DIL_WIDTH, Q_LORA, KV_LORA, MLA_ROPE, 2 * D_MODEL):
        parts.append(w[:, o:o + n])
        o += n
    wq, wk, wv, wcq, wckv, wkr, wg = parts
    wkr_pad = jnp.pad(wkr, ((0, 0), (MLA_NOPE, MLA_SLOT - MLA_NOPE - MLA_ROPE)))
    wr = jnp.pad(jnp.concatenate([w_re[l], w_rg[l]], axis=1),
                 ((0, 0), (0, ROUTE_LANES - N_EXPERTS - N_GROUPS)))
    wrh = wr.astype(BF16)
    wrl = (wr - wrh.astype(F32)).astype(BF16)
    br = jnp.pad(jnp.concatenate([b_re[l], b_rg[l]]), (0, ROUTE_LANES - N_EXPERTS - N_GROUPS))
    return dict(
        g1=norm1_g[l][None, :], wq=wq.astype(BF16), wk=wk.astype(BF16), wv=wv.astype(BF16),
        wcq=wcq.astype(BF16), wckv=wckv.astype(BF16), wkr=wkr_pad.astype(BF16), wg=wg.astype(BF16),
        qn=q_norm_g[l][None, :], kvn=kv_norm_g[l][None, :],
        wuq=_pad_heads(w_uq[l], MLA_QK).astype(BF16), wuk=_pad_heads(w_uk[l], MLA_NOPE).astype(BF16),
        wuv=_pad_heads(w_uv[l], MLA_V).astype(BF16),
        wpa=w_pa[l].astype(BF16), wpb=w_pb[l].astype(BF16), wo=w_o[l].astype(BF16),
        g2=norm2_g[l][None, :], wrh=wrh, wrl=wrl, br=br[None, :].astype(F32),
        weg=w_e_gate[l].astype(BF16), weu=w_e_up[l].astype(BF16), wed=w_e_down[l].astype(BF16))


def _trunk(x3, layers, final_g, tm=256):
    batch, seq, _ = x3.shape
    T = batch * seq
    x = x3.reshape(T, D_MODEL)
    tabs = _rope_tables(seq, ROT_DIM, HEAD_DIM, 0) + _rope_tables(seq, MLA_ROPE, MLA_SLOT, MLA_NOPE)
    fg = final_g[None, :]
    for l, lw in enumerate(layers):
        qa, ka, va, qm, km, vm, gates = _inproj(x, lw, tabs, seq, tm)
        dil = []
        for g, (_, d) in enumerate(DIL_GROUPS):
            dil.extend(_dilated_group(qa, ka, va, batch, seq, g, d))
        ob = _mla(qm, km, vm, batch, seq)
        xm, h2, route = _outproj(x, dil, ob, gates, lw, tm)
        slot_tok, blk_expert, blk_valid, comb_idx = _dispatch_plan(route, T)
        xs = _gather_rows(h2, slot_tok)
        ys = _experts(xs, blk_expert, blk_valid, lw['weg'], lw['weu'], lw['wed'])
        yg = _gather_rows(ys, comb_idx)
        x = _combine(xm, yg, route, fg, l == len(layers) - 1, tm)
    return x.reshape(batch, seq, D_MODEL)


def kernel(x_prompt, x_sample, norm1_g, w_in, q_norm_g, w_uq, kv_norm_g, w_uk, w_uv, w_pa, w_pb, w_o,
           norm2_g, w_rg, b_rg, w_re, b_re, w_e_gate, w_e_up, w_e_down, final_g):
    layers = [_prep_layer(l, norm1_g, w_in, q_norm_g, w_uq, kv_norm_g, w_uk, w_uv, w_pa, w_pb, w_o,
                          norm2_g, w_rg, b_rg, w_re, b_re, w_e_gate, w_e_up, w_e_down)
              for l in range(DEPTH)]
    return (_trunk(x_prompt, layers, final_g), _trunk(x_sample, layers, final_g))
```

```python
import functools

import jax
import jax.numpy as jnp
from jax import lax
from jax.experimental import pallas as pl
from jax.experimental.pallas import tpu as pltpu
from jax.experimental.pallas import tpu_sc as plsc

D_MODEL = 1024
DEPTH = 2
HEAD_DIM = 64
DIL_GROUPS = ((128, 1), (512, 4), (2048, 16))
DIL_HEADS_PER_GROUP = 4
DIL_WIDTH = 768
DIL_OUT = 256
DIL_SCALE = HEAD_DIM ** -0.5
ROT_DIM = 16
ROPE_THETA = 500000.0
DIL_HALF = 64

MLA_HEADS = 8
MLA_NOPE = 64
MLA_ROPE = 32
MLA_QK = 96
MLA_V = 64
MLA_OUT = 512
MLA_SCALE = MLA_QK ** -0.5
Q_LORA = 256
KV_LORA = 128

N_GROUPS = 8
EXPERTS_PER_GROUP = 8
N_EXPERTS = 64
TOP_K = 2
D_EXPERT = 512
EPS = 1e-6

LANES = 128
MLA_SLOT = LANES
MLA_PAD = MLA_HEADS * MLA_SLOT
ROUTE_LANES = LANES
VMEM_LIMIT = 56 * 1024 * 1024

BF16 = jnp.bfloat16
F32 = jnp.float32
NEG = -1e30


def _cparams(sem):
    return pltpu.CompilerParams(dimension_semantics=sem, vmem_limit_bytes=VMEM_LIMIT)


def _rms(t, g):
    return t * lax.rsqrt(jnp.mean(t * t, axis=-1, keepdims=True) + EPS) * g


def _rope_chunk(t, c, sa, sb, shift):
    return t * c + pltpu.roll(t, LANES - shift, 1) * sa + pltpu.roll(t, shift, 1) * sb


def _inproj_kernel(x_ref, g1_ref, wq_ref, wk_ref, wv_ref, wcq_ref, wckv_ref, wkr_ref, wg_ref,
                   qn_ref, kvn_ref, wuq_ref, wuk_ref, wuv_ref,
                   ca_ref, saa_ref, sab_ref, cb_ref, sba_ref, sbb_ref,
                   qa_ref, ka_ref, va_ref, qm_ref, km_ref, vm_ref, gate_ref):
    x = x_ref[...]
    h = _rms(x, g1_ref[...]).astype(BF16)

    ca, saa, sab = ca_ref[...], saa_ref[...], sab_ref[...]
    q = jnp.dot(h, wq_ref[...], preferred_element_type=F32)
    k = jnp.dot(h, wk_ref[...], preferred_element_type=F32)
    for j in range(DIL_WIDTH // LANES):
        sl = slice(j * LANES, (j + 1) * LANES)
        qa_ref[:, sl] = (_rope_chunk(q[:, sl], ca, saa, sab, ROT_DIM // 2) * DIL_SCALE).astype(BF16)
        ka_ref[:, sl] = _rope_chunk(k[:, sl], ca, saa, sab, ROT_DIM // 2).astype(BF16)
    va_ref[...] = jnp.dot(h, wv_ref[...], preferred_element_type=F32).astype(BF16)

    cb, sba, sbb = cb_ref[...], sba_ref[...], sbb_ref[...]
    cq = jnp.dot(h, wcq_ref[...], preferred_element_type=F32)
    cqn = _rms(cq, qn_ref[...]).astype(BF16)
    qm = jnp.dot(cqn, wuq_ref[...], preferred_element_type=F32) * MLA_SCALE
    ckv = jnp.dot(h, wckv_ref[...], preferred_element_type=F32)
    c = _rms(ckv, kvn_ref[...]).astype(BF16)
    kn = jnp.dot(c, wuk_ref[...], preferred_element_type=F32)
    vv = jnp.dot(c, wuv_ref[...], preferred_element_type=F32)
    kr = jnp.dot(h, wkr_ref[...], preferred_element_type=F32)
    kr = _rope_chunk(kr, cb, sba, sbb, MLA_ROPE // 2)
    lane = lax.broadcasted_iota(jnp.int32, (1, LANES), 1)
    ones_col = jnp.where(lane == MLA_V, 1.0, 0.0).astype(F32)
    for j in range(MLA_HEADS):
        sl = slice(j * LANES, (j + 1) * LANES)
        qm_ref[:, sl] = _rope_chunk(qm[:, sl], cb, sba, sbb, MLA_ROPE // 2).astype(BF16)
        km_ref[:, sl] = (kn[:, sl] + kr).astype(BF16)
        vm_ref[:, sl] = (vv[:, sl] + ones_col).astype(BF16)

    gate_ref[...] = jax.nn.sigmoid(jnp.dot(h, wg_ref[...], preferred_element_type=F32))


def _inproj(x, lw, tabs, seq, tm):
    T = x.shape[0]
    nt = seq // tm
    row = lambda i: (i, 0)
    const = lambda i: (0, 0)
    tab = lambda i: (i % nt, 0)

    def wspec(a):
        return pl.BlockSpec(a.shape, const)

    weights = [lw['g1'], lw['wq'], lw['wk'], lw['wv'], lw['wcq'], lw['wckv'], lw['wkr'], lw['wg'],
               lw['qn'], lw['kvn'], lw['wuq'], lw['wuk'], lw['wuv']]
    in_specs = ([pl.BlockSpec((tm, D_MODEL), row)] + [wspec(a) for a in weights]
                + [pl.BlockSpec((tm, LANES), tab)] * 6)
    out_shape = [jax.ShapeDtypeStruct((T, DIL_WIDTH), BF16)] * 3 \
        + [jax.ShapeDtypeStruct((T, MLA_PAD), BF16)] * 3 \
        + [jax.ShapeDtypeStruct((T, 2 * D_MODEL), F32)]
    out_specs = [pl.BlockSpec((tm, DIL_WIDTH), row)] * 3 + [pl.BlockSpec((tm, MLA_PAD), row)] * 3 \
        + [pl.BlockSpec((tm, 2 * D_MODEL), row)]
    return pl.pallas_call(
        _inproj_kernel, grid=(T // tm,), in_specs=in_specs, out_specs=out_specs, out_shape=out_shape,
        compiler_params=_cparams(("parallel",)), name="inproj",
    )(x, *weights, *tabs)


def _dil_kernel(q_ref, kp_ref, kc_ref, kn_ref, vp_ref, vc_ref, vn_ref, o_ref, lse_ref,
                kbuf, vbuf, *, tq, sub_len):
    i = pl.program_id(2)
    qb = LANES
    kbuf[0:qb] = kp_ref[0]
    kbuf[qb:qb + tq] = kc_ref[0]
    kbuf[qb + tq:] = kn_ref[0]
    vbuf[0:qb] = vp_ref[0]
    vbuf[qb:qb + tq] = vc_ref[0]
    vbuf[qb + tq:] = vn_ref[0]
    head = lax.broadcasted_iota(jnp.int32, (1, DIL_OUT), 1) // HEAD_DIM
    rel = (lax.broadcasted_iota(jnp.int32, (qb, 2 * qb), 1) - DIL_HALF
           - lax.broadcasted_iota(jnp.int32, (qb, 2 * qb), 0))
    band = jnp.abs(rel) <= DIL_HALF
    kcol = lax.broadcasted_iota(jnp.int32, (1, 2 * qb), 1) - DIL_HALF
    for j in range(tq // qb):
        q = q_ref[0, j * qb:(j + 1) * qb, :]
        k = kbuf[DIL_HALF + j * qb:DIL_HALF + j * qb + 2 * qb, :]
        v = vbuf[DIL_HALF + j * qb:DIL_HALF + j * qb + 2 * qb, :]
        kpos = kcol + (i * tq + j * qb)
        valid = band & (kpos >= 0) & (kpos < sub_len)
        o_acc = jnp.zeros((qb, DIL_OUT), F32)
        l_acc = jnp.zeros((qb, DIL_OUT), F32)
        for hd in range(DIL_HEADS_PER_GROUP):
            hm = head == hd
            qh = jnp.where(hm, q, jnp.zeros_like(q))
            s = lax.dot_general(qh, k, (((1,), (1,)), ((), ())), preferred_element_type=F32)
            s = jnp.where(valid, s, NEG)
            m = jnp.max(s, axis=-1, keepdims=True)
            p = jnp.exp(s - m)
            den = jnp.sum(p, axis=-1, keepdims=True)
            oh = jnp.dot(p.astype(BF16), v, preferred_element_type=F32) * (1.0 / den)
            o_acc = jnp.where(hm, oh, o_acc)
            l_acc = jnp.where(hm, m + jnp.log(den), l_acc)
        o_ref[0, j * qb:(j + 1) * qb, :] = o_acc
        lse_ref[0, j * qb:(j + 1) * qb, :] = l_acc


def _dilated_group(qa, ka, va, batch, seq, g, dil):
    L = seq // dil
    tq = min(L, 512)
    nq = L // tq
    hb = tq // LANES
    nhb = L // LANES
    qv = qa.reshape(batch, L, dil * DIL_WIDTH)
    kv = ka.reshape(batch, L, dil * DIL_WIDTH)
    vv = va.reshape(batch, L, dil * DIL_WIDTH)
    ncol = DIL_WIDTH // DIL_OUT
    cur = lambda b, r, i: (b, i, r * ncol + g)
    prev = lambda b, r, i: (b, jnp.maximum(i * hb - 1, 0), r * ncol + g)
    nxt = lambda b, r, i: (b, jnp.minimum((i + 1) * hb, nhb - 1), r * ncol + g)
    blk = (1, tq, DIL_OUT)
    halo = (1, LANES, DIL_OUT)
    out_map = lambda b, r, i: (b, i, r)
    o, lse = pl.pallas_call(
        functools.partial(_dil_kernel, tq=tq, sub_len=L),
        grid=(batch, dil, nq),
        in_specs=[pl.BlockSpec(blk, cur),
                  pl.BlockSpec(halo, prev), pl.BlockSpec(blk, cur), pl.BlockSpec(halo, nxt),
                  pl.BlockSpec(halo, prev), pl.BlockSpec(blk, cur), pl.BlockSpec(halo, nxt)],
        out_specs=[pl.BlockSpec(blk, out_map), pl.BlockSpec(blk, out_map)],
        out_shape=[jax.ShapeDtypeStruct((batch, L, dil * DIL_OUT), F32)] * 2,
        scratch_shapes=[pltpu.VMEM((tq + 2 * LANES, DIL_OUT), BF16)] * 2,
        compiler_params=_cparams(("parallel", "parallel", "parallel")), name=f"dilated_g{g}",
    )(qv, kv, kv, kv, vv, vv, vv)
    return o.reshape(batch * seq, DIL_OUT), lse.reshape(batch * seq, DIL_OUT)


def _mla_kernel(q_ref, k_ref, v_ref, o_ref, m_sc, acc_sc, *, tk, nk):
    outs = []
    for hh in range(2):
        sl = slice(hh * LANES, (hh + 1) * LANES)
        q = q_ref[0, :, sl]
        m_sc[...] = jnp.full(m_sc.shape, NEG, F32)
        acc_sc[...] = jnp.zeros(acc_sc.shape, F32)

        def body(j, carry):
            off = pl.multiple_of(j * tk, tk)
            k = k_ref[0, pl.ds(off, tk), sl]
            v = v_ref[0, pl.ds(off, tk), sl]
            s = lax.dot_general(q, k, (((1,), (1,)), ((), ())), preferred_element_type=F32)
            m_old = m_sc[...]
            m_new = jnp.maximum(m_old, jnp.max(s, axis=-1, keepdims=True))
            p = jnp.exp(s - m_new).astype(BF16)
            acc_sc[...] = acc_sc[...] * jnp.exp(m_old - m_new) + jnp.dot(p, v, preferred_element_type=F32)
            m_sc[...] = m_new
            return carry

        lax.fori_loop(0, nk, body, 0)
        acc = acc_sc[...]
        outs.append(acc * (1.0 / acc[:, MLA_V:MLA_V + 1]))
    lane = lax.broadcasted_iota(jnp.int32, (1, LANES), 1)
    o_ref[0] = jnp.where(lane < MLA_V, outs[0], pltpu.roll(outs[1], MLA_V, 1)).astype(o_ref.dtype)


def _mla(qm, km, vm, batch, seq):
    tq = min(seq, 256)
    tk = min(seq, 512)
    q3 = qm.reshape(batch, seq, MLA_PAD)
    k3 = km.reshape(batch, seq, MLA_PAD)
    v3 = vm.reshape(batch, seq, MLA_PAD)
    o = pl.pallas_call(
        functools.partial(_mla_kernel, tk=tk, nk=seq // tk),
        grid=(batch, MLA_HEADS // 2, seq // tq),
        in_specs=[pl.BlockSpec((1, tq, 2 * LANES), lambda b, h, i: (b, i, h)),
                  pl.BlockSpec((1, seq, 2 * LANES), lambda b, h, i: (b, 0, h)),
                  pl.BlockSpec((1, seq, 2 * LANES), lambda b, h, i: (b, 0, h))],
        out_specs=pl.BlockSpec((1, tq, LANES), lambda b, h, i: (b, i, h)),
        out_shape=jax.ShapeDtypeStruct((batch, seq, MLA_OUT), BF16),
        scratch_shapes=[pltpu.VMEM((tq, 1), F32), pltpu.VMEM((tq, LANES), F32)],
        compiler_params=_cparams(("parallel", "parallel", "arbitrary")), name="mla",
    )(q3, k3, v3)
    return o.reshape(batch * seq, MLA_OUT)


def _outproj_kernel(x_ref, o0_ref, l0_ref, o1_ref, l1_ref, o2_ref, l2_ref, ob_ref, gate_ref,
                    wpa_ref, wpb_ref, wo_ref, g2_ref, wrh_ref, wrl_ref, br_ref,
                    xm_ref, h2_ref, route_ref):
    l0, l1, l2 = l0_ref[...], l1_ref[...], l2_ref[...]
    m = jnp.maximum(jnp.maximum(l0, l1), l2)
    w0, w1, w2 = jnp.exp(l0 - m), jnp.exp(l1 - m), jnp.exp(l2 - m)
    oa = (w0 * o0_ref[...] + w1 * o1_ref[...] + w2 * o2_ref[...]) / (w0 + w1 + w2)
    pa = jnp.dot(oa.astype(BF16), wpa_ref[...], preferred_element_type=F32)
    pb = jnp.dot(ob_ref[...], wpb_ref[...], preferred_element_type=F32)
    merged = gate_ref[:, :D_MODEL] * pa + gate_ref[:, D_MODEL:] * pb
    xm = x_ref[...] + jnp.dot(merged.astype(BF16), wo_ref[...], preferred_element_type=F32)
    xm_ref[...] = xm
    h2 = _rms(xm, g2_ref[...])
    h2_ref[...] = h2

    hi = h2.astype(BF16)
    lo = (h2 - hi.astype(F32)).astype(BF16)
    lg = (jnp.dot(hi, wrh_ref[...], preferred_element_type=F32)
          + jnp.dot(lo, wrh_ref[...], preferred_element_type=F32)
          + jnp.dot(hi, wrl_ref[...], preferred_element_type=F32)) + br_ref[...]
    lane = lax.broadcasted_iota(jnp.int32, lg.shape, 1)
    lanef = lane.astype(F32)
    big = float(ROUTE_LANES)
    gmask = (lane >= N_EXPERTS) & (lane < N_EXPERTS + N_GROUPS)
    gl = jnp.where(gmask, lg, NEG)
    gmax = jnp.max(gl, axis=-1, keepdims=True)
    gidx = jnp.min(jnp.where(gl == gmax, lanef, big), axis=-1, keepdims=True) - float(N_EXPERTS)
    p_grp = 1.0 / jnp.sum(jnp.where(gmask, jnp.exp(gl - gmax), 0.0), axis=-1, keepdims=True)
    emask = (lane // EXPERTS_PER_GROUP).astype(F32) == gidx
    el = jnp.where(emask, lg, NEG)
    m1 = jnp.max(el, axis=-1, keepdims=True)
    i1 = jnp.min(jnp.where(el == m1, lanef, big), axis=-1, keepdims=True)
    el2 = jnp.where(lanef == i1, NEG, el)
    m2 = jnp.max(el2, axis=-1, keepdims=True)
    i2 = jnp.min(jnp.where(el2 == m2, lanef, big), axis=-1, keepdims=True)
    r = jnp.exp(m2 - m1)
    wa = p_grp / (1.0 + r)
    wb = p_grp * r / (1.0 + r)
    route_ref[...] = jnp.where(lane == 0, i1, jnp.where(lane == 1, i2,
                               jnp.where(lane == 2, wa, jnp.where(lane == 3, wb, 0.0))))


def _outproj(x, dil_outs, ob, gates, lw, tm):
    T = x.shape[0]
    row = lambda i: (i, 0)
    const = lambda i: (0, 0)
    weights = [lw['wpa'], lw['wpb'], lw['wo'], lw['g2'], lw['wrh'], lw['wrl'], lw['br']]
    acts = [x] + list(dil_outs) + [ob, gates]
    in_specs = [pl.BlockSpec((tm, a.shape[1]), row) for a in acts] \
        + [pl.BlockSpec(a.shape, const) for a in weights]
    return pl.pallas_call(
        _outproj_kernel, grid=(T // tm,), in_specs=in_specs,
        out_specs=[pl.BlockSpec((tm, D_MODEL), row), pl.BlockSpec((tm, D_MODEL), row),
                   pl.BlockSpec((tm, ROUTE_LANES), row)],
        out_shape=[jax.ShapeDtypeStruct((T, D_MODEL), F32), jax.ShapeDtypeStruct((T, D_MODEL), F32),
                   jax.ShapeDtypeStruct((T, ROUTE_LANES), F32)],
        compiler_params=_cparams(("parallel",)), name="outproj_router",
    )(*acts, *weights)


GATHER_WINDOW = 32


def _gather_rows(src, idx):
    m = idx.shape[0]
    width = src.shape[1]
    info = plsc.get_sparse_core_info()
    n_workers = info.num_cores * info.num_subcores
    per_w = m // n_workers
    n_pairs = per_w // (2 * GATHER_WINDOW)
    assert n_pairs * 2 * GATHER_WINDOW * n_workers == m
    mesh = plsc.VectorSubcoreMesh(core_axis_name="core", subcore_axis_name="subcore")

    @functools.partial(
        pl.kernel, out_type=jax.ShapeDtypeStruct((m, width), src.dtype), mesh=mesh, name="gather_rows",
        scratch_types=[pltpu.VMEM((per_w,), jnp.int32),
                       pltpu.VMEM((2, GATHER_WINDOW, width), src.dtype),
                       pltpu.SemaphoreType.DMA((2,))])
    def gather(src_hbm, idx_hbm, out_hbm, idx_v, rows_v, sems):
        wid = lax.axis_index("subcore") * info.num_cores + lax.axis_index("core")
        base = wid * per_w
        pltpu.sync_copy(idx_hbm.at[pl.ds(base, per_w)], idx_v)

        def fetch(c, b):
            return pltpu.make_async_copy(src_hbm.at[idx_v.at[pl.ds(c * GATHER_WINDOW, GATHER_WINDOW)]],
                                         rows_v.at[b], sems.at[b])

        def flush(c, b):
            pltpu.sync_copy(rows_v.at[b], out_hbm.at[pl.ds(base + c * GATHER_WINDOW, GATHER_WINDOW)])

        fetch(0, 0).start()

        @pl.loop(0, n_pairs)
        def _(p):
            c = 2 * p
            fetch(c + 1, 1).start()
            fetch(c, 0).wait()
            flush(c, 0)

            @pl.when(p + 1 < n_pairs)
            def _():
                fetch(c + 2, 0).start()

            fetch(c + 1, 1).wait()
            flush(c + 1, 1)

    return gather(src, idx)


EXPERT_BLOCK = 256


def _expert_kernel(be_ref, bv_ref, xs_ref, wg_ref, wu_ref, wd_ref, y_ref):
    i = pl.program_id(0)

    @pl.when(bv_ref[i] > 0)
    def _():
        xb = xs_ref[...].astype(BF16)
        g = jnp.dot(xb, wg_ref[0], preferred_element_type=F32)
        u = jnp.dot(xb, wu_ref[0], preferred_element_type=F32)
        hb = (g * jax.nn.sigmoid(g) * u).astype(BF16)
        y_ref[...] = jnp.dot(hb, wd_ref[0], preferred_element_type=F32)

    @pl.when(bv_ref[i] == 0)
    def _():
        y_ref[...] = jnp.zeros(y_ref.shape, F32)


def _experts(xs, blk_expert, blk_valid, wg, wu, wd):
    n_slots = xs.shape[0]
    nb = n_slots // EXPERT_BLOCK
    grid_spec = pltpu.PrefetchScalarGridSpec(
        num_scalar_prefetch=2, grid=(nb,),
        in_specs=[pl.BlockSpec((EXPERT_BLOCK, D_MODEL), lambda i, be, bv: (i, 0)),
                  pl.BlockSpec((1, D_MODEL, D_EXPERT), lambda i, be, bv: (be[i], 0, 0)),
                  pl.BlockSpec((1, D_MODEL, D_EXPERT), lambda i, be, bv: (be[i], 0, 0)),
                  pl.BlockSpec((1, D_EXPERT, D_MODEL), lambda i, be, bv: (be[i], 0, 0))],
        out_specs=pl.BlockSpec((EXPERT_BLOCK, D_MODEL), lambda i, be, bv: (i, 0)))
    return pl.pallas_call(
        _expert_kernel, grid_spec=grid_spec,
        out_shape=jax.ShapeDtypeStruct((n_slots, D_MODEL), F32),
        compiler_params=_cparams(("arbitrary",)), name="experts",
    )(blk_expert, blk_valid, xs, wg, wu, wd)


def _combine_kernel(x_ref, ya_ref, yb_ref, route_ref, g_ref, o_ref, *, final):
    route = route_ref[...]
    wa = route[:, 2:3]
    wb = route[:, 3:4]
    x = x_ref[...] + (ya_ref[...] * wa + yb_ref[...] * wb)
    if final:
        x = _rms(x, g_ref[...])
    o_ref[...] = x


def _combine(x, yg, route, g, final, tm):
    T = x.shape[0]
    nt = T // tm
    return pl.pallas_call(
        functools.partial(_combine_kernel, final=final), grid=(nt,),
        in_specs=[pl.BlockSpec((tm, D_MODEL), lambda i: (i, 0)),
                  pl.BlockSpec((tm, D_MODEL), lambda i: (i, 0)),
                  pl.BlockSpec((tm, D_MODEL), lambda i: (i + nt, 0)),
                  pl.BlockSpec((tm, ROUTE_LANES), lambda i: (i, 0)),
                  pl.BlockSpec((1, D_MODEL), lambda i: (0, 0))],
        out_specs=pl.BlockSpec((tm, D_MODEL), lambda i: (i, 0)),
        out_shape=jax.ShapeDtypeStruct((T, D_MODEL), F32),
        compiler_params=_cparams(("parallel",)), name="combine",
    )(x, yg, yg, route, g)


def _dispatch_plan(route, T):
    A = T * TOP_K
    flat_e = route[:, 0:TOP_K].astype(jnp.int32).reshape(A)
    order = jnp.argsort(flat_e)
    e_sorted = flat_e[order]
    counts = jnp.zeros((N_EXPERTS,), jnp.int32).at[flat_e].add(1)
    padded = (counts + EXPERT_BLOCK - 1) // EXPERT_BLOCK * EXPERT_BLOCK
    pad_end = jnp.cumsum(padded)
    pad_start = pad_end - padded
    start = jnp.cumsum(counts) - counts
    slot_sorted = pad_start[e_sorted] + jnp.arange(A, dtype=jnp.int32) - start[e_sorted]
    n_blocks = A // EXPERT_BLOCK + N_EXPERTS
    n_slots = n_blocks * EXPERT_BLOCK
    slot_tok = jnp.zeros((n_slots,), jnp.int32).at[slot_sorted].set((order // TOP_K).astype(jnp.int32))
    slot_of = jnp.zeros((A,), jnp.int32).at[order].set(slot_sorted)
    blk_start = jnp.arange(n_blocks, dtype=jnp.int32) * EXPERT_BLOCK
    blk_expert = jnp.minimum(jnp.searchsorted(pad_end, blk_start, side='right'),
                             N_EXPERTS - 1).astype(jnp.int32)
    blk_valid = (blk_start < pad_end[-1]).astype(jnp.int32)
    pair = slot_of.reshape(T, TOP_K)
    comb_idx = jnp.concatenate([pair[:, 0], pair[:, 1]])
    return slot_tok, blk_expert, blk_valid, comb_idx


def _rope_tables(seq, dim, period, first):
    half = dim // 2
    inv = ROPE_THETA ** (-jnp.arange(0, dim, 2, dtype=F32) / dim)
    ang = jnp.arange(seq, dtype=F32)[:, None] * inv[None, :]
    cos, sin = jnp.cos(ang), jnp.sin(ang)
    d = jnp.arange(LANES) % period - first
    in_a = (d >= 0) & (d < half)
    in_b = (d >= half) & (d < dim)
    idx = jnp.clip(jnp.where(in_b, d - half, d), 0, half - 1)
    c = jnp.where((in_a | in_b)[None, :], cos[:, idx], 1.0)
    sa = jnp.where(in_a[None, :], -sin[:, idx], 0.0)
    sb = jnp.where(in_b[None, :], sin[:, idx], 0.0)
    return c, sa, sb


def _pad_heads(w, width):
    k = w.shape[0]
    w = w.reshape(k, MLA_HEADS, width)
    return jnp.pad(w, ((0, 0), (0, 0), (0, MLA_SLOT - width))).reshape(k, MLA_PAD)


def _prep_layer(l, norm1_g, w_in, q_norm_g, w_uq, kv_norm_g, w_uk, w_uv, w_pa, w_pb, w_o,
                norm2_g, w_rg, b_rg, w_re, b_re, w_e_gate, w_e_up, w_e_down):
    w = w_in[l]
    o = 0
    parts = []
    for n in (DIL_WIDTH, DIL_WIDTH, DIL_WIDTH, Q_LORA, KV_LORA, MLA_ROPE, 2 * D_MODEL):
        parts.append(w[:, o:o + n])
        o += n
    wq, wk, wv, wcq, wckv, wkr, wg = parts
    wkr_pad = jnp.pad(wkr, ((0, 0), (MLA_NOPE, MLA_SLOT - MLA_NOPE - MLA_ROPE)))
    wr = jnp.pad(jnp.concatenate([w_re[l], w_rg[l]], axis=1),
                 ((0, 0), (0, ROUTE_LANES - N_EXPERTS - N_GROUPS)))
    wrh = wr.astype(BF16)
    wrl = (wr - wrh.astype(F32)).astype(BF16)
    br = jnp.pad(jnp.concatenate([b_re[l], b_rg[l]]), (0, ROUTE_LANES - N_EXPERTS - N_GROUPS))
    return dict(
        g1=norm1_g[l][None, :], wq=wq.astype(BF16), wk=wk.astype(BF16), wv=wv.astype(BF16),
        wcq=wcq.astype(BF16), wckv=wckv.astype(BF16), wkr=wkr_pad.astype(BF16), wg=wg.astype(BF16),
        qn=q_norm_g[l][None, :], kvn=kv_norm_g[l][None, :],
        wuq=_pad_heads(w_uq[l], MLA_QK).astype(BF16), wuk=_pad_heads(w_uk[l], MLA_NOPE).astype(BF16),
        wuv=_pad_heads(w_uv[l], MLA_V).astype(BF16),
        wpa=w_pa[l].astype(BF16), wpb=w_pb[l].astype(BF16), wo=w_o[l].astype(BF16),
        g2=norm2_g[l][None, :], wrh=wrh, wrl=wrl, br=br[None, :].astype(F32),
        weg=w_e_gate[l].astype(BF16), weu=w_e_up[l].astype(BF16), wed=w_e_down[l].astype(BF16))


def _trunk(x3, layers, final_g, tm=256):
    batch, seq, _ = x3.shape
    T = batch * seq
    x = x3.reshape(T, D_MODEL)
    tabs = _rope_tables(seq, ROT_DIM, HEAD_DIM, 0) + _rope_tables(seq, MLA_ROPE, MLA_SLOT, MLA_NOPE)
    fg = final_g[None, :]
    for l, lw in enumerate(layers):
        qa, ka, va, qm, km, vm, gates = _inproj(x, lw, tabs, seq, tm)
        dil = []
        for g, (_, d) in enumerate(DIL_GROUPS):
            dil.extend(_dilated_group(qa, ka, va, batch, seq, g, d))
        ob = _mla(qm, km, vm, batch, seq)
        xm, h2, route = _outproj(x, dil, ob, gates, lw, tm)
        slot_tok, blk_expert, blk_valid, comb_idx = _dispatch_plan(route, T)
        xs = _gather_rows(h2, slot_tok)
        ys = _experts(xs, blk_expert, blk_valid, lw['weg'], lw['weu'], lw['wed'])
        yg = _gather_rows(ys, comb_idx)
        x = _combine(xm, yg, route, fg, l == len(layers) - 1, tm)
    return x.reshape(batch, seq, D_MODEL)


def kernel(x_prompt, x_sample, norm1_g, w_in, q_norm_g, w_uq, kv_norm_g, w_uk, w_uv, w_pa, w_pb, w_o,
           norm2_g, w_rg, b_rg, w_re, b_re, w_e_gate, w_e_up, w_e_down, final_g):
    layers = [_prep_layer(l, norm1_g, w_in, q_norm_g, w_uq, kv_norm_g, w_uk, w_uv, w_pa, w_pb, w_o,
                          norm2_g, w_rg, b_rg, w_re, b_re, w_e_gate, w_e_up, w_e_down)
              for l in range(DEPTH)]
    return (_trunk(x_prompt, layers, final_g), _trunk(x_sample, layers, final_g))
```

```python
import functools

import jax
import jax.numpy as jnp
from jax import lax
from jax.experimental import pallas as pl
from jax.experimental.pallas import tpu as pltpu
from jax.experimental.pallas import tpu_sc as plsc

D_MODEL = 1024
DEPTH = 2
HEAD_DIM = 64
DIL_GROUPS = ((128, 1), (512, 4), (2048, 16))
DIL_HEADS_PER_GROUP = 4
DIL_WIDTH = 768
DIL_OUT = 256
DIL_SCALE = HEAD_DIM ** -0.5
ROT_DIM = 16
ROPE_THETA = 500000.0
DIL_HALF = 64

MLA_HEADS = 8
MLA_NOPE = 64
MLA_ROPE = 32
MLA_QK = 96
MLA_V = 64
MLA_OUT = 512
MLA_SCALE = MLA_QK ** -0.5
Q_LORA = 256
KV_LORA = 128

N_GROUPS = 8
EXPERTS_PER_GROUP = 8
N_EXPERTS = 64
TOP_K = 2
D_EXPERT = 512
EPS = 1e-6

LANES = 128
MLA_SLOT = LANES
MLA_PAD = MLA_HEADS * MLA_SLOT
ROUTE_LANES = LANES
VMEM_LIMIT = 56 * 1024 * 1024

BF16 = jnp.bfloat16
F32 = jnp.float32
NEG = -1e30
LOG2E = 1.4426950408889634


def _cparams(sem):
    return pltpu.CompilerParams(dimension_semantics=sem, vmem_limit_bytes=VMEM_LIMIT)


def _rms(t, g):
    return t * lax.rsqrt(jnp.mean(t * t, axis=-1, keepdims=True) + EPS) * g


def _rope_chunk(t, c, sa, sb, shift):
    return t * c + pltpu.roll(t, LANES - shift, 1) * sa + pltpu.roll(t, shift, 1) * sb


def _inproj_kernel(x_ref, g1_ref, wq_ref, wk_ref, wv_ref, wcq_ref, wckv_ref, wkr_ref, wg_ref,
                   qn_ref, kvn_ref, wuq_ref, wuk_ref, wuv_ref,
                   ca_ref, saa_ref, sab_ref, cb_ref, sba_ref, sbb_ref,
                   qa_ref, ka_ref, va_ref, qm_ref, km_ref, vm_ref, gate_ref):
    x = x_ref[...]
    h = _rms(x, g1_ref[...]).astype(BF16)

    ca, saa, sab = ca_ref[...], saa_ref[...], sab_ref[...]
    q = jnp.dot(h, wq_ref[...], preferred_element_type=F32)
    k = jnp.dot(h, wk_ref[...], preferred_element_type=F32)
    for j in range(DIL_WIDTH // LANES):
        sl = slice(j * LANES, (j + 1) * LANES)
        qa_ref[:, sl] = (_rope_chunk(q[:, sl], ca, saa, sab, ROT_DIM // 2) * DIL_SCALE).astype(BF16)
        ka_ref[:, sl] = _rope_chunk(k[:, sl], ca, saa, sab, ROT_DIM // 2).astype(BF16)
    va_ref[...] = jnp.dot(h, wv_ref[...], preferred_element_type=F32).astype(BF16)

    cb, sba, sbb = cb_ref[...], sba_ref[...], sbb_ref[...]
    cq = jnp.dot(h, wcq_ref[...], preferred_element_type=F32)
    cqn = _rms(cq, qn_ref[...]).astype(BF16)
    qm = jnp.dot(cqn, wuq_ref[...], preferred_element_type=F32) * (MLA_SCALE * LOG2E)
    ckv = jnp.dot(h, wckv_ref[...], preferred_element_type=F32)
    c = _rms(ckv, kvn_ref[...]).astype(BF16)
    kn = jnp.dot(c, wuk_ref[...], preferred_element_type=F32)
    vv = jnp.dot(c, wuv_ref[...], preferred_element_type=F32)
    kr = jnp.dot(h, wkr_ref[...], preferred_element_type=F32)
    kr = _rope_chunk(kr, cb, sba, sbb, MLA_ROPE // 2)
    lane = lax.broadcasted_iota(jnp.int32, (1, LANES), 1)
    ones_col = jnp.where(lane == MLA_V, 1.0, 0.0).astype(F32)
    for j in range(MLA_HEADS):
        sl = slice(j * LANES, (j + 1) * LANES)
        qm_ref[:, sl] = _rope_chunk(qm[:, sl], cb, sba, sbb, MLA_ROPE // 2).astype(BF16)
        km_ref[:, sl] = (kn[:, sl] + kr).astype(BF16)
        vm_ref[:, sl] = (vv[:, sl] + ones_col).astype(BF16)

    gate_ref[...] = jax.nn.sigmoid(jnp.dot(h, wg_ref[...], preferred_element_type=F32))


def _inproj(x, lw, tabs, seq, tm):
    T = x.shape[0]
    nt = seq // tm
    row = lambda i: (i, 0)
    const = lambda i: (0, 0)
    tab = lambda i: (i % nt, 0)

    def wspec(a):
        return pl.BlockSpec(a.shape, const)

    weights = [lw['g1'], lw['wq'], lw['wk'], lw['wv'], lw['wcq'], lw['wckv'], lw['wkr'], lw['wg'],
               lw['qn'], lw['kvn'], lw['wuq'], lw['wuk'], lw['wuv']]
    in_specs = ([pl.BlockSpec((tm, D_MODEL), row)] + [wspec(a) for a in weights]
                + [pl.BlockSpec((tm, LANES), tab)] * 6)
    out_shape = [jax.ShapeDtypeStruct((T, DIL_WIDTH), BF16)] * 3 \
        + [jax.ShapeDtypeStruct((T, MLA_PAD), BF16)] * 3 \
        + [jax.ShapeDtypeStruct((T, 2 * D_MODEL), F32)]
    out_specs = [pl.BlockSpec((tm, DIL_WIDTH), row)] * 3 + [pl.BlockSpec((tm, MLA_PAD), row)] * 3 \
        + [pl.BlockSpec((tm, 2 * D_MODEL), row)]
    return pl.pallas_call(
        _inproj_kernel, grid=(T // tm,), in_specs=in_specs, out_specs=out_specs, out_shape=out_shape,
        compiler_params=_cparams(("parallel",)), name="inproj",
    )(x, *weights, *tabs)


def _dil_kernel(q_ref, kp_ref, kc_ref, kn_ref, vp_ref, vc_ref, vn_ref, o_ref, lse_ref,
                kbuf, vbuf, *, tq, sub_len):
    i = pl.program_id(2)
    qb = LANES
    kbuf[0:qb] = kp_ref[0]
    kbuf[qb:qb + tq] = kc_ref[0]
    kbuf[qb + tq:] = kn_ref[0]
    vbuf[0:qb] = vp_ref[0]
    vbuf[qb:qb + tq] = vc_ref[0]
    vbuf[qb + tq:] = vn_ref[0]
    head = lax.broadcasted_iota(jnp.int32, (1, DIL_OUT), 1) // HEAD_DIM
    rel = (lax.broadcasted_iota(jnp.int32, (qb, 2 * qb), 1) - DIL_HALF
           - lax.broadcasted_iota(jnp.int32, (qb, 2 * qb), 0))
    band = jnp.abs(rel) <= DIL_HALF
    kcol = lax.broadcasted_iota(jnp.int32, (1, 2 * qb), 1) - DIL_HALF
    for j in range(tq // qb):
        q = q_ref[0, j * qb:(j + 1) * qb, :]
        k = kbuf[DIL_HALF + j * qb:DIL_HALF + j * qb + 2 * qb, :]
        v = vbuf[DIL_HALF + j * qb:DIL_HALF + j * qb + 2 * qb, :]
        kpos = kcol + (i * tq + j * qb)
        valid = band & (kpos >= 0) & (kpos < sub_len)
        o_acc = jnp.zeros((qb, DIL_OUT), F32)
        l_acc = jnp.zeros((qb, DIL_OUT), F32)
        for hd in range(DIL_HEADS_PER_GROUP):
            hm = head == hd
            qh = jnp.where(hm, q, jnp.zeros_like(q))
            s = lax.dot_general(qh, k, (((1,), (1,)), ((), ())), preferred_element_type=F32)
            s = jnp.where(valid, s, NEG)
            m = jnp.max(s, axis=-1, keepdims=True)
            p = jnp.exp(s - m)
            den = jnp.sum(p, axis=-1, keepdims=True)
            oh = jnp.dot(p.astype(BF16), v, preferred_element_type=F32) * (1.0 / den)
            o_acc = jnp.where(hm, oh, o_acc)
            l_acc = jnp.where(hm, m + jnp.log(den), l_acc)
        o_ref[0, j * qb:(j + 1) * qb, :] = o_acc
        lse_ref[0, j * qb:(j + 1) * qb, :] = l_acc


def _dilated_group(qa, ka, va, batch, seq, g, dil):
    L = seq // dil
    tq = min(L, 512)
    nq = L // tq
    hb = tq // LANES
    nhb = L // LANES
    qv = qa.reshape(batch, L, dil * DIL_WIDTH)
    kv = ka.reshape(batch, L, dil * DIL_WIDTH)
    vv = va.reshape(batch, L, dil * DIL_WIDTH)
    ncol = DIL_WIDTH // DIL_OUT
    cur = lambda b, r, i: (b, i, r * ncol + g)
    prev = lambda b, r, i: (b, jnp.maximum(i * hb - 1, 0), r * ncol + g)
    nxt = lambda b, r, i: (b, jnp.minimum((i + 1) * hb, nhb - 1), r * ncol + g)
    blk = (1, tq, DIL_OUT)
    halo = (1, LANES, DIL_OUT)
    out_map = lambda b, r, i: (b, i, r)
    o, lse = pl.pallas_call(
        functools.partial(_dil_kernel, tq=tq, sub_len=L),
        grid=(batch, dil, nq),
        in_specs=[pl.BlockSpec(blk, cur),
                  pl.BlockSpec(halo, prev), pl.BlockSpec(blk, cur), pl.BlockSpec(halo, nxt),
                  pl.BlockSpec(halo, prev), pl.BlockSpec(blk, cur), pl.BlockSpec(halo, nxt)],
        out_specs=[pl.BlockSpec(blk, out_map), pl.BlockSpec(blk, out_map)],
        out_shape=[jax.ShapeDtypeStruct((batch, L, dil * DIL_OUT), F32)] * 2,
        scratch_shapes=[pltpu.VMEM((tq + 2 * LANES, DIL_OUT), BF16)] * 2,
        compiler_params=_cparams(("parallel", "parallel", "parallel")), name=f"dilated_g{g}",
    )(qv, kv, kv, kv, vv, vv, vv)
    return o.reshape(batch * seq, DIL_OUT), lse.reshape(batch * seq, DIL_OUT)


def _mla_kernel(q_ref, k_ref, v_ref, o_ref, m_sc, acc_sc, *, tk, nk, unroll):
    nc = tk // LANES
    m_sc[...] = jnp.full(m_sc.shape, NEG, F32)
    acc_sc[...] = jnp.zeros(acc_sc.shape, F32)

    def step(off, hh):
        sl = slice(hh * LANES, (hh + 1) * LANES)
        q = q_ref[0, :, sl]
        k = k_ref[0, pl.ds(off, tk), sl]
        v = v_ref[0, pl.ds(off, tk), sl]
        s = lax.dot_general(q, k, (((1,), (1,)), ((), ())), preferred_element_type=F32)
        cols = [s[:, c * LANES:(c + 1) * LANES] for c in range(nc)]
        m_old = m_sc[hh]
        m_new = jnp.maximum(m_old, jnp.max(functools.reduce(jnp.maximum, cols), axis=-1, keepdims=True))
        p = jnp.concatenate([jnp.exp2(c - m_new) for c in cols], axis=1).astype(BF16)
        acc_sc[hh] = acc_sc[hh] * jnp.exp2(m_old - m_new) + jnp.dot(p, v, preferred_element_type=F32)
        m_sc[hh] = m_new

    def body(j, carry):
        for u in range(unroll):
            off = pl.multiple_of((j * unroll + u) * tk, tk)
            for hh in range(2):
                step(off, hh)
        return carry

    lax.fori_loop(0, nk // unroll, body, 0)
    outs = []
    for hh in range(2):
        acc = acc_sc[hh]
        outs.append(acc * (1.0 / acc[:, MLA_V:MLA_V + 1]))
    lane = lax.broadcasted_iota(jnp.int32, (1, LANES), 1)
    o_ref[0] = jnp.where(lane < MLA_V, outs[0], pltpu.roll(outs[1], MLA_V, 1)).astype(o_ref.dtype)


def _mla(qm, km, vm, batch, seq):
    tq = min(seq, 512)
    tk = min(seq, 512)
    nk = seq // tk
    unroll = 2 if nk % 2 == 0 else 1
    q3 = qm.reshape(batch, seq, MLA_PAD)
    k3 = km.reshape(batch, seq, MLA_PAD)
    v3 = vm.reshape(batch, seq, MLA_PAD)
    resident = lambda b, h, i: (b, 0, h)
    o = pl.pallas_call(
        functools.partial(_mla_kernel, tk=tk, nk=nk, unroll=unroll),
        grid=(batch, MLA_HEADS // 2, seq // tq),
        in_specs=[pl.BlockSpec((1, tq, 2 * LANES), lambda b, h, i: (b, i, h)),
                  pl.BlockSpec((1, seq, 2 * LANES), resident, pipeline_mode=pl.Buffered(1)),
                  pl.BlockSpec((1, seq, 2 * LANES), resident, pipeline_mode=pl.Buffered(1))],
        out_specs=pl.BlockSpec((1, tq, LANES), lambda b, h, i: (b, i, h)),
        out_shape=jax.ShapeDtypeStruct((batch, seq, MLA_OUT), BF16),
        scratch_shapes=[pltpu.VMEM((2, tq, LANES), F32), pltpu.VMEM((2, tq, LANES), F32)],
        compiler_params=_cparams(("parallel", "parallel", "arbitrary")), name="mla",
    )(q3, k3, v3)
    return o.reshape(batch * seq, MLA_OUT)


def _outproj_kernel(x_ref, o0_ref, l0_ref, o1_ref, l1_ref, o2_ref, l2_ref, ob_ref, gate_ref,
                    wpa_ref, wpb_ref, wo_ref, g2_ref, wrh_ref, wrl_ref, br_ref,
                    xm_ref, h2_ref, route_ref):
    l0, l1, l2 = l0_ref[...], l1_ref[...], l2_ref[...]
    m = jnp.maximum(jnp.maximum(l0, l1), l2)
    w0, w1, w2 = jnp.exp(l0 - m), jnp.exp(l1 - m), jnp.exp(l2 - m)
    oa = (w0 * o0_ref[...] + w1 * o1_ref[...] + w2 * o2_ref[...]) / (w0 + w1 + w2)
    pa = jnp.dot(oa.astype(BF16), wpa_ref[...], preferred_element_type=F32)
    pb = jnp.dot(ob_ref[...], wpb_ref[...], preferred_element_type=F32)
    merged = gate_ref[:, :D_MODEL] * pa + gate_ref[:, D_MODEL:] * pb
    xm = x_ref[...] + jnp.dot(merged.astype(BF16), wo_ref[...], preferred_element_type=F32)
    xm_ref[...] = xm
    h2 = _rms(xm, g2_ref[...])
    h2_ref[...] = h2

    hi = h2.astype(BF16)
    lo = (h2 - hi.astype(F32)).astype(BF16)
    lg = (jnp.dot(hi, wrh_ref[...], preferred_element_type=F32)
          + jnp.dot(lo, wrh_ref[...], preferred_element_type=F32)
          + jnp.dot(hi, wrl_ref[...], preferred_element_type=F32)) + br_ref[...]
    lane = lax.broadcasted_iota(jnp.int32, lg.shape, 1)
    lanef = lane.astype(F32)
    big = float(ROUTE_LANES)
    gmask = (lane >= N_EXPERTS) & (lane < N_EXPERTS + N_GROUPS)
    gl = jnp.where(gmask, lg, NEG)
    gmax = jnp.max(gl, axis=-1, keepdims=True)
    gidx = jnp.min(jnp.where(gl == gmax, lanef, big), axis=-1, keepdims=True) - float(N_EXPERTS)
    p_grp = 1.0 / jnp.sum(jnp.where(gmask, jnp.exp(gl - gmax), 0.0), axis=-1, keepdims=True)
    emask = (lane // EXPERTS_PER_GROUP).astype(F32) == gidx
    el = jnp.where(emask, lg, NEG)
    m1 = jnp.max(el, axis=-1, keepdims=True)
    i1 = jnp.min(jnp.where(el == m1, lanef, big), axis=-1, keepdims=True)
    el2 = jnp.where(lanef == i1, NEG, el)
    m2 = jnp.max(el2, axis=-1, keepdims=True)
    i2 = jnp.min(jnp.where(el2 == m2, lanef, big), axis=-1, keepdims=True)
    r = jnp.exp(m2 - m1)
    wa = p_grp / (1.0 + r)
    wb = p_grp * r / (1.0 + r)
    route_ref[...] = jnp.where(lane == 0, i1, jnp.where(lane == 1, i2,
                               jnp.where(lane == 2, wa, jnp.where(lane == 3, wb, 0.0))))


def _outproj(x, dil_outs, ob, gates, lw, tm):
    T = x.shape[0]
    row = lambda i: (i, 0)
    const = lambda i: (0, 0)
    weights = [lw['wpa'], lw['wpb'], lw['wo'], lw['g2'], lw['wrh'], lw['wrl'], lw['br']]
    acts = [x] + list(dil_outs) + [ob, gates]
    in_specs = [pl.BlockSpec((tm, a.shape[1]), row) for a in acts] \
        + [pl.BlockSpec(a.shape, const) for a in weights]
    return pl.pallas_call(
        _outproj_kernel, grid=(T // tm,), in_specs=in_specs,
        out_specs=[pl.BlockSpec((tm, D_MODEL), row), pl.BlockSpec((tm, D_MODEL), row),
                   pl.BlockSpec((tm, ROUTE_LANES), row)],
        out_shape=[jax.ShapeDtypeStruct((T, D_MODEL), F32), jax.ShapeDtypeStruct((T, D_MODEL), F32),
                   jax.ShapeDtypeStruct((T, ROUTE_LANES), F32)],
        compiler_params=_cparams(("parallel",)), name="outproj_router",
    )(*acts, *weights)


GATHER_WINDOW = 32


def _gather_rows(src, idx):
    m = idx.shape[0]
    width = src.shape[1]
    info = plsc.get_sparse_core_info()
    n_workers = info.num_cores * info.num_subcores
    per_w = m // n_workers
    n_pairs = per_w // (2 * GATHER_WINDOW)
    assert n_pairs * 2 * GATHER_WINDOW * n_workers == m
    mesh = plsc.VectorSubcoreMesh(core_axis_name="core", subcore_axis_name="subcore")

    @functools.partial(
        pl.kernel, out_type=jax.ShapeDtypeStruct((m, width), src.dtype), mesh=mesh, name="gather_rows",
        scratch_types=[pltpu.VMEM((per_w,), jnp.int32),
                       pltpu.VMEM((2, GATHER_WINDOW, width), src.dtype),
                       pltpu.SemaphoreType.DMA((2,))])
    def gather(src_hbm, idx_hbm, out_hbm, idx_v, rows_v, sems):
        wid = lax.axis_index("subcore") * info.num_cores + lax.axis_index("core")
        base = wid * per_w
        pltpu.sync_copy(idx_hbm.at[pl.ds(base, per_w)], idx_v)

        def fetch(c, b):
            return pltpu.make_async_copy(src_hbm.at[idx_v.at[pl.ds(c * GATHER_WINDOW, GATHER_WINDOW)]],
                                         rows_v.at[b], sems.at[b])

        def flush(c, b):
            pltpu.sync_copy(rows_v.at[b], out_hbm.at[pl.ds(base + c * GATHER_WINDOW, GATHER_WINDOW)])

        fetch(0, 0).start()

        @pl.loop(0, n_pairs)
        def _(p):
            c = 2 * p
            fetch(c + 1, 1).start()
            fetch(c, 0).wait()
            flush(c, 0)

            @pl.when(p + 1 < n_pairs)
            def _():
                fetch(c + 2, 0).start()

            fetch(c + 1, 1).wait()
            flush(c + 1, 1)

    return gather(src, idx)


EXPERT_BLOCK = 256


def _expert_kernel(be_ref, bv_ref, xs_ref, wg_ref, wu_ref, wd_ref, y_ref):
    i = pl.program_id(0)

    @pl.when(bv_ref[i] > 0)
    def _():
        xb = xs_ref[...].astype(BF16)
        g = jnp.dot(xb, wg_ref[0], preferred_element_type=F32)
        u = jnp.dot(xb, wu_ref[0], preferred_element_type=F32)
        hb = (g * jax.nn.sigmoid(g) * u).astype(BF16)
        y_ref[...] = jnp.dot(hb, wd_ref[0], preferred_element_type=F32)

    @pl.when(bv_ref[i] == 0)
    def _():
        y_ref[...] = jnp.zeros(y_ref.shape, F32)


def _experts(xs, blk_expert, blk_valid, wg, wu, wd):
    n_slots = xs.shape[0]
    nb = n_slots // EXPERT_BLOCK
    grid_spec = pltpu.PrefetchScalarGridSpec(
        num_scalar_prefetch=2, grid=(nb,),
        in_specs=[pl.BlockSpec((EXPERT_BLOCK, D_MODEL), lambda i, be, bv: (i, 0)),
                  pl.BlockSpec((1, D_MODEL, D_EXPERT), lambda i, be, bv: (be[i], 0, 0)),
                  pl.BlockSpec((1, D_MODEL, D_EXPERT), lambda i, be, bv: (be[i], 0, 0)),
                  pl.BlockSpec((1, D_EXPERT, D_MODEL), lambda i, be, bv: (be[i], 0, 0))],
        out_specs=pl.BlockSpec((EXPERT_BLOCK, D_MODEL), lambda i, be, bv: (i, 0)))
    return pl.pallas_call(
        _expert_kernel, grid_spec=grid_spec,
        out_shape=jax.ShapeDtypeStruct((n_slots, D_MODEL), F32),
        compiler_params=_cparams(("arbitrary",)), name="experts",
    )(blk_expert, blk_valid, xs, wg, wu, wd)


def _combine_kernel(x_ref, ya_ref, yb_ref, route_ref, g_ref, o_ref, *, final):
    route = route_ref[...]
    wa = route[:, 2:3]
    wb = route[:, 3:4]
    x = x_ref[...] + (ya_ref[...] * wa + yb_ref[...] * wb)
    if final:
        x = _rms(x, g_ref[...])
    o_ref[...] = x


def _combine(x, yg, route, g, final, tm):
    T = x.shape[0]
    nt = T // tm
    return pl.pallas_call(
        functools.partial(_combine_kernel, final=final), grid=(nt,),
        in_specs=[pl.BlockSpec((tm, D_MODEL), lambda i: (i, 0)),
                  pl.BlockSpec((tm, D_MODEL), lambda i: (i, 0)),
                  pl.BlockSpec((tm, D_MODEL), lambda i: (i + nt, 0)),
                  pl.BlockSpec((tm, ROUTE_LANES), lambda i: (i, 0)),
                  pl.BlockSpec((1, D_MODEL), lambda i: (0, 0))],
        out_specs=pl.BlockSpec((tm, D_MODEL), lambda i: (i, 0)),
        out_shape=jax.ShapeDtypeStruct((T, D_MODEL), F32),
        compiler_params=_cparams(("parallel",)), name="combine",
    )(x, yg, yg, route, g)


def _dispatch_plan(route, T):
    A = T * TOP_K
    flat_e = route[:, 0:TOP_K].astype(jnp.int32).reshape(A)
    order = jnp.argsort(flat_e)
    e_sorted = flat_e[order]
    counts = jnp.zeros((N_EXPERTS,), jnp.int32).at[flat_e].add(1)
    padded = (counts + EXPERT_BLOCK - 1) // EXPERT_BLOCK * EXPERT_BLOCK
    pad_end = jnp.cumsum(padded)
    pad_start = pad_end - padded
    start = jnp.cumsum(counts) - counts
    slot_sorted = pad_start[e_sorted] + jnp.arange(A, dtype=jnp.int32) - start[e_sorted]
    n_blocks = A // EXPERT_BLOCK + N_EXPERTS
    n_slots = n_blocks * EXPERT_BLOCK
    slot_tok = jnp.zeros((n_slots,), jnp.int32).at[slot_sorted].set((order // TOP_K).astype(jnp.int32))
    slot_of = jnp.zeros((A,), jnp.int32).at[order].set(slot_sorted)
    blk_start = jnp.arange(n_blocks, dtype=jnp.int32) * EXPERT_BLOCK
    blk_expert = jnp.minimum(jnp.searchsorted(pad_end, blk_start, side='right'),
                             N_EXPERTS - 1).astype(jnp.int32)
    blk_valid = (blk_start < pad_end[-1]).astype(jnp.int32)
    pair = slot_of.reshape(T, TOP_K)
    comb_idx = jnp.concatenate([pair[:, 0], pair[:, 1]])
    return slot_tok, blk_expert, blk_valid, comb_idx


def _rope_tables(seq, dim, period, first):
    half = dim // 2
    inv = ROPE_THETA ** (-jnp.arange(0, dim, 2, dtype=F32) / dim)
    ang = jnp.arange(seq, dtype=F32)[:, None] * inv[None, :]
    cos, sin = jnp.cos(ang), jnp.sin(ang)
    d = jnp.arange(LANES) % period - first
    in_a = (d >= 0) & (d < half)
    in_b = (d >= half) & (d < dim)
    idx = jnp.clip(jnp.where(in_b, d - half, d), 0, half - 1)
    c = jnp.where((in_a | in_b)[None, :], cos[:, idx], 1.0)
    sa = jnp.where(in_a[None, :], -sin[:, idx], 0.0)
    sb = jnp.where(in_b[None, :], sin[:, idx], 0.0)
    return c, sa, sb


def _pad_heads(w, width):
    k = w.shape[0]
    w = w.reshape(k, MLA_HEADS, width)
    return jnp.pad(w, ((0, 0), (0, 0), (0, MLA_SLOT - width))).reshape(k, MLA_PAD)


def _prep_layer(l, norm1_g, w_in, q_norm_g, w_uq, kv_norm_g, w_uk, w_uv, w_pa, w_pb, w_o,
                norm2_g, w_rg, b_rg, w_re, b_re, w_e_gate, w_e_up, w_e_down):
    w = w_in[l]
    o = 0
    parts = []
    for n in (DIL_WIDTH, DIL_WIDTH, DIL_WIDTH, Q_LORA, KV_LORA, MLA_ROPE, 2 * D_MODEL):
        parts.append(w[:, o:o + n])
        o += n
    wq, wk, wv, wcq, wckv, wkr, wg = parts
    wkr_pad = jnp.pad(wkr, ((0, 0), (MLA_NOPE, MLA_SLOT - MLA_NOPE - MLA_ROPE)))
    wr = jnp.pad(jnp.concatenate([w_re[l], w_rg[l]], axis=1),
                 ((0, 0), (0, ROUTE_LANES - N_EXPERTS - N_GROUPS)))
    wrh = wr.astype(BF16)
    wrl = (wr - wrh.astype(F32)).astype(BF16)
    br = jnp.pad(jnp.concatenate([b_re[l], b_rg[l]]), (0, ROUTE_LANES - N_EXPERTS - N_GROUPS))
    return dict(
        g1=norm1_g[l][None, :], wq=wq.astype(BF16), wk=wk.astype(BF16), wv=wv.astype(BF16),
        wcq=wcq.astype(BF16), wckv=wckv.astype(BF16), wkr=wkr_pad.astype(BF16), wg=wg.astype(BF16),
        qn=q_norm_g[l][None, :], kvn=kv_norm_g[l][None, :],
        wuq=_pad_heads(w_uq[l], MLA_QK).astype(BF16), wuk=_pad_heads(w_uk[l], MLA_NOPE).astype(BF16),
        wuv=_pad_heads(w_uv[l], MLA_V).astype(BF16),
        wpa=w_pa[l].astype(BF16), wpb=w_pb[l].astype(BF16), wo=w_o[l].astype(BF16),
        g2=norm2_g[l][None, :], wrh=wrh, wrl=wrl, br=br[None, :].astype(F32),
        weg=w_e_gate[l].astype(BF16), weu=w_e_up[l].astype(BF16), wed=w_e_down[l].astype(BF16))


def _trunk(x3, layers, final_g, tm=256):
    batch, seq, _ = x3.shape
    T = batch * seq
    x = x3.reshape(T, D_MODEL)
    tabs = _rope_tables(seq, ROT_DIM, HEAD_DIM, 0) + _rope_tables(seq, MLA_ROPE, MLA_SLOT, MLA_NOPE)
    fg = final_g[None, :]
    for l, lw in enumerate(layers):
        qa, ka, va, qm, km, vm, gates = _inproj(x, lw, tabs, seq, tm)
        dil = []
        for g, (_, d) in enumerate(DIL_GROUPS):
            dil.extend(_dilated_group(qa, ka, va, batch, seq, g, d))
        ob = _mla(qm, km, vm, batch, seq)
        xm, h2, route = _outproj(x, dil, ob, gates, lw, tm)
        slot_tok, blk_expert, blk_valid, comb_idx = _dispatch_plan(route, T)
        xs = _gather_rows(h2, slot_tok)
        ys = _experts(xs, blk_expert, blk_valid, lw['weg'], lw['weu'], lw['wed'])
        yg = _gather_rows(ys, comb_idx)
        x = _combine(xm, yg, route, fg, l == len(layers) - 1, tm)
    return x.reshape(batch, seq, D_MODEL)


def kernel(x_prompt, x_sample, norm1_g, w_in, q_norm_g, w_uq, kv_norm_g, w_uk, w_uv, w_pa, w_pb, w_o,
           norm2_g, w_rg, b_rg, w_re, b_re, w_e_gate, w_e_up, w_e_down, final_g):
    layers = [_prep_layer(l, norm1_g, w_in, q_norm_g, w_uq, kv_norm_g, w_uk, w_uv, w_pa, w_pb, w_o,
                          norm2_g, w_rg, b_rg, w_re, b_re, w_e_gate, w_e_up, w_e_down)
              for l in range(DEPTH)]
    return (_trunk(x_prompt, layers, final_g), _trunk(x_sample, layers, final_g))
```

```python
import functools

import jax
import jax.numpy as jnp
from jax import lax
from jax.experimental import pallas as pl
from jax.experimental.pallas import tpu as pltpu
from jax.experimental.pallas import tpu_sc as plsc

D_MODEL = 1024
DEPTH = 2
HEAD_DIM = 64
DIL_GROUPS = ((128, 1), (512, 4), (2048, 16))
DIL_HEADS_PER_GROUP = 4
DIL_WIDTH = 768
DIL_OUT = 256
DIL_SCALE = HEAD_DIM ** -0.5
ROT_DIM = 16
ROPE_THETA = 500000.0
DIL_HALF = 64

MLA_HEADS = 8
MLA_NOPE = 64
MLA_ROPE = 32
MLA_QK = 96
MLA_V = 64
MLA_OUT = 512
MLA_SCALE = MLA_QK ** -0.5
Q_LORA = 256
KV_LORA = 128

N_GROUPS = 8
EXPERTS_PER_GROUP = 8
N_EXPERTS = 64
TOP_K = 2
D_EXPERT = 512
EPS = 1e-6

LANES = 128
MLA_SLOT = LANES
MLA_PAD = MLA_HEADS * MLA_SLOT
ROUTE_LANES = LANES
VMEM_LIMIT = 56 * 1024 * 1024

BF16 = jnp.bfloat16
F32 = jnp.float32
NEG = -1e30
LOG2E = 1.4426950408889634


def _cparams(sem):
    return pltpu.CompilerParams(dimension_semantics=sem, vmem_limit_bytes=VMEM_LIMIT)


def _rms(t, g):
    return t * lax.rsqrt(jnp.mean(t * t, axis=-1, keepdims=True) + EPS) * g


def _rope_chunk(t, c, sa, sb, shift):
    return t * c + pltpu.roll(t, LANES - shift, 1) * sa + pltpu.roll(t, shift, 1) * sb


def _inproj_kernel(x_ref, g1_ref, wq_ref, wk_ref, wv_ref, wcq_ref, wckv_ref, wkr_ref, wg_ref,
                   qn_ref, kvn_ref, wuq_ref, wuk_ref, wuv_ref,
                   ca_ref, saa_ref, sab_ref, cb_ref, sba_ref, sbb_ref,
                   qa_ref, ka_ref, va_ref, qm_ref, km_ref, vm_ref, gate_ref):
    x = x_ref[...]
    h = _rms(x, g1_ref[...]).astype(BF16)

    ca, saa, sab = ca_ref[...], saa_ref[...], sab_ref[...]
    q = jnp.dot(h, wq_ref[...], preferred_element_type=F32)
    k = jnp.dot(h, wk_ref[...], preferred_element_type=F32)
    for j in range(DIL_WIDTH // LANES):
        sl = slice(j * LANES, (j + 1) * LANES)
        qa_ref[:, sl] = (_rope_chunk(q[:, sl], ca, saa, sab, ROT_DIM // 2) * DIL_SCALE).astype(BF16)
        ka_ref[:, sl] = _rope_chunk(k[:, sl], ca, saa, sab, ROT_DIM // 2).astype(BF16)
    va_ref[...] = jnp.dot(h, wv_ref[...], preferred_element_type=F32).astype(BF16)

    cb, sba, sbb = cb_ref[...], sba_ref[...], sbb_ref[...]
    cq = jnp.dot(h, wcq_ref[...], preferred_element_type=F32)
    cqn = _rms(cq, qn_ref[...]).astype(BF16)
    qm = jnp.dot(cqn, wuq_ref[...], preferred_element_type=F32) * (MLA_SCALE * LOG2E)
    ckv = jnp.dot(h, wckv_ref[...], preferred_element_type=F32)
    c = _rms(ckv, kvn_ref[...]).astype(BF16)
    kn = jnp.dot(c, wuk_ref[...], preferred_element_type=F32)
    vv = jnp.dot(c, wuv_ref[...], preferred_element_type=F32)
    kr = jnp.dot(h, wkr_ref[...], preferred_element_type=F32)
    kr = _rope_chunk(kr, cb, sba, sbb, MLA_ROPE // 2)
    lane = lax.broadcasted_iota(jnp.int32, (1, LANES), 1)
    ones_col = jnp.where(lane == MLA_V, 1.0, 0.0).astype(F32)
    for j in range(MLA_HEADS):
        sl = slice(j * LANES, (j + 1) * LANES)
        qm_ref[:, sl] = _rope_chunk(qm[:, sl], cb, sba, sbb, MLA_ROPE // 2).astype(BF16)
        km_ref[:, sl] = (kn[:, sl] + kr).astype(BF16)
        vm_ref[:, sl] = (vv[:, sl] + ones_col).astype(BF16)

    gate_ref[...] = jax.nn.sigmoid(jnp.dot(h, wg_ref[...], preferred_element_type=F32))


def _inproj(x, lw, tabs, seq, tm):
    T = x.shape[0]
    nt = seq // tm
    row = lambda i: (i, 0)
    const = lambda i: (0, 0)
    tab = lambda i: (i % nt, 0)

    def wspec(a):
        return pl.BlockSpec(a.shape, const)

    weights = [lw['g1'], lw['wq'], lw['wk'], lw['wv'], lw['wcq'], lw['wckv'], lw['wkr'], lw['wg'],
               lw['qn'], lw['kvn'], lw['wuq'], lw['wuk'], lw['wuv']]
    in_specs = ([pl.BlockSpec((tm, D_MODEL), row)] + [wspec(a) for a in weights]
                + [pl.BlockSpec((tm, LANES), tab)] * 6)
    out_shape = [jax.ShapeDtypeStruct((T, DIL_WIDTH), BF16)] * 3 \
        + [jax.ShapeDtypeStruct((T, MLA_PAD), BF16)] * 3 \
        + [jax.ShapeDtypeStruct((T, 2 * D_MODEL), F32)]
    out_specs = [pl.BlockSpec((tm, DIL_WIDTH), row)] * 3 + [pl.BlockSpec((tm, MLA_PAD), row)] * 3 \
        + [pl.BlockSpec((tm, 2 * D_MODEL), row)]
    return pl.pallas_call(
        _inproj_kernel, grid=(T // tm,), in_specs=in_specs, out_specs=out_specs, out_shape=out_shape,
        compiler_params=_cparams(("parallel",)), name="inproj",
    )(x, *weights, *tabs)


def _dil_kernel(q_ref, kp_ref, kc_ref, kn_ref, vp_ref, vc_ref, vn_ref, o_ref, lse_ref,
                kbuf, vbuf, *, tq, sub_len):
    i = pl.program_id(2)
    qb = LANES
    kbuf[0:qb] = kp_ref[0]
    kbuf[qb:qb + tq] = kc_ref[0]
    kbuf[qb + tq:] = kn_ref[0]
    vbuf[0:qb] = vp_ref[0]
    vbuf[qb:qb + tq] = vc_ref[0]
    vbuf[qb + tq:] = vn_ref[0]
    head = lax.broadcasted_iota(jnp.int32, (1, DIL_OUT), 1) // HEAD_DIM
    rel = (lax.broadcasted_iota(jnp.int32, (qb, 2 * qb), 1) - DIL_HALF
           - lax.broadcasted_iota(jnp.int32, (qb, 2 * qb), 0))
    band = jnp.abs(rel) <= DIL_HALF
    kcol = lax.broadcasted_iota(jnp.int32, (1, 2 * qb), 1) - DIL_HALF
    for j in range(tq // qb):
        q = q_ref[0, j * qb:(j + 1) * qb, :]
        k = kbuf[DIL_HALF + j * qb:DIL_HALF + j * qb + 2 * qb, :]
        v = vbuf[DIL_HALF + j * qb:DIL_HALF + j * qb + 2 * qb, :]
        kpos = kcol + (i * tq + j * qb)
        valid = band & (kpos >= 0) & (kpos < sub_len)
        o_acc = jnp.zeros((qb, DIL_OUT), F32)
        l_acc = jnp.zeros((qb, DIL_OUT), F32)
        for hd in range(DIL_HEADS_PER_GROUP):
            hm = head == hd
            qh = jnp.where(hm, q, jnp.zeros_like(q))
            s = lax.dot_general(qh, k, (((1,), (1,)), ((), ())), preferred_element_type=F32)
            s = jnp.where(valid, s, NEG)
            m = jnp.max(s, axis=-1, keepdims=True)
            p = jnp.exp(s - m)
            den = jnp.sum(p, axis=-1, keepdims=True)
            oh = jnp.dot(p.astype(BF16), v, preferred_element_type=F32) * (1.0 / den)
            o_acc = jnp.where(hm, oh, o_acc)
            l_acc = jnp.where(hm, m + jnp.log(den), l_acc)
        o_ref[0, j * qb:(j + 1) * qb, :] = o_acc
        lse_ref[0, j * qb:(j + 1) * qb, :] = l_acc


def _dilated_group(qa, ka, va, batch, seq, g, dil):
    L = seq // dil
    tq = min(L, 512)
    nq = L // tq
    hb = tq // LANES
    nhb = L // LANES
    qv = qa.reshape(batch, L, dil * DIL_WIDTH)
    kv = ka.reshape(batch, L, dil * DIL_WIDTH)
    vv = va.reshape(batch, L, dil * DIL_WIDTH)
    ncol = DIL_WIDTH // DIL_OUT
    cur = lambda b, r, i: (b, i, r * ncol + g)
    prev = lambda b, r, i: (b, jnp.maximum(i * hb - 1, 0), r * ncol + g)
    nxt = lambda b, r, i: (b, jnp.minimum((i + 1) * hb, nhb - 1), r * ncol + g)
    blk = (1, tq, DIL_OUT)
    halo = (1, LANES, DIL_OUT)
    out_map = lambda b, r, i: (b, i, r)
    o, lse = pl.pallas_call(
        functools.partial(_dil_kernel, tq=tq, sub_len=L),
        grid=(batch, dil, nq),
        in_specs=[pl.BlockSpec(blk, cur),
                  pl.BlockSpec(halo, prev), pl.BlockSpec(blk, cur), pl.BlockSpec(halo, nxt),
                  pl.BlockSpec(halo, prev), pl.BlockSpec(blk, cur), pl.BlockSpec(halo, nxt)],
        out_specs=[pl.BlockSpec(blk, out_map), pl.BlockSpec(blk, out_map)],
        out_shape=[jax.ShapeDtypeStruct((batch, L, dil * DIL_OUT), F32)] * 2,
        scratch_shapes=[pltpu.VMEM((tq + 2 * LANES, DIL_OUT), BF16)] * 2,
        compiler_params=_cparams(("parallel", "parallel", "parallel")), name=f"dilated_g{g}",
    )(qv, kv, kv, kv, vv, vv, vv)
    return o.reshape(batch * seq, DIL_OUT), lse.reshape(batch * seq, DIL_OUT)


def _mla_kernel(q_ref, k_ref, v_ref, o_ref, m_sc, acc_sc, *, tk, nk, unroll):
    nc = tk // LANES
    m_sc[...] = jnp.full(m_sc.shape, NEG, F32)
    acc_sc[...] = jnp.zeros(acc_sc.shape, F32)

    def step(off, hh):
        sl = slice(hh * LANES, (hh + 1) * LANES)
        q = q_ref[0, :, sl]
        k = k_ref[0, pl.ds(off, tk), sl]
        v = v_ref[0, pl.ds(off, tk), sl]
        s = lax.dot_general(q, k, (((1,), (1,)), ((), ())), preferred_element_type=F32)
        cols = [s[:, c * LANES:(c + 1) * LANES] for c in range(nc)]
        m_old = m_sc[hh]
        m_new = jnp.maximum(m_old, jnp.max(functools.reduce(jnp.maximum, cols), axis=-1, keepdims=True))
        p = jnp.concatenate([jnp.exp2(c - m_new) for c in cols], axis=1).astype(BF16)
        acc_sc[hh] = acc_sc[hh] * jnp.exp2(m_old - m_new) + jnp.dot(p, v, preferred_element_type=F32)
        m_sc[hh] = m_new

    def body(j, carry):
        for u in range(unroll):
            off = pl.multiple_of((j * unroll + u) * tk, tk)
            for hh in range(2):
                step(off, hh)
        return carry

    lax.fori_loop(0, nk // unroll, body, 0)
    outs = []
    for hh in range(2):
        acc = acc_sc[hh]
        outs.append(acc * (1.0 / acc[:, MLA_V:MLA_V + 1]))
    lane = lax.broadcasted_iota(jnp.int32, (1, LANES), 1)
    o_ref[0] = jnp.where(lane < MLA_V, outs[0], pltpu.roll(outs[1], MLA_V, 1)).astype(o_ref.dtype)


def _mla(qm, km, vm, batch, seq, tq=1024, tk=2048, unroll=1):
    tq = min(seq, tq)
    tk = min(seq, tk)
    nk = seq // tk
    unroll = unroll if nk % unroll == 0 else 1
    q3 = qm.reshape(batch, seq, MLA_PAD)
    k3 = km.reshape(batch, seq, MLA_PAD)
    v3 = vm.reshape(batch, seq, MLA_PAD)
    resident = lambda b, h, i: (b, 0, h)
    o = pl.pallas_call(
        functools.partial(_mla_kernel, tk=tk, nk=nk, unroll=unroll),
        grid=(batch, MLA_HEADS // 2, seq // tq),
        in_specs=[pl.BlockSpec((1, tq, 2 * LANES), lambda b, h, i: (b, i, h)),
                  pl.BlockSpec((1, seq, 2 * LANES), resident, pipeline_mode=pl.Buffered(1)),
                  pl.BlockSpec((1, seq, 2 * LANES), resident, pipeline_mode=pl.Buffered(1))],
        out_specs=pl.BlockSpec((1, tq, LANES), lambda b, h, i: (b, i, h)),
        out_shape=jax.ShapeDtypeStruct((batch, seq, MLA_OUT), BF16),
        scratch_shapes=[pltpu.VMEM((2, tq, LANES), F32), pltpu.VMEM((2, tq, LANES), F32)],
        compiler_params=_cparams(("parallel", "parallel", "arbitrary")), name="mla",
    )(q3, k3, v3)
    return o.reshape(batch * seq, MLA_OUT)


def _outproj_kernel(x_ref, o0_ref, l0_ref, o1_ref, l1_ref, o2_ref, l2_ref, ob_ref, gate_ref,
                    wpa_ref, wpb_ref, wo_ref, g2_ref, wrh_ref, wrl_ref, br_ref,
                    xm_ref, h2_ref, route_ref, cnt_ref, cnt_sc):
    l0, l1, l2 = l0_ref[...], l1_ref[...], l2_ref[...]
    m = jnp.maximum(jnp.maximum(l0, l1), l2)
    w0, w1, w2 = jnp.exp(l0 - m), jnp.exp(l1 - m), jnp.exp(l2 - m)
    oa = (w0 * o0_ref[...] + w1 * o1_ref[...] + w2 * o2_ref[...]) / (w0 + w1 + w2)
    pa = jnp.dot(oa.astype(BF16), wpa_ref[...], preferred_element_type=F32)
    pb = jnp.dot(ob_ref[...], wpb_ref[...], preferred_element_type=F32)
    merged = gate_ref[:, :D_MODEL] * pa + gate_ref[:, D_MODEL:] * pb
    xm = x_ref[...] + jnp.dot(merged.astype(BF16), wo_ref[...], preferred_element_type=F32)
    xm_ref[...] = xm
    h2 = _rms(xm, g2_ref[...])
    h2_ref[...] = h2

    hi = h2.astype(BF16)
    lo = (h2 - hi.astype(F32)).astype(BF16)
    lg = (jnp.dot(hi, wrh_ref[...], preferred_element_type=F32)
          + jnp.dot(lo, wrh_ref[...], preferred_element_type=F32)
          + jnp.dot(hi, wrl_ref[...], preferred_element_type=F32)) + br_ref[...]
    lane = lax.broadcasted_iota(jnp.int32, lg.shape, 1)
    lanef = lane.astype(F32)
    big = float(ROUTE_LANES)
    gmask = (lane >= N_EXPERTS) & (lane < N_EXPERTS + N_GROUPS)
    gl = jnp.where(gmask, lg, NEG)
    gmax = jnp.max(gl, axis=-1, keepdims=True)
    gidx = jnp.min(jnp.where(gl == gmax, lanef, big), axis=-1, keepdims=True) - float(N_EXPERTS)
    p_grp = 1.0 / jnp.sum(jnp.where(gmask, jnp.exp(gl - gmax), 0.0), axis=-1, keepdims=True)
    emask = (lane // EXPERTS_PER_GROUP).astype(F32) == gidx
    el = jnp.where(emask, lg, NEG)
    m1 = jnp.max(el, axis=-1, keepdims=True)
    i1 = jnp.min(jnp.where(el == m1, lanef, big), axis=-1, keepdims=True)
    el2 = jnp.where(lanef == i1, NEG, el)
    m2 = jnp.max(el2, axis=-1, keepdims=True)
    i2 = jnp.min(jnp.where(el2 == m2, lanef, big), axis=-1, keepdims=True)
    r = jnp.exp(m2 - m1)
    wa = p_grp / (1.0 + r)
    wb = p_grp * r / (1.0 + r)
    @pl.when(pl.program_id(0) == 0)
    def _():
        cnt_sc[...] = jnp.zeros(cnt_sc.shape, F32)

    oh1 = lanef == i1
    oh2 = lanef == i2
    oh = jnp.where(oh1 | oh2, 1.0, 0.0)
    tm = lg.shape[0]
    earlier = (lax.broadcasted_iota(jnp.int32, (tm, tm), 0) > lax.broadcasted_iota(jnp.int32, (tm, tm), 1))
    before = jnp.dot(jnp.where(earlier, 1.0, 0.0).astype(BF16), oh.astype(BF16),
                     preferred_element_type=F32) + cnt_sc[0:1, :]
    r1 = jnp.sum(jnp.where(oh1, before, 0.0), axis=-1, keepdims=True)
    r2 = jnp.sum(jnp.where(oh2, before, 0.0), axis=-1, keepdims=True)
    cnt = cnt_sc[...] + jnp.sum(oh, axis=0, keepdims=True)
    cnt_sc[...] = cnt
    cnt_ref[...] = cnt
    route_ref[...] = jnp.where(lane == 0, i1, jnp.where(lane == 1, i2, jnp.where(lane == 2, wa,
                               jnp.where(lane == 3, wb, jnp.where(lane == 4, r1,
                                                                  jnp.where(lane == 5, r2, 0.0))))))


def _outproj(x, dil_outs, ob, gates, lw, tm):
    T = x.shape[0]
    row = lambda i: (i, 0)
    const = lambda i: (0, 0)
    weights = [lw['wpa'], lw['wpb'], lw['wo'], lw['g2'], lw['wrh'], lw['wrl'], lw['br']]
    acts = [x] + list(dil_outs) + [ob, gates]
    in_specs = [pl.BlockSpec((tm, a.shape[1]), row) for a in acts] \
        + [pl.BlockSpec(a.shape, const) for a in weights]
    return pl.pallas_call(
        _outproj_kernel, grid=(T // tm,), in_specs=in_specs,
        out_specs=[pl.BlockSpec((tm, D_MODEL), row), pl.BlockSpec((tm, D_MODEL), row),
                   pl.BlockSpec((tm, ROUTE_LANES), row), pl.BlockSpec((8, ROUTE_LANES), const)],
        out_shape=[jax.ShapeDtypeStruct((T, D_MODEL), F32), jax.ShapeDtypeStruct((T, D_MODEL), F32),
                   jax.ShapeDtypeStruct((T, ROUTE_LANES), F32), jax.ShapeDtypeStruct((8, ROUTE_LANES), F32)],
        scratch_shapes=[pltpu.VMEM((8, ROUTE_LANES), F32)],
        compiler_params=_cparams(("arbitrary",)), name="outproj_router",
    )(*acts, *weights)


GATHER_WINDOW = 32


def _gather_rows(src, idx):
    m = idx.shape[0]
    width = src.shape[1]
    info = plsc.get_sparse_core_info()
    n_workers = info.num_cores * info.num_subcores
    per_w = m // n_workers
    n_pairs = per_w // (2 * GATHER_WINDOW)
    assert n_pairs * 2 * GATHER_WINDOW * n_workers == m
    mesh = plsc.VectorSubcoreMesh(core_axis_name="core", subcore_axis_name="subcore")

    @functools.partial(
        pl.kernel, out_type=jax.ShapeDtypeStruct((m, width), src.dtype), mesh=mesh, name="gather_rows",
        scratch_types=[pltpu.VMEM((per_w,), jnp.int32),
                       pltpu.VMEM((2, GATHER_WINDOW, width), src.dtype),
                       pltpu.SemaphoreType.DMA((2,))])
    def gather(src_hbm, idx_hbm, out_hbm, idx_v, rows_v, sems):
        wid = lax.axis_index("subcore") * info.num_cores + lax.axis_index("core")
        base = wid * per_w
        pltpu.sync_copy(idx_hbm.at[pl.ds(base, per_w)], idx_v)

        def fetch(c, b):
            return pltpu.make_async_copy(src_hbm.at[idx_v.at[pl.ds(c * GATHER_WINDOW, GATHER_WINDOW)]],
                                         rows_v.at[b], sems.at[b])

        def flush(c, b):
            pltpu.sync_copy(rows_v.at[b], out_hbm.at[pl.ds(base + c * GATHER_WINDOW, GATHER_WINDOW)])

        fetch(0, 0).start()

        @pl.loop(0, n_pairs)
        def _(p):
            c = 2 * p
            fetch(c + 1, 1).start()
            fetch(c, 0).wait()
            flush(c, 0)

            @pl.when(p + 1 < n_pairs)
            def _():
                fetch(c + 2, 0).start()

            fetch(c + 1, 1).wait()
            flush(c + 1, 1)

    return gather(src, idx)


EXPERT_BLOCK = 256


def _expert_kernel(be_ref, bv_ref, bf_ref, xs_ref, wg_ref, wu_ref, wd_ref, y_ref, wg_sc, wu_sc, wd_sc):
    i = pl.program_id(0)

    @pl.when(bf_ref[i] > 0)
    def _():
        wg_sc[...] = wg_ref[0].astype(BF16)
        wu_sc[...] = wu_ref[0].astype(BF16)
        wd_sc[...] = wd_ref[0].astype(BF16)

    @pl.when(bv_ref[i] > 0)
    def _():
        xb = xs_ref[...].astype(BF16)
        g = jnp.dot(xb, wg_sc[...], preferred_element_type=F32)
        u = jnp.dot(xb, wu_sc[...], preferred_element_type=F32)
        hb = (g * jax.nn.sigmoid(g) * u).astype(BF16)
        y_ref[...] = jnp.dot(hb, wd_sc[...], preferred_element_type=F32)

    @pl.when(bv_ref[i] == 0)
    def _():
        y_ref[...] = jnp.zeros(y_ref.shape, F32)


def _experts(xs, blk_expert, blk_valid, blk_first, wg, wu, wd):
    n_slots = xs.shape[0]
    nb = n_slots // EXPERT_BLOCK
    grid_spec = pltpu.PrefetchScalarGridSpec(
        num_scalar_prefetch=3, grid=(nb,),
        in_specs=[pl.BlockSpec((EXPERT_BLOCK, D_MODEL), lambda i, be, bv, bf: (i, 0)),
                  pl.BlockSpec((1, D_MODEL, D_EXPERT), lambda i, be, bv, bf: (be[i], 0, 0)),
                  pl.BlockSpec((1, D_MODEL, D_EXPERT), lambda i, be, bv, bf: (be[i], 0, 0)),
                  pl.BlockSpec((1, D_EXPERT, D_MODEL), lambda i, be, bv, bf: (be[i], 0, 0))],
        out_specs=pl.BlockSpec((EXPERT_BLOCK, D_MODEL), lambda i, be, bv, bf: (i, 0)),
        scratch_shapes=[pltpu.VMEM((D_MODEL, D_EXPERT), BF16), pltpu.VMEM((D_MODEL, D_EXPERT), BF16),
                        pltpu.VMEM((D_EXPERT, D_MODEL), BF16)])
    return pl.pallas_call(
        _expert_kernel, grid_spec=grid_spec,
        out_shape=jax.ShapeDtypeStruct((n_slots, D_MODEL), F32),
        compiler_params=_cparams(("arbitrary",)), name="experts",
    )(blk_expert, blk_valid, blk_first, xs, wg, wu, wd)


def _combine_kernel(x_ref, ya_ref, yb_ref, route_ref, g_ref, o_ref, *, final):
    route = route_ref[...]
    wa = route[:, 2:3]
    wb = route[:, 3:4]
    x = x_ref[...] + (ya_ref[...] * wa + yb_ref[...] * wb)
    if final:
        x = _rms(x, g_ref[...])
    o_ref[...] = x


def _combine(x, yg, route, g, final, tm):
    T = x.shape[0]
    nt = T // tm
    return pl.pallas_call(
        functools.partial(_combine_kernel, final=final), grid=(nt,),
        in_specs=[pl.BlockSpec((tm, D_MODEL), lambda i: (i, 0)),
                  pl.BlockSpec((tm, D_MODEL), lambda i: (i, 0)),
                  pl.BlockSpec((tm, D_MODEL), lambda i: (i + nt, 0)),
                  pl.BlockSpec((tm, ROUTE_LANES), lambda i: (i, 0)),
                  pl.BlockSpec((1, D_MODEL), lambda i: (0, 0))],
        out_specs=pl.BlockSpec((tm, D_MODEL), lambda i: (i, 0)),
        out_shape=jax.ShapeDtypeStruct((T, D_MODEL), F32),
        compiler_params=_cparams(("parallel",)), name="combine",
    )(x, yg, yg, route, g)


def _dispatch_plan(route, cnt, T):
    A = T * TOP_K
    expert = route[:, 0:TOP_K].astype(jnp.int32)
    rank = route[:, 4:4 + TOP_K].astype(jnp.int32)
    counts = cnt[0, :N_EXPERTS].astype(jnp.int32)
    padded = (counts + EXPERT_BLOCK - 1) // EXPERT_BLOCK * EXPERT_BLOCK
    pad_end = jnp.cumsum(padded)
    pad_start = pad_end - padded
    slot = pad_start[expert] + rank
    n_blocks = A // EXPERT_BLOCK + N_EXPERTS
    n_slots = n_blocks * EXPERT_BLOCK
    tok = jnp.broadcast_to(jnp.arange(T, dtype=jnp.int32)[:, None], (T, TOP_K))
    slot_tok = jnp.zeros((n_slots,), jnp.int32).at[slot.reshape(A)].set(tok.reshape(A), unique_indices=True)
    blk_start = jnp.arange(n_blocks, dtype=jnp.int32) * EXPERT_BLOCK
    blk_expert = jnp.minimum(jnp.searchsorted(pad_end, blk_start, side='right'),
                             N_EXPERTS - 1).astype(jnp.int32)
    blk_valid = (blk_start < pad_end[-1]).astype(jnp.int32)
    blk_first = jnp.concatenate([jnp.ones((1,), jnp.int32),
                                 (blk_expert[1:] != blk_expert[:-1]).astype(jnp.int32)])
    comb_idx = jnp.concatenate([slot[:, 0], slot[:, 1]])
    return slot_tok, blk_expert, blk_valid, blk_first, comb_idx


def _rope_tables(seq, dim, period, first):
    half = dim // 2
    inv = ROPE_THETA ** (-jnp.arange(0, dim, 2, dtype=F32) / dim)
    ang = jnp.arange(seq, dtype=F32)[:, None] * inv[None, :]
    cos, sin = jnp.cos(ang), jnp.sin(ang)
    d = jnp.arange(LANES) % period - first
    in_a = (d >= 0) & (d < half)
    in_b = (d >= half) & (d < dim)
    idx = jnp.clip(jnp.where(in_b, d - half, d), 0, half - 1)
    c = jnp.where((in_a | in_b)[None, :], cos[:, idx], 1.0)
    sa = jnp.where(in_a[None, :], -sin[:, idx], 0.0)
    sb = jnp.where(in_b[None, :], sin[:, idx], 0.0)
    return c, sa, sb


def _pad_heads(w, width):
    k = w.shape[0]
    w = w.reshape(k, MLA_HEADS, width)
    return jnp.pad(w, ((0, 0), (0, 0), (0, MLA_SLOT - width))).reshape(k, MLA_PAD)


def _prep_layer(l, norm1_g, w_in, q_norm_g, w_uq, kv_norm_g, w_uk, w_uv, w_pa, w_pb, w_o,
                norm2_g, w_rg, b_rg, w_re, b_re, w_e_gate, w_e_up, w_e_down):
    w = w_in[l]
    o = 0
    parts = []
    for n in (DIL_WIDTH, DIL_WIDTH, DIL_WIDTH, Q_LORA, KV_LORA, MLA_ROPE, 2 * D_MODEL):
        parts.append(w[:, o:o + n])
        o += n
    wq, wk, wv, wcq, wckv, wkr, wg = parts
    wkr_pad = jnp.pad(wkr, ((0, 0), (MLA_NOPE, MLA_SLOT - MLA_NOPE - MLA_ROPE)))
    wr = jnp.pad(jnp.concatenate([w_re[l], w_rg[l]], axis=1),
                 ((0, 0), (0, ROUTE_LANES - N_EXPERTS - N_GROUPS)))
    wrh = wr.astype(BF16)
    wrl = (wr - wrh.astype(F32)).astype(BF16)
    br = jnp.pad(jnp.concatenate([b_re[l], b_rg[l]]), (0, ROUTE_LANES - N_EXPERTS - N_GROUPS))
    return dict(
        g1=norm1_g[l][None, :], wq=wq.astype(BF16), wk=wk.astype(BF16), wv=wv.astype(BF16),
        wcq=wcq.astype(BF16), wckv=wckv.astype(BF16), wkr=wkr_pad.astype(BF16), wg=wg.astype(BF16),
        qn=q_norm_g[l][None, :], kvn=kv_norm_g[l][None, :],
        wuq=_pad_heads(w_uq[l], MLA_QK).astype(BF16), wuk=_pad_heads(w_uk[l], MLA_NOPE).astype(BF16),
        wuv=_pad_heads(w_uv[l], MLA_V).astype(BF16),
        wpa=w_pa[l].astype(BF16), wpb=w_pb[l].astype(BF16), wo=w_o[l].astype(BF16),
        g2=norm2_g[l][None, :], wrh=wrh, wrl=wrl, br=br[None, :].astype(F32),
        weg=w_e_gate[l], weu=w_e_up[l], wed=w_e_down[l])


def _trunk(x3, layers, final_g, tm=256):
    batch, seq, _ = x3.shape
    T = batch * seq
    x = x3.reshape(T, D_MODEL)
    tabs = _rope_tables(seq, ROT_DIM, HEAD_DIM, 0) + _rope_tables(seq, MLA_ROPE, MLA_SLOT, MLA_NOPE)
    fg = final_g[None, :]
    for l, lw in enumerate(layers):
        qa, ka, va, qm, km, vm, gates = _inproj(x, lw, tabs, seq, tm)
        dil = []
        for g, (_, d) in enumerate(DIL_GROUPS):
            dil.extend(_dilated_group(qa, ka, va, batch, seq, g, d))
        ob = _mla(qm, km, vm, batch, seq)
        xm, h2, route, cnt = _outproj(x, dil, ob, gates, lw, tm)
        slot_tok, blk_expert, blk_valid, blk_first, comb_idx = _dispatch_plan(route, cnt, T)
        xs = _gather_rows(h2, slot_tok)
        ys = _experts(xs, blk_expert, blk_valid, blk_first, lw['weg'], lw['weu'], lw['wed'])
        yg = _gather_rows(ys, comb_idx)
        x = _combine(xm, yg, route, fg, l == len(layers) - 1, tm)
    return x.reshape(batch, seq, D_MODEL)


def kernel(x_prompt, x_sample, norm1_g, w_in, q_norm_g, w_uq, kv_norm_g, w_uk, w_uv, w_pa, w_pb, w_o,
           norm2_g, w_rg, b_rg, w_re, b_re, w_e_gate, w_e_up, w_e_down, final_g):
    layers = [_prep_layer(l, norm1_g, w_in, q_norm_g, w_uq, kv_norm_g, w_uk, w_uv, w_pa, w_pb, w_o,
                          norm2_g, w_rg, b_rg, w_re, b_re, w_e_gate, w_e_up, w_e_down)
              for l in range(DEPTH)]
    return (_trunk(x_prompt, layers, final_g), _trunk(x_sample, layers, final_g))
```

```python
import functools

import jax
import jax.numpy as jnp
from jax import lax
from jax.experimental import pallas as pl
from jax.experimental.pallas import tpu as pltpu
from jax.experimental.pallas import tpu_sc as plsc

D_MODEL = 1024
DEPTH = 2
HEAD_DIM = 64
DIL_GROUPS = ((128, 1), (512, 4), (2048, 16))
DIL_HEADS_PER_GROUP = 4
DIL_WIDTH = 768
DIL_OUT = 256
DIL_SCALE = HEAD_DIM ** -0.5
ROT_DIM = 16
ROPE_THETA = 500000.0
DIL_HALF = 64

MLA_HEADS = 8
MLA_NOPE = 64
MLA_ROPE = 32
MLA_QK = 96
MLA_V = 64
MLA_OUT = 512
MLA_SCALE = MLA_QK ** -0.5
Q_LORA = 256
KV_LORA = 128

N_GROUPS = 8
EXPERTS_PER_GROUP = 8
N_EXPERTS = 64
TOP_K = 2
D_EXPERT = 512
EPS = 1e-6

LANES = 128
MLA_SLOT = LANES
MLA_PAD = MLA_HEADS * MLA_SLOT
ROUTE_LANES = LANES
VMEM_LIMIT = 56 * 1024 * 1024

BF16 = jnp.bfloat16
F32 = jnp.float32
NEG = -1e30
LOG2E = 1.4426950408889634


def _cparams(sem):
    return pltpu.CompilerParams(dimension_semantics=sem, vmem_limit_bytes=VMEM_LIMIT)


def _rms(t, g):
    return t * lax.rsqrt(jnp.mean(t * t, axis=-1, keepdims=True) + EPS) * g


def _rope_chunk(t, c, sa, sb, shift):
    return t * c + pltpu.roll(t, LANES - shift, 1) * sa + pltpu.roll(t, shift, 1) * sb


def _inproj_kernel(x_ref, g1_ref, wq_ref, wk_ref, wv_ref, wcq_ref, wckv_ref, wkr_ref, wg_ref,
                   qn_ref, kvn_ref, wuq_ref, wuk_ref, wuv_ref,
                   ca_ref, saa_ref, sab_ref, cb_ref, sba_ref, sbb_ref,
                   q0_ref, q1_ref, q2_ref, k0_ref, k1_ref, k2_ref, v0_ref, v1_ref, v2_ref,
                   qm_ref, km_ref, vm_ref, gate_ref, q_sc, k_sc, v_sc):
    x = x_ref[...]
    h = _rms(x, g1_ref[...]).astype(BF16)
    tm = x.shape[0]

    ca, saa, sab = ca_ref[...], saa_ref[...], sab_ref[...]
    q = jnp.dot(h, wq_ref[...], preferred_element_type=F32)
    k = jnp.dot(h, wk_ref[...], preferred_element_type=F32)
    v = jnp.dot(h, wv_ref[...], preferred_element_type=F32)
    for j in range(DIL_WIDTH // LANES):
        sl = slice(j * LANES, (j + 1) * LANES)
        q_sc[j] = _rope_chunk(q[:, sl], ca, saa, sab, ROT_DIM // 2) * DIL_SCALE
        k_sc[j] = _rope_chunk(k[:, sl], ca, saa, sab, ROT_DIM // 2)
        v_sc[j] = v[:, sl]
    halves = DIL_OUT // LANES
    for src, outs in ((q_sc, (q0_ref, q1_ref, q2_ref)), (k_sc, (k0_ref, k1_ref, k2_ref)),
                      (v_sc, (v0_ref, v1_ref, v2_ref))):
        for g, (_, dil) in enumerate(DIL_GROUPS):
            for hf in range(halves):
                for r in range(dil):
                    rows = pl.ds(r, tm // dil, stride=dil) if dil > 1 else slice(None)
                    outs[g][0, r, :, hf * LANES:(hf + 1) * LANES] = src[g * halves + hf, rows, :].astype(BF16)

    cb, sba, sbb = cb_ref[...], sba_ref[...], sbb_ref[...]
    cq = jnp.dot(h, wcq_ref[...], preferred_element_type=F32)
    cqn = _rms(cq, qn_ref[...]).astype(BF16)
    qm = jnp.dot(cqn, wuq_ref[...], preferred_element_type=F32) * (MLA_SCALE * LOG2E)
    ckv = jnp.dot(h, wckv_ref[...], preferred_element_type=F32)
    c = _rms(ckv, kvn_ref[...]).astype(BF16)
    kn = jnp.dot(c, wuk_ref[...], preferred_element_type=F32)
    vv = jnp.dot(c, wuv_ref[...], preferred_element_type=F32)
    kr = jnp.dot(h, wkr_ref[...], preferred_element_type=F32)
    kr = _rope_chunk(kr, cb, sba, sbb, MLA_ROPE // 2)
    lane = lax.broadcasted_iota(jnp.int32, (1, LANES), 1)
    ones_col = jnp.where(lane == MLA_V, 1.0, 0.0).astype(F32)
    for j in range(MLA_HEADS):
        sl = slice(j * LANES, (j + 1) * LANES)
        qm_ref[:, sl] = _rope_chunk(qm[:, sl], cb, sba, sbb, MLA_ROPE // 2).astype(BF16)
        km_ref[:, sl] = (kn[:, sl] + kr).astype(BF16)
        vm_ref[:, sl] = (vv[:, sl] + ones_col).astype(BF16)

    gate_ref[...] = jax.nn.sigmoid(jnp.dot(h, wg_ref[...], preferred_element_type=F32))


def _inproj(x, lw, tabs, seq, tm):
    T = x.shape[0]
    nt = seq // tm
    row = lambda i: (i, 0)
    const = lambda i: (0, 0)
    tab = lambda i: (i % nt, 0)

    def wspec(a):
        return pl.BlockSpec(a.shape, const)

    weights = [lw['g1'], lw['wq'], lw['wk'], lw['wv'], lw['wcq'], lw['wckv'], lw['wkr'], lw['wg'],
               lw['qn'], lw['kvn'], lw['wuq'], lw['wuk'], lw['wuv']]
    in_specs = ([pl.BlockSpec((tm, D_MODEL), row)] + [wspec(a) for a in weights]
                + [pl.BlockSpec((tm, LANES), tab)] * 6)
    batch = T // seq
    res_map = lambda i: (i // nt, 0, i % nt, 0)
    dil_shape = [jax.ShapeDtypeStruct((batch, d, seq // d, DIL_OUT), BF16) for _, d in DIL_GROUPS] * 3
    dil_specs = [pl.BlockSpec((1, d, tm // d, DIL_OUT), res_map) for _, d in DIL_GROUPS] * 3
    out_shape = dil_shape + [jax.ShapeDtypeStruct((T, MLA_PAD), BF16)] * 3 \
        + [jax.ShapeDtypeStruct((T, 2 * D_MODEL), F32)]
    out_specs = dil_specs + [pl.BlockSpec((tm, MLA_PAD), row)] * 3 + [pl.BlockSpec((tm, 2 * D_MODEL), row)]
    return pl.pallas_call(
        _inproj_kernel, grid=(T // tm,), in_specs=in_specs, out_specs=out_specs, out_shape=out_shape,
        scratch_shapes=[pltpu.VMEM((DIL_WIDTH // LANES, tm, LANES), F32)] * 3,
        compiler_params=_cparams(("parallel",)), name="inproj",
    )(x, *weights, *tabs)


DIL_STEP_TOKENS = 2048


def _dil_kernel(q_ref, kp_ref, kc_ref, kn_ref, vp_ref, vc_ref, vn_ref, o_ref, lse_ref, *, tq, dil, sub_len):
    i = pl.program_id(1)
    qb = LANES
    nh = DIL_HEADS_PER_GROUP
    head = lax.broadcasted_iota(jnp.int32, (1, DIL_OUT), 1) // HEAD_DIM
    qloc = lax.broadcasted_iota(jnp.int32, (nh * qb, 2 * qb), 0) % qb
    rel = lax.broadcasted_iota(jnp.int32, (nh * qb, 2 * qb), 1) - DIL_HALF - qloc
    band = jnp.abs(rel) <= DIL_HALF
    kcol = lax.broadcasted_iota(jnp.int32, (1, 2 * qb), 1) - DIL_HALF

    def window(p_ref, c_ref, n_ref, r, j):
        lo, hi = j * qb - DIL_HALF, j * qb + 2 * qb - DIL_HALF
        parts = []
        if lo < 0:
            parts.append(p_ref[0, r, qb + lo:qb, :])
            lo = 0
        parts.append(c_ref[0, r, lo:min(hi, tq), :])
        if hi > tq:
            parts.append(n_ref[0, r, 0:hi - tq, :])
        return jnp.concatenate(parts, axis=0) if len(parts) > 1 else parts[0]

    for r in range(dil):
        for j in range(tq // qb):
            q = q_ref[0, r, j * qb:(j + 1) * qb, :]
            k = window(kp_ref, kc_ref, kn_ref, r, j)
            v = window(vp_ref, vc_ref, vn_ref, r, j)
            qs = jnp.concatenate([jnp.where(head == hd, q, jnp.zeros_like(q)) for hd in range(nh)], axis=0)
            s = lax.dot_general(qs, k, (((1,), (1,)), ((), ())), preferred_element_type=F32)
            kpos = kcol + (i * tq + j * qb)
            s = jnp.where(band & (kpos >= 0) & (kpos < sub_len), s, NEG)
            m = jnp.max(s, axis=-1, keepdims=True)
            p = jnp.exp(s - m)
            den = jnp.sum(p, axis=-1, keepdims=True)
            oh = jnp.dot(p.astype(BF16), v, preferred_element_type=F32) * (1.0 / den)
            lse = m + jnp.log(den)
            o_acc = oh[0:qb]
            l_acc = jnp.broadcast_to(lse[0:qb], (qb, DIL_OUT))
            for hd in range(1, nh):
                o_acc = jnp.where(head == hd, oh[hd * qb:(hd + 1) * qb], o_acc)
                l_acc = jnp.where(head == hd, lse[hd * qb:(hd + 1) * qb], l_acc)
            rows = pl.ds(j * qb * dil + r, qb, stride=dil) if dil > 1 else slice(j * qb, (j + 1) * qb)
            for hf in range(DIL_OUT // LANES):
                o_ref[0, hf, rows, :] = o_acc[:, hf * LANES:(hf + 1) * LANES]
                lse_ref[0, hf, rows, :] = l_acc[:, hf * LANES:(hf + 1) * LANES]


def _dilated_group(qd, kd, vd, batch, seq, g, dil):
    L = seq // dil
    tq = min(L, DIL_STEP_TOKENS // dil)
    nq = L // tq
    hb = tq // LANES
    nhb = L // LANES
    cur = lambda b, i: (b, 0, i, 0)
    prev = lambda b, i: (b, 0, jnp.maximum(i * hb - 1, 0), 0)
    nxt = lambda b, i: (b, 0, jnp.minimum((i + 1) * hb, nhb - 1), 0)
    blk = (1, dil, tq, DIL_OUT)
    halo = (1, dil, LANES, DIL_OUT)
    halves = DIL_OUT // LANES
    out_blk = (1, halves, tq * dil, LANES)
    out_map = lambda b, i: (b, 0, i, 0)
    o, lse = pl.pallas_call(
        functools.partial(_dil_kernel, tq=tq, dil=dil, sub_len=L),
        grid=(batch, nq),
        in_specs=[pl.BlockSpec(blk, cur),
                  pl.BlockSpec(halo, prev), pl.BlockSpec(blk, cur), pl.BlockSpec(halo, nxt),
                  pl.BlockSpec(halo, prev), pl.BlockSpec(blk, cur), pl.BlockSpec(halo, nxt)],
        out_specs=[pl.BlockSpec(out_blk, out_map), pl.BlockSpec(out_blk, out_map)],
        out_shape=[jax.ShapeDtypeStruct((batch, halves, seq, LANES), F32)] * 2,
        compiler_params=_cparams(("parallel", "parallel")), name=f"dilated_g{g}",
    )(qd, kd, kd, kd, vd, vd, vd)
    return o, lse


def _mla_kernel(q_ref, k_ref, v_ref, o_ref, m_sc, acc_sc, *, tk, nk, unroll):
    nc = tk // LANES
    m_sc[...] = jnp.full(m_sc.shape, NEG, F32)
    acc_sc[...] = jnp.zeros(acc_sc.shape, F32)

    def step(off, hh):
        sl = slice(hh * LANES, (hh + 1) * LANES)
        q = q_ref[0, :, sl]
        k = k_ref[0, pl.ds(off, tk), sl]
        v = v_ref[0, pl.ds(off, tk), sl]
        s = lax.dot_general(q, k, (((1,), (1,)), ((), ())), preferred_element_type=F32)
        cols = [s[:, c * LANES:(c + 1) * LANES] for c in range(nc)]
        m_old = m_sc[hh]
        m_new = jnp.maximum(m_old, jnp.max(functools.reduce(jnp.maximum, cols), axis=-1, keepdims=True))
        p = jnp.concatenate([jnp.exp2(c - m_new) for c in cols], axis=1).astype(BF16)
        acc_sc[hh] = acc_sc[hh] * jnp.exp2(m_old - m_new) + jnp.dot(p, v, preferred_element_type=F32)
        m_sc[hh] = m_new

    def body(j, carry):
        for u in range(unroll):
            off = pl.multiple_of((j * unroll + u) * tk, tk)
            for hh in range(2):
                step(off, hh)
        return carry

    lax.fori_loop(0, nk // unroll, body, 0)
    outs = []
    for hh in range(2):
        acc = acc_sc[hh]
        outs.append(acc * (1.0 / acc[:, MLA_V:MLA_V + 1]))
    lane = lax.broadcasted_iota(jnp.int32, (1, LANES), 1)
    o_ref[0] = jnp.where(lane < MLA_V, outs[0], pltpu.roll(outs[1], MLA_V, 1)).astype(o_ref.dtype)


def _mla(qm, km, vm, batch, seq, tq=1024, tk=2048, unroll=1):
    tq = min(seq, tq)
    tk = min(seq, tk)
    nk = seq // tk
    unroll = unroll if nk % unroll == 0 else 1
    q3 = qm.reshape(batch, seq, MLA_PAD)
    k3 = km.reshape(batch, seq, MLA_PAD)
    v3 = vm.reshape(batch, seq, MLA_PAD)
    resident = lambda b, h, i: (b, 0, h)
    o = pl.pallas_call(
        functools.partial(_mla_kernel, tk=tk, nk=nk, unroll=unroll),
        grid=(batch, MLA_HEADS // 2, seq // tq),
        in_specs=[pl.BlockSpec((1, tq, 2 * LANES), lambda b, h, i: (b, i, h)),
                  pl.BlockSpec((1, seq, 2 * LANES), resident, pipeline_mode=pl.Buffered(1)),
                  pl.BlockSpec((1, seq, 2 * LANES), resident, pipeline_mode=pl.Buffered(1))],
        out_specs=pl.BlockSpec((1, tq, LANES), lambda b, h, i: (b, i, h)),
        out_shape=jax.ShapeDtypeStruct((batch, seq, MLA_OUT), BF16),
        scratch_shapes=[pltpu.VMEM((2, tq, LANES), F32), pltpu.VMEM((2, tq, LANES), F32)],
        compiler_params=_cparams(("parallel", "parallel", "arbitrary")), name="mla",
    )(q3, k3, v3)
    return o.reshape(batch * seq, MLA_OUT)


def _outproj_kernel(x_ref, o0_ref, l0_ref, o1_ref, l1_ref, o2_ref, l2_ref, ob_ref, gate_ref,
                    wpa_ref, wpb_ref, wo_ref, g2_ref, wrh_ref, wrl_ref, br_ref,
                    xm_ref, h2_ref, route_ref, cnt_ref, cnt_sc):
    def halves(ref):
        return jnp.concatenate([ref[0, hf] for hf in range(ref.shape[1])], axis=1)

    l0, l1, l2 = halves(l0_ref), halves(l1_ref), halves(l2_ref)
    m = jnp.maximum(jnp.maximum(l0, l1), l2)
    w0, w1, w2 = jnp.exp(l0 - m), jnp.exp(l1 - m), jnp.exp(l2 - m)
    oa = (w0 * halves(o0_ref) + w1 * halves(o1_ref) + w2 * halves(o2_ref)) / (w0 + w1 + w2)
    pa = jnp.dot(oa.astype(BF16), wpa_ref[...], preferred_element_type=F32)
    pb = jnp.dot(ob_ref[...], wpb_ref[...], preferred_element_type=F32)
    merged = gate_ref[:, :D_MODEL] * pa + gate_ref[:, D_MODEL:] * pb
    xm = x_ref[...] + jnp.dot(merged.astype(BF16), wo_ref[...], preferred_element_type=F32)
    xm_ref[...] = xm
    h2 = _rms(xm, g2_ref[...])
    h2_ref[...] = h2

    hi = h2.astype(BF16)
    lo = (h2 - hi.astype(F32)).astype(BF16)
    lg = (jnp.dot(hi, wrh_ref[...], preferred_element_type=F32)
          + jnp.dot(lo, wrh_ref[...], preferred_element_type=F32)
          + jnp.dot(hi, wrl_ref[...], preferred_element_type=F32)) + br_ref[...]
    lane = lax.broadcasted_iota(jnp.int32, lg.shape, 1)
    lanef = lane.astype(F32)
    big = float(ROUTE_LANES)
    gmask = (lane >= N_EXPERTS) & (lane < N_EXPERTS + N_GROUPS)
    gl = jnp.where(gmask, lg, NEG)
    gmax = jnp.max(gl, axis=-1, keepdims=True)
    gidx = jnp.min(jnp.where(gl == gmax, lanef, big), axis=-1, keepdims=True) - float(N_EXPERTS)
    p_grp = 1.0 / jnp.sum(jnp.where(gmask, jnp.exp(gl - gmax), 0.0), axis=-1, keepdims=True)
    emask = (lane // EXPERTS_PER_GROUP).astype(F32) == gidx
    el = jnp.where(emask, lg, NEG)
    m1 = jnp.max(el, axis=-1, keepdims=True)
    i1 = jnp.min(jnp.where(el == m1, lanef, big), axis=-1, keepdims=True)
    el2 = jnp.where(lanef == i1, NEG, el)
    m2 = jnp.max(el2, axis=-1, keepdims=True)
    i2 = jnp.min(jnp.where(el2 == m2, lanef, big), axis=-1, keepdims=True)
    r = jnp.exp(m2 - m1)
    wa = p_grp / (1.0 + r)
    wb = p_grp * r / (1.0 + r)
    @pl.when(pl.program_id(0) == 0)
    def _():
        cnt_sc[...] = jnp.zeros(cnt_sc.shape, F32)

    oh1 = lanef == i1
    oh2 = lanef == i2
    oh = jnp.where(oh1 | oh2, 1.0, 0.0)
    tm = lg.shape[0]
    earlier = (lax.broadcasted_iota(jnp.int32, (tm, tm), 0) > lax.broadcasted_iota(jnp.int32, (tm, tm), 1))
    before = jnp.dot(jnp.where(earlier, 1.0, 0.0).astype(BF16), oh.astype(BF16),
                     preferred_element_type=F32) + cnt_sc[0:1, :]
    r1 = jnp.sum(jnp.where(oh1, before, 0.0), axis=-1, keepdims=True)
    r2 = jnp.sum(jnp.where(oh2, before, 0.0), axis=-1, keepdims=True)
    cnt = cnt_sc[...] + jnp.sum(oh, axis=0, keepdims=True)
    cnt_sc[...] = cnt
    cnt_ref[...] = cnt
    route_ref[...] = jnp.where(lane == 0, i1, jnp.where(lane == 1, i2, jnp.where(lane == 2, wa,
                               jnp.where(lane == 3, wb, jnp.where(lane == 4, r1,
                                                                  jnp.where(lane == 5, r2, 0.0))))))


def _outproj(x, dil_outs, ob, gates, lw, seq, tm):
    T = x.shape[0]
    nt = seq // tm
    row = lambda i: (i, 0)
    const = lambda i: (0, 0)
    weights = [lw['wpa'], lw['wpb'], lw['wo'], lw['g2'], lw['wrh'], lw['wrl'], lw['br']]
    acts = [x] + list(dil_outs) + [ob, gates]
    dil_spec = pl.BlockSpec((1, DIL_OUT // LANES, tm, LANES), lambda i: (i // nt, 0, i % nt, 0))
    in_specs = [pl.BlockSpec((tm, D_MODEL), row)] + [dil_spec] * len(dil_outs) \
        + [pl.BlockSpec((tm, ob.shape[1]), row), pl.BlockSpec((tm, gates.shape[1]), row)] \
        + [pl.BlockSpec(a.shape, const) for a in weights]
    return pl.pallas_call(
        _outproj_kernel, grid=(T // tm,), in_specs=in_specs,
        out_specs=[pl.BlockSpec((tm, D_MODEL), row), pl.BlockSpec((tm, D_MODEL), row),
                   pl.BlockSpec((tm, ROUTE_LANES), row), pl.BlockSpec((8, ROUTE_LANES), const)],
        out_shape=[jax.ShapeDtypeStruct((T, D_MODEL), F32), jax.ShapeDtypeStruct((T, D_MODEL), F32),
                   jax.ShapeDtypeStruct((T, ROUTE_LANES), F32), jax.ShapeDtypeStruct((8, ROUTE_LANES), F32)],
        scratch_shapes=[pltpu.VMEM((8, ROUTE_LANES), F32)],
        compiler_params=_cparams(("arbitrary",)), name="outproj_router",
    )(*acts, *weights)


GATHER_WINDOW = 32


def _gather_rows(src, idx):
    m = idx.shape[0]
    width = src.shape[1]
    info = plsc.get_sparse_core_info()
    n_workers = info.num_cores * info.num_subcores
    per_w = m // n_workers
    n_pairs = per_w // (2 * GATHER_WINDOW)
    assert n_pairs * 2 * GATHER_WINDOW * n_workers == m
    mesh = plsc.VectorSubcoreMesh(core_axis_name="core", subcore_axis_name="subcore")

    @functools.partial(
        pl.kernel, out_type=jax.ShapeDtypeStruct((m, width), src.dtype), mesh=mesh, name="gather_rows",
        scratch_types=[pltpu.VMEM((per_w,), jnp.int32),
                       pltpu.VMEM((2, GATHER_WINDOW, width), src.dtype),
                       pltpu.SemaphoreType.DMA((2,))])
    def gather(src_hbm, idx_hbm, out_hbm, idx_v, rows_v, sems):
        wid = lax.axis_index("subcore") * info.num_cores + lax.axis_index("core")
        base = wid * per_w
        pltpu.sync_copy(idx_hbm.at[pl.ds(base, per_w)], idx_v)

        def fetch(c, b):
            return pltpu.make_async_copy(src_hbm.at[idx_v.at[pl.ds(c * GATHER_WINDOW, GATHER_WINDOW)]],
                                         rows_v.at[b], sems.at[b])

        def flush(c, b):
            pltpu.sync_copy(rows_v.at[b], out_hbm.at[pl.ds(base + c * GATHER_WINDOW, GATHER_WINDOW)])

        fetch(0, 0).start()

        @pl.loop(0, n_pairs)
        def _(p):
            c = 2 * p
            fetch(c + 1, 1).start()
            fetch(c, 0).wait()
            flush(c, 0)

            @pl.when(p + 1 < n_pairs)
            def _():
                fetch(c + 2, 0).start()

            fetch(c + 1, 1).wait()
            flush(c + 1, 1)

    return gather(src, idx)


EXPERT_BLOCK = 256


def _expert_kernel(be_ref, bv_ref, bf_ref, xs_ref, wg_ref, wu_ref, wd_ref, y_ref, wg_sc, wu_sc, wd_sc):
    i = pl.program_id(0)

    @pl.when(bf_ref[i] > 0)
    def _():
        wg_sc[...] = wg_ref[0].astype(BF16)
        wu_sc[...] = wu_ref[0].astype(BF16)
        wd_sc[...] = wd_ref[0].astype(BF16)

    @pl.when(bv_ref[i] > 0)
    def _():
        xb = xs_ref[...].astype(BF16)
        g = jnp.dot(xb, wg_sc[...], preferred_element_type=F32)
        u = jnp.dot(xb, wu_sc[...], preferred_element_type=F32)
        hb = (g * jax.nn.sigmoid(g) * u).astype(BF16)
        y_ref[...] = jnp.dot(hb, wd_sc[...], preferred_element_type=F32)

    @pl.when(bv_ref[i] == 0)
    def _():
        y_ref[...] = jnp.zeros(y_ref.shape, F32)


def _experts(xs, blk_expert, blk_valid, blk_first, wg, wu, wd, layer):
    n_slots = xs.shape[0]
    nb = n_slots // EXPERT_BLOCK
    grid_spec = pltpu.PrefetchScalarGridSpec(
        num_scalar_prefetch=3, grid=(nb,),
        in_specs=[pl.BlockSpec((EXPERT_BLOCK, D_MODEL), lambda i, be, bv, bf: (i, 0)),
                  pl.BlockSpec((None, 1, D_MODEL, D_EXPERT), lambda i, be, bv, bf: (layer, be[i], 0, 0)),
                  pl.BlockSpec((None, 1, D_MODEL, D_EXPERT), lambda i, be, bv, bf: (layer, be[i], 0, 0)),
                  pl.BlockSpec((None, 1, D_EXPERT, D_MODEL), lambda i, be, bv, bf: (layer, be[i], 0, 0))],
        out_specs=pl.BlockSpec((EXPERT_BLOCK, D_MODEL), lambda i, be, bv, bf: (i, 0)),
        scratch_shapes=[pltpu.VMEM((D_MODEL, D_EXPERT), BF16), pltpu.VMEM((D_MODEL, D_EXPERT), BF16),
                        pltpu.VMEM((D_EXPERT, D_MODEL), BF16)])
    return pl.pallas_call(
        _expert_kernel, grid_spec=grid_spec,
        out_shape=jax.ShapeDtypeStruct((n_slots, D_MODEL), F32),
        compiler_params=_cparams(("arbitrary",)), name="experts",
    )(blk_expert, blk_valid, blk_first, xs, wg, wu, wd)


def _combine_kernel(x_ref, ya_ref, yb_ref, route_ref, g_ref, o_ref, *, final):
    route = route_ref[...]
    wa = route[:, 2:3]
    wb = route[:, 3:4]
    x = x_ref[...] + (ya_ref[...] * wa + yb_ref[...] * wb)
    if final:
        x = _rms(x, g_ref[...])
    o_ref[...] = x


def _combine(x, yg, route, g, final, tm):
    T = x.shape[0]
    nt = T // tm
    return pl.pallas_call(
        functools.partial(_combine_kernel, final=final), grid=(nt,),
        in_specs=[pl.BlockSpec((tm, D_MODEL), lambda i: (i, 0)),
                  pl.BlockSpec((tm, D_MODEL), lambda i: (i, 0)),
                  pl.BlockSpec((tm, D_MODEL), lambda i: (i + nt, 0)),
                  pl.BlockSpec((tm, ROUTE_LANES), lambda i: (i, 0)),
                  pl.BlockSpec((1, D_MODEL), lambda i: (0, 0))],
        out_specs=pl.BlockSpec((tm, D_MODEL), lambda i: (i, 0)),
        out_shape=jax.ShapeDtypeStruct((T, D_MODEL), F32),
        compiler_params=_cparams(("parallel",)), name="combine",
    )(x, yg, yg, route, g)


def _dispatch_plan(route, cnt, T):
    A = T * TOP_K
    expert = route[:, 0:TOP_K].astype(jnp.int32)
    rank = route[:, 4:4 + TOP_K].astype(jnp.int32)
    counts = cnt[0, :N_EXPERTS].astype(jnp.int32)
    padded = (counts + EXPERT_BLOCK - 1) // EXPERT_BLOCK * EXPERT_BLOCK
    pad_end = jnp.cumsum(padded)
    pad_start = pad_end - padded
    slot = pad_start[expert] + rank
    n_blocks = A // EXPERT_BLOCK + N_EXPERTS
    n_slots = n_blocks * EXPERT_BLOCK
    tok = jnp.broadcast_to(jnp.arange(T, dtype=jnp.int32)[:, None], (T, TOP_K))
    slot_tok = jnp.zeros((n_slots,), jnp.int32).at[slot.reshape(A)].set(tok.reshape(A), unique_indices=True)
    blk_start = jnp.arange(n_blocks, dtype=jnp.int32) * EXPERT_BLOCK
    blk_expert = jnp.minimum(jnp.sum((blk_start[:, None] >= pad_end[None, :]).astype(jnp.int32), axis=1),
                             N_EXPERTS - 1)
    blk_valid = (blk_start < pad_end[-1]).astype(jnp.int32)
    blk_first = jnp.concatenate([jnp.ones((1,), jnp.int32),
                                 (blk_expert[1:] != blk_expert[:-1]).astype(jnp.int32)])
    comb_idx = jnp.concatenate([slot[:, 0], slot[:, 1]])
    return slot_tok, blk_expert, blk_valid, blk_first, comb_idx


def _rope_tables(seq, dim, period, first):
    half = dim // 2
    inv = ROPE_THETA ** (-jnp.arange(0, dim, 2, dtype=F32) / dim)
    ang = jnp.arange(seq, dtype=F32)[:, None] * inv[None, :]
    cos, sin = jnp.cos(ang), jnp.sin(ang)
    d = jnp.arange(LANES) % period - first
    in_a = (d >= 0) & (d < half)
    in_b = (d >= half) & (d < dim)
    idx = jnp.clip(jnp.where(in_b, d - half, d), 0, half - 1)
    c = jnp.where((in_a | in_b)[None, :], cos[:, idx], 1.0)
    sa = jnp.where(in_a[None, :], -sin[:, idx], 0.0)
    sb = jnp.where(in_b[None, :], sin[:, idx], 0.0)
    return c, sa, sb


def _pad_heads(w, width):
    k = w.shape[0]
    w = w.reshape(k, MLA_HEADS, width)
    return jnp.pad(w, ((0, 0), (0, 0), (0, MLA_SLOT - width))).reshape(k, MLA_PAD)


def _prep_layer(l, norm1_g, w_in, q_norm_g, w_uq, kv_norm_g, w_uk, w_uv, w_pa, w_pb, w_o,
                norm2_g, w_rg, b_rg, w_re, b_re, w_e_gate, w_e_up, w_e_down):
    w = w_in[l]
    o = 0
    parts = []
    for n in (DIL_WIDTH, DIL_WIDTH, DIL_WIDTH, Q_LORA, KV_LORA, MLA_ROPE, 2 * D_MODEL):
        parts.append(w[:, o:o + n])
        o += n
    wq, wk, wv, wcq, wckv, wkr, wg = parts
    wkr_pad = jnp.pad(wkr, ((0, 0), (MLA_NOPE, MLA_SLOT - MLA_NOPE - MLA_ROPE)))
    wr = jnp.pad(jnp.concatenate([w_re[l], w_rg[l]], axis=1),
                 ((0, 0), (0, ROUTE_LANES - N_EXPERTS - N_GROUPS)))
    wrh = wr.astype(BF16)
    wrl = (wr - wrh.astype(F32)).astype(BF16)
    br = jnp.pad(jnp.concatenate([b_re[l], b_rg[l]]), (0, ROUTE_LANES - N_EXPERTS - N_GROUPS))
    return dict(
        g1=norm1_g[l][None, :], wq=wq.astype(BF16), wk=wk.astype(BF16), wv=wv.astype(BF16),
        wcq=wcq.astype(BF16), wckv=wckv.astype(BF16), wkr=wkr_pad.astype(BF16), wg=wg.astype(BF16),
        qn=q_norm_g[l][None, :], kvn=kv_norm_g[l][None, :],
        wuq=_pad_heads(w_uq[l], MLA_QK).astype(BF16), wuk=_pad_heads(w_uk[l], MLA_NOPE).astype(BF16),
        wuv=_pad_heads(w_uv[l], MLA_V).astype(BF16),
        wpa=w_pa[l].astype(BF16), wpb=w_pb[l].astype(BF16), wo=w_o[l].astype(BF16),
        g2=norm2_g[l][None, :], wrh=wrh, wrl=wrl, br=br[None, :].astype(F32),
        weg=w_e_gate, weu=w_e_up, wed=w_e_down)


def _trunk(x3, layers, final_g, tm=256):
    batch, seq, _ = x3.shape
    T = batch * seq
    x = x3.reshape(T, D_MODEL)
    tabs = _rope_tables(seq, ROT_DIM, HEAD_DIM, 0) + _rope_tables(seq, MLA_ROPE, MLA_SLOT, MLA_NOPE)
    fg = final_g[None, :]
    for l, lw in enumerate(layers):
        outs = _inproj(x, lw, tabs, seq, tm)
        n_g = len(DIL_GROUPS)
        qd, kd, vd = outs[0:n_g], outs[n_g:2 * n_g], outs[2 * n_g:3 * n_g]
        qm, km, vm, gates = outs[3 * n_g:]
        dil = []
        for g, (_, d) in enumerate(DIL_GROUPS):
            dil.extend(_dilated_group(qd[g], kd[g], vd[g], batch, seq, g, d))
        ob = _mla(qm, km, vm, batch, seq)
        xm, h2, route, cnt = _outproj(x, dil, ob, gates, lw, seq, tm)
        slot_tok, blk_expert, blk_valid, blk_first, comb_idx = _dispatch_plan(route, cnt, T)
        xs = _gather_rows(h2, slot_tok)
        ys = _experts(xs, blk_expert, blk_valid, blk_first, lw['weg'], lw['weu'], lw['wed'], l)
        yg = _gather_rows(ys, comb_idx)
        x = _combine(xm, yg, route, fg, l == len(layers) - 1, tm)
    return x.reshape(batch, seq, D_MODEL)


def kernel(x_prompt, x_sample, norm1_g, w_in, q_norm_g, w_uq, kv_norm_g, w_uk, w_uv, w_pa, w_pb, w_o,
           norm2_g, w_rg, b_rg, w_re, b_re, w_e_gate, w_e_up, w_e_down, final_g):
    layers = [_prep_layer(l, norm1_g, w_in, q_norm_g, w_uq, kv_norm_g, w_uk, w_uv, w_pa, w_pb, w_o,
                          norm2_g, w_rg, b_rg, w_re, b_re, w_e_gate, w_e_up, w_e_down)
              for l in range(DEPTH)]
    return (_trunk(x_prompt, layers, final_g), _trunk(x_sample, layers, final_g))
```

```python
import functools

import jax
import jax.numpy as jnp
from jax import lax
from jax.experimental import pallas as pl
from jax.experimental.pallas import tpu as pltpu
from jax.experimental.pallas import tpu_sc as plsc

D_MODEL = 1024
DEPTH = 2
HEAD_DIM = 64
DIL_GROUPS = ((128, 1), (512, 4), (2048, 16))
DIL_HEADS_PER_GROUP = 4
DIL_WIDTH = 768
DIL_OUT = 256
DIL_SCALE = HEAD_DIM ** -0.5
ROT_DIM = 16
ROPE_THETA = 500000.0
DIL_HALF = 64

MLA_HEADS = 8
MLA_NOPE = 64
MLA_ROPE = 32
MLA_QK = 96
MLA_V = 64
MLA_OUT = 512
MLA_SCALE = MLA_QK ** -0.5
Q_LORA = 256
KV_LORA = 128

N_GROUPS = 8
EXPERTS_PER_GROUP = 8
N_EXPERTS = 64
TOP_K = 2
D_EXPERT = 512
EPS = 1e-6

LANES = 128
MLA_SLOT = LANES
MLA_PAD = MLA_HEADS * MLA_SLOT
ROUTE_LANES = LANES
VMEM_LIMIT = 56 * 1024 * 1024

BF16 = jnp.bfloat16
F32 = jnp.float32
NEG = -1e30
LOG2E = 1.4426950408889634


def _cparams(sem):
    return pltpu.CompilerParams(dimension_semantics=sem, vmem_limit_bytes=VMEM_LIMIT)


def _rms(t, g):
    return t * lax.rsqrt(jnp.mean(t * t, axis=-1, keepdims=True) + EPS) * g


def _rope_chunk(t, c, sa, sb, shift):
    return t * c + pltpu.roll(t, LANES - shift, 1) * sa + pltpu.roll(t, shift, 1) * sb


def _inproj_kernel(x_ref, g1_ref, wq_ref, wk_ref, wv_ref, wcq_ref, wckv_ref, wkr_ref, wg_ref,
                   qn_ref, kvn_ref, wuq_ref, wuk_ref, wuv_ref,
                   ca_ref, saa_ref, sab_ref, cb_ref, sba_ref, sbb_ref,
                   q0_ref, q1_ref, q2_ref, k0_ref, k1_ref, k2_ref, v0_ref, v1_ref, v2_ref,
                   qm_ref, km_ref, vm_ref, gate_ref, q_sc, k_sc, v_sc):
    x = x_ref[...]
    h = _rms(x, g1_ref[...]).astype(BF16)
    tm = x.shape[0]

    ca, saa, sab = ca_ref[...], saa_ref[...], sab_ref[...]
    q = jnp.dot(h, wq_ref[...], preferred_element_type=F32)
    k = jnp.dot(h, wk_ref[...], preferred_element_type=F32)
    v = jnp.dot(h, wv_ref[...], preferred_element_type=F32)
    for j in range(DIL_WIDTH // LANES):
        sl = slice(j * LANES, (j + 1) * LANES)
        q_sc[j] = _rope_chunk(q[:, sl], ca, saa, sab, ROT_DIM // 2) * DIL_SCALE
        k_sc[j] = _rope_chunk(k[:, sl], ca, saa, sab, ROT_DIM // 2)
        v_sc[j] = v[:, sl]
    halves = DIL_OUT // LANES
    for src, outs in ((q_sc, (q0_ref, q1_ref, q2_ref)), (k_sc, (k0_ref, k1_ref, k2_ref)),
                      (v_sc, (v0_ref, v1_ref, v2_ref))):
        for g, (_, dil) in enumerate(DIL_GROUPS):
            for hf in range(halves):
                for r in range(dil):
                    rows = pl.ds(r, tm // dil, stride=dil) if dil > 1 else slice(None)
                    outs[g][0, r, :, hf * LANES:(hf + 1) * LANES] = src[g * halves + hf, rows, :].astype(BF16)

    cb, sba, sbb = cb_ref[...], sba_ref[...], sbb_ref[...]
    cq = jnp.dot(h, wcq_ref[...], preferred_element_type=F32)
    cqn = _rms(cq, qn_ref[...]).astype(BF16)
    qm = jnp.dot(cqn, wuq_ref[...], preferred_element_type=F32) * (MLA_SCALE * LOG2E)
    ckv = jnp.dot(h, wckv_ref[...], preferred_element_type=F32)
    c = _rms(ckv, kvn_ref[...]).astype(BF16)
    kn = jnp.dot(c, wuk_ref[...], preferred_element_type=F32)
    vv = jnp.dot(c, wuv_ref[...], preferred_element_type=F32)
    kr = jnp.dot(h, wkr_ref[...], preferred_element_type=F32)
    kr = _rope_chunk(kr, cb, sba, sbb, MLA_ROPE // 2)
    lane = lax.broadcasted_iota(jnp.int32, (1, LANES), 1)
    ones_col = jnp.where(lane == MLA_V, 1.0, 0.0).astype(F32)
    for j in range(MLA_HEADS):
        sl = slice(j * LANES, (j + 1) * LANES)
        qm_ref[:, sl] = _rope_chunk(qm[:, sl], cb, sba, sbb, MLA_ROPE // 2).astype(BF16)
        km_ref[:, sl] = (kn[:, sl] + kr).astype(BF16)
        vm_ref[:, sl] = (vv[:, sl] + ones_col).astype(BF16)

    gate_ref[...] = jax.nn.sigmoid(jnp.dot(h, wg_ref[...], preferred_element_type=F32))


def _inproj(x, lw, tabs, seq, tm):
    T = x.shape[0]
    nt = seq // tm
    row = lambda i: (i, 0)
    const = lambda i: (0, 0)
    tab = lambda i: (i % nt, 0)

    def wspec(a):
        return pl.BlockSpec(a.shape, const)

    weights = [lw['g1'], lw['wq'], lw['wk'], lw['wv'], lw['wcq'], lw['wckv'], lw['wkr'], lw['wg'],
               lw['qn'], lw['kvn'], lw['wuq'], lw['wuk'], lw['wuv']]
    in_specs = ([pl.BlockSpec((tm, D_MODEL), row)] + [wspec(a) for a in weights]
                + [pl.BlockSpec((tm, LANES), tab)] * 6)
    batch = T // seq
    res_map = lambda i: (i // nt, 0, i % nt, 0)
    dil_shape = [jax.ShapeDtypeStruct((batch, d, seq // d, DIL_OUT), BF16) for _, d in DIL_GROUPS] * 3
    dil_specs = [pl.BlockSpec((1, d, tm // d, DIL_OUT), res_map) for _, d in DIL_GROUPS] * 3
    out_shape = dil_shape + [jax.ShapeDtypeStruct((T, MLA_PAD), BF16)] * 3 \
        + [jax.ShapeDtypeStruct((T, 2 * D_MODEL), F32)]
    out_specs = dil_specs + [pl.BlockSpec((tm, MLA_PAD), row)] * 3 + [pl.BlockSpec((tm, 2 * D_MODEL), row)]
    return pl.pallas_call(
        _inproj_kernel, grid=(T // tm,), in_specs=in_specs, out_specs=out_specs, out_shape=out_shape,
        scratch_shapes=[pltpu.VMEM((DIL_WIDTH // LANES, tm, LANES), F32)] * 3,
        compiler_params=_cparams(("parallel",)), name="inproj",
    )(x, *weights, *tabs)


DIL_STEP_TOKENS = 2048


def _dil_kernel(q_ref, kp_ref, kc_ref, kn_ref, vp_ref, vc_ref, vn_ref, o_ref, lse_ref, *, tq, dil, sub_len):
    i = pl.program_id(1)
    qb = LANES
    nh = DIL_HEADS_PER_GROUP
    head = lax.broadcasted_iota(jnp.int32, (1, DIL_OUT), 1) // HEAD_DIM
    qloc = lax.broadcasted_iota(jnp.int32, (nh * qb, 2 * qb), 0) % qb
    rel = lax.broadcasted_iota(jnp.int32, (nh * qb, 2 * qb), 1) - DIL_HALF - qloc
    band = jnp.abs(rel) <= DIL_HALF
    kcol = lax.broadcasted_iota(jnp.int32, (1, 2 * qb), 1) - DIL_HALF

    def window(p_ref, c_ref, n_ref, r, j):
        lo, hi = j * qb - DIL_HALF, j * qb + 2 * qb - DIL_HALF
        parts = []
        if lo < 0:
            parts.append(p_ref[0, r, qb + lo:qb, :])
            lo = 0
        parts.append(c_ref[0, r, lo:min(hi, tq), :])
        if hi > tq:
            parts.append(n_ref[0, r, 0:hi - tq, :])
        return jnp.concatenate(parts, axis=0) if len(parts) > 1 else parts[0]

    for r in range(dil):
        for j in range(tq // qb):
            q = q_ref[0, r, j * qb:(j + 1) * qb, :]
            k = window(kp_ref, kc_ref, kn_ref, r, j)
            v = window(vp_ref, vc_ref, vn_ref, r, j)
            qs = jnp.concatenate([jnp.where(head == hd, q, jnp.zeros_like(q)) for hd in range(nh)], axis=0)
            s = lax.dot_general(qs, k, (((1,), (1,)), ((), ())), preferred_element_type=F32)
            kpos = kcol + (i * tq + j * qb)
            s = jnp.where(band & (kpos >= 0) & (kpos < sub_len), s, NEG)
            m = jnp.max(s, axis=-1, keepdims=True)
            p = jnp.exp(s - m)
            den = jnp.sum(p, axis=-1, keepdims=True)
            oh = jnp.dot(p.astype(BF16), v, preferred_element_type=F32) * (1.0 / den)
            lse = m + jnp.log(den)
            o_acc = oh[0:qb]
            l_acc = jnp.broadcast_to(lse[0:qb], (qb, DIL_OUT))
            for hd in range(1, nh):
                o_acc = jnp.where(head == hd, oh[hd * qb:(hd + 1) * qb], o_acc)
                l_acc = jnp.where(head == hd, lse[hd * qb:(hd + 1) * qb], l_acc)
            rows = pl.ds(j * qb * dil + r, qb, stride=dil) if dil > 1 else slice(j * qb, (j + 1) * qb)
            for hf in range(DIL_OUT // LANES):
                o_ref[0, hf, rows, :] = o_acc[:, hf * LANES:(hf + 1) * LANES]
                lse_ref[0, hf, rows, :] = l_acc[:, hf * LANES:(hf + 1) * LANES]


def _dilated_group(qd, kd, vd, batch, seq, g, dil):
    L = seq // dil
    tq = min(L, DIL_STEP_TOKENS // dil)
    nq = L // tq
    hb = tq // LANES
    nhb = L // LANES
    cur = lambda b, i: (b, 0, i, 0)
    prev = lambda b, i: (b, 0, jnp.maximum(i * hb - 1, 0), 0)
    nxt = lambda b, i: (b, 0, jnp.minimum((i + 1) * hb, nhb - 1), 0)
    blk = (1, dil, tq, DIL_OUT)
    halo = (1, dil, LANES, DIL_OUT)
    halves = DIL_OUT // LANES
    out_blk = (1, halves, tq * dil, LANES)
    out_map = lambda b, i: (b, 0, i, 0)
    o, lse = pl.pallas_call(
        functools.partial(_dil_kernel, tq=tq, dil=dil, sub_len=L),
        grid=(batch, nq),
        in_specs=[pl.BlockSpec(blk, cur),
                  pl.BlockSpec(halo, prev), pl.BlockSpec(blk, cur), pl.BlockSpec(halo, nxt),
                  pl.BlockSpec(halo, prev), pl.BlockSpec(blk, cur), pl.BlockSpec(halo, nxt)],
        out_specs=[pl.BlockSpec(out_blk, out_map), pl.BlockSpec(out_blk, out_map)],
        out_shape=[jax.ShapeDtypeStruct((batch, halves, seq, LANES), F32)] * 2,
        compiler_params=_cparams(("parallel", "parallel")), name=f"dilated_g{g}",
    )(qd, kd, kd, kd, vd, vd, vd)
    return o, lse


def _mla_kernel(q_ref, k_ref, v_ref, o_ref, m_sc, acc_sc, *, tk, nk, unroll):
    nc = tk // LANES
    m_sc[...] = jnp.full(m_sc.shape, NEG, F32)
    acc_sc[...] = jnp.zeros(acc_sc.shape, F32)

    def step(off, hh):
        sl = slice(hh * LANES, (hh + 1) * LANES)
        q = q_ref[0, :, sl]
        k = k_ref[0, pl.ds(off, tk), sl]
        v = v_ref[0, pl.ds(off, tk), sl]
        s = lax.dot_general(q, k, (((1,), (1,)), ((), ())), preferred_element_type=F32)
        cols = [s[:, c * LANES:(c + 1) * LANES] for c in range(nc)]
        m_old = m_sc[hh]
        m_new = jnp.maximum(m_old, jnp.max(functools.reduce(jnp.maximum, cols), axis=-1, keepdims=True))
        p = jnp.concatenate([jnp.exp2(c - m_new) for c in cols], axis=1).astype(BF16)
        acc_sc[hh] = acc_sc[hh] * jnp.exp2(m_old - m_new) + jnp.dot(p, v, preferred_element_type=F32)
        m_sc[hh] = m_new

    def body(j, carry):
        for u in range(unroll):
            off = pl.multiple_of((j * unroll + u) * tk, tk)
            for hh in range(2):
                step(off, hh)
        return carry

    lax.fori_loop(0, nk // unroll, body, 0)
    outs = []
    for hh in range(2):
        acc = acc_sc[hh]
        outs.append(acc * (1.0 / acc[:, MLA_V:MLA_V + 1]))
    lane = lax.broadcasted_iota(jnp.int32, (1, LANES), 1)
    o_ref[0] = jnp.where(lane < MLA_V, outs[0], pltpu.roll(outs[1], MLA_V, 1)).astype(o_ref.dtype)


def _mla(qm, km, vm, batch, seq, tq=1024, tk=2048, unroll=1):
    tq = min(seq, tq)
    tk = min(seq, tk)
    nk = seq // tk
    unroll = unroll if nk % unroll == 0 else 1
    q3 = qm.reshape(batch, seq, MLA_PAD)
    k3 = km.reshape(batch, seq, MLA_PAD)
    v3 = vm.reshape(batch, seq, MLA_PAD)
    resident = lambda b, h, i: (b, 0, h)
    o = pl.pallas_call(
        functools.partial(_mla_kernel, tk=tk, nk=nk, unroll=unroll),
        grid=(batch, MLA_HEADS // 2, seq // tq),
        in_specs=[pl.BlockSpec((1, tq, 2 * LANES), lambda b, h, i: (b, i, h)),
                  pl.BlockSpec((1, seq, 2 * LANES), resident, pipeline_mode=pl.Buffered(1)),
                  pl.BlockSpec((1, seq, 2 * LANES), resident, pipeline_mode=pl.Buffered(1))],
        out_specs=pl.BlockSpec((1, tq, LANES), lambda b, h, i: (b, i, h)),
        out_shape=jax.ShapeDtypeStruct((batch, seq, MLA_OUT), BF16),
        scratch_shapes=[pltpu.VMEM((2, tq, LANES), F32), pltpu.VMEM((2, tq, LANES), F32)],
        compiler_params=_cparams(("parallel", "parallel", "arbitrary")), name="mla",
    )(q3, k3, v3)
    return o.reshape(batch * seq, MLA_OUT)


def _outproj_kernel(x_ref, o0_ref, l0_ref, o1_ref, l1_ref, o2_ref, l2_ref, ob_ref, gate_ref,
                    wpa_ref, wpb_ref, wo_ref, g2_ref, wrh_ref, wrl_ref, br_ref,
                    xm_ref, h2_ref, route_ref, cnt_ref, cnt_sc):
    def halves(ref):
        return jnp.concatenate([ref[0, hf] for hf in range(ref.shape[1])], axis=1)

    l0, l1, l2 = halves(l0_ref), halves(l1_ref), halves(l2_ref)
    m = jnp.maximum(jnp.maximum(l0, l1), l2)
    w0, w1, w2 = jnp.exp(l0 - m), jnp.exp(l1 - m), jnp.exp(l2 - m)
    oa = (w0 * halves(o0_ref) + w1 * halves(o1_ref) + w2 * halves(o2_ref)) / (w0 + w1 + w2)
    pa = jnp.dot(oa.astype(BF16), wpa_ref[...], preferred_element_type=F32)
    pb = jnp.dot(ob_ref[...], wpb_ref[...], preferred_element_type=F32)
    merged = gate_ref[:, :D_MODEL] * pa + gate_ref[:, D_MODEL:] * pb
    xm = x_ref[...] + jnp.dot(merged.astype(BF16), wo_ref[...], preferred_element_type=F32)
    xm_ref[...] = xm
    h2 = _rms(xm, g2_ref[...])
    h2_ref[...] = h2

    hi = h2.astype(BF16)
    lo = (h2 - hi.astype(F32)).astype(BF16)
    lg = (jnp.dot(hi, wrh_ref[...], preferred_element_type=F32)
          + jnp.dot(lo, wrh_ref[...], preferred_element_type=F32)
          + jnp.dot(hi, wrl_ref[...], preferred_element_type=F32)) + br_ref[...]
    lane = lax.broadcasted_iota(jnp.int32, lg.shape, 1)
    lanef = lane.astype(F32)
    big = float(ROUTE_LANES)
    gmask = (lane >= N_EXPERTS) & (lane < N_EXPERTS + N_GROUPS)
    gl = jnp.where(gmask, lg, NEG)
    gmax = jnp.max(gl, axis=-1, keepdims=True)
    gidx = jnp.min(jnp.where(gl == gmax, lanef, big), axis=-1, keepdims=True) - float(N_EXPERTS)
    p_grp = 1.0 / jnp.sum(jnp.where(gmask, jnp.exp(gl - gmax), 0.0), axis=-1, keepdims=True)
    emask = (lane // EXPERTS_PER_GROUP).astype(F32) == gidx
    el = jnp.where(emask, lg, NEG)
    m1 = jnp.max(el, axis=-1, keepdims=True)
    i1 = jnp.min(jnp.where(el == m1, lanef, big), axis=-1, keepdims=True)
    el2 = jnp.where(lanef == i1, NEG, el)
    m2 = jnp.max(el2, axis=-1, keepdims=True)
    i2 = jnp.min(jnp.where(el2 == m2, lanef, big), axis=-1, keepdims=True)
    r = jnp.exp(m2 - m1)
    wa = p_grp / (1.0 + r)
    wb = p_grp * r / (1.0 + r)
    @pl.when(pl.program_id(0) == 0)
    def _():
        cnt_sc[...] = jnp.zeros(cnt_sc.shape, F32)

    oh1 = lanef == i1
    oh2 = lanef == i2
    oh = jnp.where(oh1 | oh2, 1.0, 0.0)
    tm = lg.shape[0]
    earlier = (lax.broadcasted_iota(jnp.int32, (tm, tm), 0) > lax.broadcasted_iota(jnp.int32, (tm, tm), 1))
    before = jnp.dot(jnp.where(earlier, 1.0, 0.0).astype(BF16), oh.astype(BF16),
                     preferred_element_type=F32) + cnt_sc[0:1, :]
    r1 = jnp.sum(jnp.where(oh1, before, 0.0), axis=-1, keepdims=True)
    r2 = jnp.sum(jnp.where(oh2, before, 0.0), axis=-1, keepdims=True)
    cnt = cnt_sc[...] + jnp.sum(oh, axis=0, keepdims=True)
    cnt_sc[...] = cnt
    cnt_ref[...] = cnt
    route_ref[...] = jnp.where(lane == 0, i1, jnp.where(lane == 1, i2, jnp.where(lane == 2, wa,
                               jnp.where(lane == 3, wb, jnp.where(lane == 4, r1,
                                                                  jnp.where(lane == 5, r2, 0.0))))))


def _outproj(x, dil_outs, ob, gates, lw, seq, tm):
    T = x.shape[0]
    nt = seq // tm
    row = lambda i: (i, 0)
    const = lambda i: (0, 0)
    weights = [lw['wpa'], lw['wpb'], lw['wo'], lw['g2'], lw['wrh'], lw['wrl'], lw['br']]
    acts = [x] + list(dil_outs) + [ob, gates]
    dil_spec = pl.BlockSpec((1, DIL_OUT // LANES, tm, LANES), lambda i: (i // nt, 0, i % nt, 0))
    in_specs = [pl.BlockSpec((tm, D_MODEL), row)] + [dil_spec] * len(dil_outs) \
        + [pl.BlockSpec((tm, ob.shape[1]), row), pl.BlockSpec((tm, gates.shape[1]), row)] \
        + [pl.BlockSpec(a.shape, const) for a in weights]
    return pl.pallas_call(
        _outproj_kernel, grid=(T // tm,), in_specs=in_specs,
        out_specs=[pl.BlockSpec((tm, D_MODEL), row), pl.BlockSpec((tm, D_MODEL), row),
                   pl.BlockSpec((tm, ROUTE_LANES), row), pl.BlockSpec((8, ROUTE_LANES), const)],
        out_shape=[jax.ShapeDtypeStruct((T, D_MODEL), F32), jax.ShapeDtypeStruct((T, D_MODEL), F32),
                   jax.ShapeDtypeStruct((T, ROUTE_LANES), F32), jax.ShapeDtypeStruct((8, ROUTE_LANES), F32)],
        scratch_shapes=[pltpu.VMEM((8, ROUTE_LANES), F32)],
        compiler_params=_cparams(("arbitrary",)), name="outproj_router",
    )(*acts, *weights)


GATHER_WINDOW = 32


def _gather_rows(src, idx):
    m = idx.shape[0]
    width = src.shape[1]
    info = plsc.get_sparse_core_info()
    n_workers = info.num_cores * info.num_subcores
    per_w = m // n_workers
    n_pairs = per_w // (2 * GATHER_WINDOW)
    assert n_pairs * 2 * GATHER_WINDOW * n_workers == m
    mesh = plsc.VectorSubcoreMesh(core_axis_name="core", subcore_axis_name="subcore")

    @functools.partial(
        pl.kernel, out_type=jax.ShapeDtypeStruct((m, width), src.dtype), mesh=mesh, name="gather_rows",
        scratch_types=[pltpu.VMEM((per_w,), jnp.int32),
                       pltpu.VMEM((2, GATHER_WINDOW, width), src.dtype),
                       pltpu.SemaphoreType.DMA((2,))])
    def gather(src_hbm, idx_hbm, out_hbm, idx_v, rows_v, sems):
        wid = lax.axis_index("subcore") * info.num_cores + lax.axis_index("core")
        base = wid * per_w
        pltpu.sync_copy(idx_hbm.at[pl.ds(base, per_w)], idx_v)

        def fetch(c, b):
            return pltpu.make_async_copy(src_hbm.at[idx_v.at[pl.ds(c * GATHER_WINDOW, GATHER_WINDOW)]],
                                         rows_v.at[b], sems.at[b])

        def flush(c, b):
            pltpu.sync_copy(rows_v.at[b], out_hbm.at[pl.ds(base + c * GATHER_WINDOW, GATHER_WINDOW)])

        fetch(0, 0).start()

        @pl.loop(0, n_pairs)
        def _(p):
            c = 2 * p
            fetch(c + 1, 1).start()
            fetch(c, 0).wait()
            flush(c, 0)

            @pl.when(p + 1 < n_pairs)
            def _():
                fetch(c + 2, 0).start()

            fetch(c + 1, 1).wait()
            flush(c + 1, 1)

    return gather(src, idx)


EXPERT_BLOCK = 512


def _expert_kernel(be_ref, bv_ref, bf_ref, xs_ref, wg_ref, wu_ref, wd_ref, y_ref, wg_sc, wu_sc, wd_sc):
    i = pl.program_id(0)

    @pl.when(bf_ref[i] > 0)
    def _():
        wg_sc[...] = wg_ref[0].astype(BF16)
        wu_sc[...] = wu_ref[0].astype(BF16)
        wd_sc[...] = wd_ref[0].astype(BF16)

    @pl.when(bv_ref[i] > 0)
    def _():
        xb = xs_ref[...].astype(BF16)
        g = jnp.dot(xb, wg_sc[...], preferred_element_type=F32)
        u = jnp.dot(xb, wu_sc[...], preferred_element_type=F32)
        hb = (g * jax.nn.sigmoid(g) * u).astype(BF16)
        y_ref[...] = jnp.dot(hb, wd_sc[...], preferred_element_type=F32)

    @pl.when(bv_ref[i] == 0)
    def _():
        y_ref[...] = jnp.zeros(y_ref.shape, F32)


def _experts(xs, blk_expert, blk_valid, blk_first, wg, wu, wd, layer):
    n_slots = xs.shape[0]
    nb = n_slots // EXPERT_BLOCK
    grid_spec = pltpu.PrefetchScalarGridSpec(
        num_scalar_prefetch=3, grid=(nb,),
        in_specs=[pl.BlockSpec((EXPERT_BLOCK, D_MODEL), lambda i, be, bv, bf: (i, 0)),
                  pl.BlockSpec((None, 1, D_MODEL, D_EXPERT), lambda i, be, bv, bf: (layer, be[i], 0, 0)),
                  pl.BlockSpec((None, 1, D_MODEL, D_EXPERT), lambda i, be, bv, bf: (layer, be[i], 0, 0)),
                  pl.BlockSpec((None, 1, D_EXPERT, D_MODEL), lambda i, be, bv, bf: (layer, be[i], 0, 0))],
        out_specs=pl.BlockSpec((EXPERT_BLOCK, D_MODEL), lambda i, be, bv, bf: (i, 0)),
        scratch_shapes=[pltpu.VMEM((D_MODEL, D_EXPERT), BF16), pltpu.VMEM((D_MODEL, D_EXPERT), BF16),
                        pltpu.VMEM((D_EXPERT, D_MODEL), BF16)])
    return pl.pallas_call(
        _expert_kernel, grid_spec=grid_spec,
        out_shape=jax.ShapeDtypeStruct((n_slots, D_MODEL), F32),
        compiler_params=_cparams(("arbitrary",)), name="experts",
    )(blk_expert, blk_valid, blk_first, xs, wg, wu, wd)


def _combine_kernel(x_ref, ya_ref, yb_ref, route_ref, g_ref, o_ref, *, final):
    route = route_ref[...]
    wa = route[:, 2:3]
    wb = route[:, 3:4]
    x = x_ref[...] + (ya_ref[...] * wa + yb_ref[...] * wb)
    if final:
        x = _rms(x, g_ref[...])
    o_ref[...] = x


def _combine(x, yg, route, g, final, tm):
    T = x.shape[0]
    nt = T // tm
    return pl.pallas_call(
        functools.partial(_combine_kernel, final=final), grid=(nt,),
        in_specs=[pl.BlockSpec((tm, D_MODEL), lambda i: (i, 0)),
                  pl.BlockSpec((tm, D_MODEL), lambda i: (i, 0)),
                  pl.BlockSpec((tm, D_MODEL), lambda i: (i + nt, 0)),
                  pl.BlockSpec((tm, ROUTE_LANES), lambda i: (i, 0)),
                  pl.BlockSpec((1, D_MODEL), lambda i: (0, 0))],
        out_specs=pl.BlockSpec((tm, D_MODEL), lambda i: (i, 0)),
        out_shape=jax.ShapeDtypeStruct((T, D_MODEL), F32),
        compiler_params=_cparams(("parallel",)), name="combine",
    )(x, yg, yg, route, g)


def _dispatch_plan(route, cnt, T):
    A = T * TOP_K
    expert = route[:, 0:TOP_K].astype(jnp.int32)
    rank = route[:, 4:4 + TOP_K].astype(jnp.int32)
    counts = cnt[0, :N_EXPERTS].astype(jnp.int32)
    padded = (counts + EXPERT_BLOCK - 1) // EXPERT_BLOCK * EXPERT_BLOCK
    pad_end = jnp.cumsum(padded)
    pad_start = pad_end - padded
    slot = pad_start[expert] + rank
    n_blocks = A // EXPERT_BLOCK + N_EXPERTS
    n_slots = n_blocks * EXPERT_BLOCK
    tok = jnp.broadcast_to(jnp.arange(T, dtype=jnp.int32)[:, None], (T, TOP_K))
    filler = jnp.arange(n_slots, dtype=jnp.int32) % T
    slot_tok = filler.at[slot.reshape(A)].set(tok.reshape(A), unique_indices=True)
    blk_start = jnp.arange(n_blocks, dtype=jnp.int32) * EXPERT_BLOCK
    blk_expert = jnp.minimum(jnp.sum((blk_start[:, None] >= pad_end[None, :]).astype(jnp.int32), axis=1),
                             N_EXPERTS - 1)
    blk_valid = (blk_start < pad_end[-1]).astype(jnp.int32)
    blk_first = jnp.concatenate([jnp.ones((1,), jnp.int32),
                                 (blk_expert[1:] != blk_expert[:-1]).astype(jnp.int32)])
    comb_idx = jnp.concatenate([slot[:, 0], slot[:, 1]])
    return slot_tok, blk_expert, blk_valid, blk_first, comb_idx


def _rope_tables(seq, dim, period, first):
    half = dim // 2
    inv = ROPE_THETA ** (-jnp.arange(0, dim, 2, dtype=F32) / dim)
    ang = jnp.arange(seq, dtype=F32)[:, None] * inv[None, :]
    cos, sin = jnp.cos(ang), jnp.sin(ang)
    d = jnp.arange(LANES) % period - first
    in_a = (d >= 0) & (d < half)
    in_b = (d >= half) & (d < dim)
    idx = jnp.clip(jnp.where(in_b, d - half, d), 0, half - 1)
    c = jnp.where((in_a | in_b)[None, :], cos[:, idx], 1.0)
    sa = jnp.where(in_a[None, :], -sin[:, idx], 0.0)
    sb = jnp.where(in_b[None, :], sin[:, idx], 0.0)
    return c, sa, sb


def _pad_heads(w, width):
    k = w.shape[0]
    w = w.reshape(k, MLA_HEADS, width)
    return jnp.pad(w, ((0, 0), (0, 0), (0, MLA_SLOT - width))).reshape(k, MLA_PAD)


def _prep_layer(l, norm1_g, w_in, q_norm_g, w_uq, kv_norm_g, w_uk, w_uv, w_pa, w_pb, w_o,
                norm2_g, w_rg, b_rg, w_re, b_re, w_e_gate, w_e_up, w_e_down):
    w = w_in[l]
    o = 0
    parts = []
    for n in (DIL_WIDTH, DIL_WIDTH, DIL_WIDTH, Q_LORA, KV_LORA, MLA_ROPE, 2 * D_MODEL):
        parts.append(w[:, o:o + n])
        o += n
    wq, wk, wv, wcq, wckv, wkr, wg = parts
    wkr_pad = jnp.pad(wkr, ((0, 0), (MLA_NOPE, MLA_SLOT - MLA_NOPE - MLA_ROPE)))
    wr = jnp.pad(jnp.concatenate([w_re[l], w_rg[l]], axis=1),
                 ((0, 0), (0, ROUTE_LANES - N_EXPERTS - N_GROUPS)))
    wrh = wr.astype(BF16)
    wrl = (wr - wrh.astype(F32)).astype(BF16)
    br = jnp.pad(jnp.concatenate([b_re[l], b_rg[l]]), (0, ROUTE_LANES - N_EXPERTS - N_GROUPS))
    return dict(
        g1=norm1_g[l][None, :], wq=wq.astype(BF16), wk=wk.astype(BF16), wv=wv.astype(BF16),
        wcq=wcq.astype(BF16), wckv=wckv.astype(BF16), wkr=wkr_pad.astype(BF16), wg=wg.astype(BF16),
        qn=q_norm_g[l][None, :], kvn=kv_norm_g[l][None, :],
        wuq=_pad_heads(w_uq[l], MLA_QK).astype(BF16), wuk=_pad_heads(w_uk[l], MLA_NOPE).astype(BF16),
        wuv=_pad_heads(w_uv[l], MLA_V).astype(BF16),
        wpa=w_pa[l].astype(BF16), wpb=w_pb[l].astype(BF16), wo=w_o[l].astype(BF16),
        g2=norm2_g[l][None, :], wrh=wrh, wrl=wrl, br=br[None, :].astype(F32),
        weg=w_e_gate, weu=w_e_up, wed=w_e_down)


def _trunk(x3, layers, final_g, tm=256):
    batch, seq, _ = x3.shape
    T = batch * seq
    x = x3.reshape(T, D_MODEL)
    tabs = _rope_tables(seq, ROT_DIM, HEAD_DIM, 0) + _rope_tables(seq, MLA_ROPE, MLA_SLOT, MLA_NOPE)
    fg = final_g[None, :]
    for l, lw in enumerate(layers):
        outs = _inproj(x, lw, tabs, seq, tm)
        n_g = len(DIL_GROUPS)
        qd, kd, vd = outs[0:n_g], outs[n_g:2 * n_g], outs[2 * n_g:3 * n_g]
        qm, km, vm, gates = outs[3 * n_g:]
        dil = []
        for g, (_, d) in enumerate(DIL_GROUPS):
            dil.extend(_dilated_group(qd[g], kd[g], vd[g], batch, seq, g, d))
        ob = _mla(qm, km, vm, batch, seq)
        xm, h2, route, cnt = _outproj(x, dil, ob, gates, lw, seq, tm)
        slot_tok, blk_expert, blk_valid, blk_first, comb_idx = _dispatch_plan(route, cnt, T)
        xs = _gather_rows(h2, slot_tok)
        ys = _experts(xs, blk_expert, blk_valid, blk_first, lw['weg'], lw['weu'], lw['wed'], l)
        yg = _gather_rows(ys, comb_idx)
        x = _combine(xm, yg, route, fg, l == len(layers) - 1, tm)
    return x.reshape(batch, seq, D_MODEL)


def kernel(x_prompt, x_sample, norm1_g, w_in, q_norm_g, w_uq, kv_norm_g, w_uk, w_uv, w_pa, w_pb, w_o,
           norm2_g, w_rg, b_rg, w_re, b_re, w_e_gate, w_e_up, w_e_down, final_g):
    layers = [_prep_layer(l, norm1_g, w_in, q_norm_g, w_uq, kv_norm_g, w_uk, w_uv, w_pa, w_pb, w_o,
                          norm2_g, w_rg, b_rg, w_re, b_re, w_e_gate, w_e_up, w_e_down)
              for l in range(DEPTH)]
    return (_trunk(x_prompt, layers, final_g), _trunk(x_sample, layers, final_g))
```

```python
import functools

import jax
import jax.numpy as jnp
from jax import lax
from jax.experimental import pallas as pl
from jax.experimental.pallas import tpu as pltpu
from jax.experimental.pallas import tpu_sc as plsc

D_MODEL = 1024
DEPTH = 2
HEAD_DIM = 64
DIL_GROUPS = ((128, 1), (512, 4), (2048, 16))
DIL_HEADS_PER_GROUP = 4
DIL_WIDTH = 768
DIL_OUT = 256
DIL_SCALE = HEAD_DIM ** -0.5
ROT_DIM = 16
ROPE_THETA = 500000.0
DIL_HALF = 64

MLA_HEADS = 8
MLA_NOPE = 64
MLA_ROPE = 32
MLA_QK = 96
MLA_V = 64
MLA_OUT = 512
MLA_SCALE = MLA_QK ** -0.5
Q_LORA = 256
KV_LORA = 128

N_GROUPS = 8
EXPERTS_PER_GROUP = 8
N_EXPERTS = 64
TOP_K = 2
D_EXPERT = 512
EPS = 1e-6

LANES = 128
MLA_SLOT = LANES
MLA_PAD = MLA_HEADS * MLA_SLOT
ROUTE_LANES = LANES
VMEM_LIMIT = 56 * 1024 * 1024

BF16 = jnp.bfloat16
F32 = jnp.float32
NEG = -1e30
LOG2E = 1.4426950408889634


def _cparams(sem):
    return pltpu.CompilerParams(dimension_semantics=sem, vmem_limit_bytes=VMEM_LIMIT)


def _rms(t, g):
    return t * lax.rsqrt(jnp.mean(t * t, axis=-1, keepdims=True) + EPS) * g


def _rope_chunk(t, c, sa, sb, shift):
    return t * c + pltpu.roll(t, LANES - shift, 1) * sa + pltpu.roll(t, shift, 1) * sb


def _inproj_kernel(x_ref, g1_ref, wq_ref, wk_ref, wv_ref, wcq_ref, wckv_ref, wkr_ref, wg_ref,
                   qn_ref, kvn_ref, wuq_ref, wuk_ref, wuv_ref,
                   ca_ref, saa_ref, sab_ref, cb_ref, sba_ref, sbb_ref,
                   q0_ref, q1_ref, q2_ref, k0_ref, k1_ref, k2_ref, v0_ref, v1_ref, v2_ref,
                   qm_ref, km_ref, vm_ref, gate_ref, q_sc, k_sc, v_sc):
    x = x_ref[...]
    h = _rms(x, g1_ref[...]).astype(BF16)
    tm = x.shape[0]

    ca, saa, sab = ca_ref[...], saa_ref[...], sab_ref[...]
    q = jnp.dot(h, wq_ref[...], preferred_element_type=F32)
    k = jnp.dot(h, wk_ref[...], preferred_element_type=F32)
    v = jnp.dot(h, wv_ref[...], preferred_element_type=F32)
    for j in range(DIL_WIDTH // LANES):
        sl = slice(j * LANES, (j + 1) * LANES)
        q_sc[j] = _rope_chunk(q[:, sl], ca, saa, sab, ROT_DIM // 2) * DIL_SCALE
        k_sc[j] = _rope_chunk(k[:, sl], ca, saa, sab, ROT_DIM // 2)
        v_sc[j] = v[:, sl]
    halves = DIL_OUT // LANES
    for src, outs in ((q_sc, (q0_ref, q1_ref, q2_ref)), (k_sc, (k0_ref, k1_ref, k2_ref)),
                      (v_sc, (v0_ref, v1_ref, v2_ref))):
        for g, (_, dil) in enumerate(DIL_GROUPS):
            for hf in range(halves):
                for r in range(dil):
                    rows = pl.ds(r, tm // dil, stride=dil) if dil > 1 else slice(None)
                    outs[g][0, r, :, hf * LANES:(hf + 1) * LANES] = src[g * halves + hf, rows, :].astype(BF16)

    cb, sba, sbb = cb_ref[...], sba_ref[...], sbb_ref[...]
    cq = jnp.dot(h, wcq_ref[...], preferred_element_type=F32)
    cqn = _rms(cq, qn_ref[...]).astype(BF16)
    qm = jnp.dot(cqn, wuq_ref[...], preferred_element_type=F32) * (MLA_SCALE * LOG2E)
    ckv = jnp.dot(h, wckv_ref[...], preferred_element_type=F32)
    c = _rms(ckv, kvn_ref[...]).astype(BF16)
    kn = jnp.dot(c, wuk_ref[...], preferred_element_type=F32)
    vv = jnp.dot(c, wuv_ref[...], preferred_element_type=F32)
    kr = jnp.dot(h, wkr_ref[...], preferred_element_type=F32)
    kr = _rope_chunk(kr, cb, sba, sbb, MLA_ROPE // 2)
    lane = lax.broadcasted_iota(jnp.int32, (1, LANES), 1)
    ones_col = jnp.where(lane == MLA_V, 1.0, 0.0).astype(F32)
    for j in range(MLA_HEADS):
        sl = slice(j * LANES, (j + 1) * LANES)
        qm_ref[:, sl] = _rope_chunk(qm[:, sl], cb, sba, sbb, MLA_ROPE // 2).astype(BF16)
        km_ref[:, sl] = (kn[:, sl] + kr).astype(BF16)
        vm_ref[:, sl] = (vv[:, sl] + ones_col).astype(BF16)

    gate_ref[...] = jax.nn.sigmoid(jnp.dot(h, wg_ref[...], preferred_element_type=F32))


def _inproj(x, lw, tabs, seq, tm):
    T = x.shape[0]
    nt = seq // tm
    row = lambda i: (i, 0)
    const = lambda i: (0, 0)
    tab = lambda i: (i % nt, 0)

    def wspec(a):
        return pl.BlockSpec(a.shape, const)

    weights = [lw['g1'], lw['wq'], lw['wk'], lw['wv'], lw['wcq'], lw['wckv'], lw['wkr'], lw['wg'],
               lw['qn'], lw['kvn'], lw['wuq'], lw['wuk'], lw['wuv']]
    in_specs = ([pl.BlockSpec((tm, D_MODEL), row)] + [wspec(a) for a in weights]
                + [pl.BlockSpec((tm, LANES), tab)] * 6)
    batch = T // seq
    res_map = lambda i: (i // nt, 0, i % nt, 0)
    dil_shape = [jax.ShapeDtypeStruct((batch, d, seq // d, DIL_OUT), BF16) for _, d in DIL_GROUPS] * 3
    dil_specs = [pl.BlockSpec((1, d, tm // d, DIL_OUT), res_map) for _, d in DIL_GROUPS] * 3
    out_shape = dil_shape + [jax.ShapeDtypeStruct((T, MLA_PAD), BF16)] * 3 \
        + [jax.ShapeDtypeStruct((T, 2 * D_MODEL), F32)]
    out_specs = dil_specs + [pl.BlockSpec((tm, MLA_PAD), row)] * 3 + [pl.BlockSpec((tm, 2 * D_MODEL), row)]
    return pl.pallas_call(
        _inproj_kernel, grid=(T // tm,), in_specs=in_specs, out_specs=out_specs, out_shape=out_shape,
        scratch_shapes=[pltpu.VMEM((DIL_WIDTH // LANES, tm, LANES), F32)] * 3,
        compiler_params=_cparams(("parallel",)), name="inproj",
    )(x, *weights, *tabs)


DIL_STEP_TOKENS = 2048


def _dil_kernel(q_ref, kp_ref, kc_ref, kn_ref, vp_ref, vc_ref, vn_ref, o_ref, lse_ref, *, tq, dil, sub_len):
    i = pl.program_id(1)
    qb = LANES
    nh = DIL_HEADS_PER_GROUP
    head = lax.broadcasted_iota(jnp.int32, (1, DIL_OUT), 1) // HEAD_DIM
    qloc = lax.broadcasted_iota(jnp.int32, (nh * qb, 2 * qb), 0) % qb
    rel = lax.broadcasted_iota(jnp.int32, (nh * qb, 2 * qb), 1) - DIL_HALF - qloc
    band = jnp.abs(rel) <= DIL_HALF
    kcol = lax.broadcasted_iota(jnp.int32, (1, 2 * qb), 1) - DIL_HALF

    def window(p_ref, c_ref, n_ref, r, j):
        lo, hi = j * qb - DIL_HALF, j * qb + 2 * qb - DIL_HALF
        parts = []
        if lo < 0:
            parts.append(p_ref[0, r, qb + lo:qb, :])
            lo = 0
        parts.append(c_ref[0, r, lo:min(hi, tq), :])
        if hi > tq:
            parts.append(n_ref[0, r, 0:hi - tq, :])
        return jnp.concatenate(parts, axis=0) if len(parts) > 1 else parts[0]

    for r in range(dil):
        for j in range(tq // qb):
            q = q_ref[0, r, j * qb:(j + 1) * qb, :]
            k = window(kp_ref, kc_ref, kn_ref, r, j)
            v = window(vp_ref, vc_ref, vn_ref, r, j)
            qs = jnp.concatenate([jnp.where(head == hd, q, jnp.zeros_like(q)) for hd in range(nh)], axis=0)
            s = lax.dot_general(qs, k, (((1,), (1,)), ((), ())), preferred_element_type=F32)
            kpos = kcol + (i * tq + j * qb)
            s = jnp.where(band & (kpos >= 0) & (kpos < sub_len), s, NEG)
            m = jnp.max(s, axis=-1, keepdims=True)
            p = jnp.exp(s - m)
            den = jnp.sum(p, axis=-1, keepdims=True)
            oh = jnp.dot(p.astype(BF16), v, preferred_element_type=F32) * (1.0 / den)
            lse = m + jnp.log(den)
            o_acc = oh[0:qb]
            l_acc = jnp.broadcast_to(lse[0:qb], (qb, DIL_OUT))
            for hd in range(1, nh):
                o_acc = jnp.where(head == hd, oh[hd * qb:(hd + 1) * qb], o_acc)
                l_acc = jnp.where(head == hd, lse[hd * qb:(hd + 1) * qb], l_acc)
            rows = pl.ds(j * qb * dil + r, qb, stride=dil) if dil > 1 else slice(j * qb, (j + 1) * qb)
            for hf in range(DIL_OUT // LANES):
                o_ref[0, hf, rows, :] = o_acc[:, hf * LANES:(hf + 1) * LANES]
                lse_ref[0, hf, rows, :] = l_acc[:, hf * LANES:(hf + 1) * LANES]


def _dilated_group(qd, kd, vd, batch, seq, g, dil):
    L = seq // dil
    tq = min(L, DIL_STEP_TOKENS // dil)
    nq = L // tq
    hb = tq // LANES
    nhb = L // LANES
    cur = lambda b, i: (b, 0, i, 0)
    prev = lambda b, i: (b, 0, jnp.maximum(i * hb - 1, 0), 0)
    nxt = lambda b, i: (b, 0, jnp.minimum((i + 1) * hb, nhb - 1), 0)
    blk = (1, dil, tq, DIL_OUT)
    halo = (1, dil, LANES, DIL_OUT)
    halves = DIL_OUT // LANES
    out_blk = (1, halves, tq * dil, LANES)
    out_map = lambda b, i: (b, 0, i, 0)
    o, lse = pl.pallas_call(
        functools.partial(_dil_kernel, tq=tq, dil=dil, sub_len=L),
        grid=(batch, nq),
        in_specs=[pl.BlockSpec(blk, cur),
                  pl.BlockSpec(halo, prev), pl.BlockSpec(blk, cur), pl.BlockSpec(halo, nxt),
                  pl.BlockSpec(halo, prev), pl.BlockSpec(blk, cur), pl.BlockSpec(halo, nxt)],
        out_specs=[pl.BlockSpec(out_blk, out_map), pl.BlockSpec(out_blk, out_map)],
        out_shape=[jax.ShapeDtypeStruct((batch, halves, seq, LANES), F32)] * 2,
        compiler_params=_cparams(("parallel", "parallel")), name=f"dilated_g{g}",
    )(qd, kd, kd, kd, vd, vd, vd)
    return o, lse


def _mla_kernel(q_ref, k_ref, v_ref, o_ref, m_sc, acc_sc, *, tk, nk, unroll):
    nc = tk // LANES
    m_sc[...] = jnp.full(m_sc.shape, NEG, F32)
    acc_sc[...] = jnp.zeros(acc_sc.shape, F32)

    def step(off, hh):
        sl = slice(hh * LANES, (hh + 1) * LANES)
        q = q_ref[0, :, sl]
        k = k_ref[0, pl.ds(off, tk), sl]
        v = v_ref[0, pl.ds(off, tk), sl]
        s = lax.dot_general(q, k, (((1,), (1,)), ((), ())), preferred_element_type=F32)
        cols = [s[:, c * LANES:(c + 1) * LANES] for c in range(nc)]
        m_old = m_sc[hh]
        m_new = jnp.maximum(m_old, jnp.max(functools.reduce(jnp.maximum, cols), axis=-1, keepdims=True))
        p = jnp.concatenate([jnp.exp2(c - m_new) for c in cols], axis=1).astype(BF16)
        acc_sc[hh] = acc_sc[hh] * jnp.exp2(m_old - m_new) + jnp.dot(p, v, preferred_element_type=F32)
        m_sc[hh] = m_new

    def body(j, carry):
        for u in range(unroll):
            off = pl.multiple_of((j * unroll + u) * tk, tk)
            for hh in range(2):
                step(off, hh)
        return carry

    lax.fori_loop(0, nk // unroll, body, 0)
    outs = []
    for hh in range(2):
        acc = acc_sc[hh]
        outs.append(acc * (1.0 / acc[:, MLA_V:MLA_V + 1]))
    lane = lax.broadcasted_iota(jnp.int32, (1, LANES), 1)
    o_ref[0] = jnp.where(lane < MLA_V, outs[0], pltpu.roll(outs[1], MLA_V, 1)).astype(o_ref.dtype)


def _mla(qm, km, vm, batch, seq, tq=1024, tk=2048, unroll=1):
    tq = min(seq, tq)
    tk = min(seq, tk)
    nk = seq // tk
    unroll = unroll if nk % unroll == 0 else 1
    q3 = qm.reshape(batch, seq, MLA_PAD)
    k3 = km.reshape(batch, seq, MLA_PAD)
    v3 = vm.reshape(batch, seq, MLA_PAD)
    resident = lambda b, h, i: (b, 0, h)
    o = pl.pallas_call(
        functools.partial(_mla_kernel, tk=tk, nk=nk, unroll=unroll),
        grid=(batch, MLA_HEADS // 2, seq // tq),
        in_specs=[pl.BlockSpec((1, tq, 2 * LANES), lambda b, h, i: (b, i, h)),
                  pl.BlockSpec((1, seq, 2 * LANES), resident, pipeline_mode=pl.Buffered(1)),
                  pl.BlockSpec((1, seq, 2 * LANES), resident, pipeline_mode=pl.Buffered(1))],
        out_specs=pl.BlockSpec((1, tq, LANES), lambda b, h, i: (b, i, h)),
        out_shape=jax.ShapeDtypeStruct((batch, seq, MLA_OUT), BF16),
        scratch_shapes=[pltpu.VMEM((2, tq, LANES), F32), pltpu.VMEM((2, tq, LANES), F32)],
        compiler_params=_cparams(("parallel", "parallel", "arbitrary")), name="mla",
    )(q3, k3, v3)
    return o.reshape(batch * seq, MLA_OUT)


def _outproj_kernel(x_ref, o0_ref, l0_ref, o1_ref, l1_ref, o2_ref, l2_ref, ob_ref, gate_ref,
                    wpa_ref, wpb_ref, wo_ref, g2_ref, wrh_ref, wrl_ref, br_ref,
                    xm_ref, h2_ref, route_ref, route_t_ref, cnt_ref, cnt_sc):
    def halves(ref):
        return jnp.concatenate([ref[0, hf] for hf in range(ref.shape[1])], axis=1)

    l0, l1, l2 = halves(l0_ref), halves(l1_ref), halves(l2_ref)
    m = jnp.maximum(jnp.maximum(l0, l1), l2)
    w0, w1, w2 = jnp.exp(l0 - m), jnp.exp(l1 - m), jnp.exp(l2 - m)
    oa = (w0 * halves(o0_ref) + w1 * halves(o1_ref) + w2 * halves(o2_ref)) / (w0 + w1 + w2)
    pa = jnp.dot(oa.astype(BF16), wpa_ref[...], preferred_element_type=F32)
    pb = jnp.dot(ob_ref[...], wpb_ref[...], preferred_element_type=F32)
    merged = gate_ref[:, :D_MODEL] * pa + gate_ref[:, D_MODEL:] * pb
    xm = x_ref[...] + jnp.dot(merged.astype(BF16), wo_ref[...], preferred_element_type=F32)
    xm_ref[...] = xm
    h2 = _rms(xm, g2_ref[...])
    h2_ref[...] = h2

    hi = h2.astype(BF16)
    lo = (h2 - hi.astype(F32)).astype(BF16)
    lg = (jnp.dot(hi, wrh_ref[...], preferred_element_type=F32)
          + jnp.dot(lo, wrh_ref[...], preferred_element_type=F32)
          + jnp.dot(hi, wrl_ref[...], preferred_element_type=F32)) + br_ref[...]
    lane = lax.broadcasted_iota(jnp.int32, lg.shape, 1)
    lanef = lane.astype(F32)
    big = float(ROUTE_LANES)
    gmask = (lane >= N_EXPERTS) & (lane < N_EXPERTS + N_GROUPS)
    gl = jnp.where(gmask, lg, NEG)
    gmax = jnp.max(gl, axis=-1, keepdims=True)
    gidx = jnp.min(jnp.where(gl == gmax, lanef, big), axis=-1, keepdims=True) - float(N_EXPERTS)
    p_grp = 1.0 / jnp.sum(jnp.where(gmask, jnp.exp(gl - gmax), 0.0), axis=-1, keepdims=True)
    emask = (lane // EXPERTS_PER_GROUP).astype(F32) == gidx
    el = jnp.where(emask, lg, NEG)
    m1 = jnp.max(el, axis=-1, keepdims=True)
    i1 = jnp.min(jnp.where(el == m1, lanef, big), axis=-1, keepdims=True)
    el2 = jnp.where(lanef == i1, NEG, el)
    m2 = jnp.max(el2, axis=-1, keepdims=True)
    i2 = jnp.min(jnp.where(el2 == m2, lanef, big), axis=-1, keepdims=True)
    r = jnp.exp(m2 - m1)
    wa = p_grp / (1.0 + r)
    wb = p_grp * r / (1.0 + r)
    @pl.when(pl.program_id(0) == 0)
    def _():
        cnt_sc[...] = jnp.zeros(cnt_sc.shape, F32)

    oh1 = lanef == i1
    oh2 = lanef == i2
    oh = jnp.where(oh1 | oh2, 1.0, 0.0)
    tm = lg.shape[0]
    earlier = (lax.broadcasted_iota(jnp.int32, (tm, tm), 0) > lax.broadcasted_iota(jnp.int32, (tm, tm), 1))
    before = jnp.dot(jnp.where(earlier, 1.0, 0.0).astype(BF16), oh.astype(BF16),
                     preferred_element_type=F32) + cnt_sc[0:1, :]
    r1 = jnp.sum(jnp.where(oh1, before, 0.0), axis=-1, keepdims=True)
    r2 = jnp.sum(jnp.where(oh2, before, 0.0), axis=-1, keepdims=True)
    cnt = cnt_sc[...] + jnp.sum(oh, axis=0, keepdims=True)
    cnt_sc[...] = cnt
    cnt_ref[...] = cnt
    route = jnp.where(lane == 0, i1, jnp.where(lane == 1, i2, jnp.where(lane == 2, wa,
                      jnp.where(lane == 3, wb, jnp.where(lane == 4, r1, jnp.where(lane == 5, r2, 0.0))))))
    route_ref[...] = route
    route_t_ref[...] = jnp.transpose(route)[0:8, :]


def _outproj(x, dil_outs, ob, gates, lw, seq, tm):
    T = x.shape[0]
    nt = seq // tm
    row = lambda i: (i, 0)
    const = lambda i: (0, 0)
    weights = [lw['wpa'], lw['wpb'], lw['wo'], lw['g2'], lw['wrh'], lw['wrl'], lw['br']]
    acts = [x] + list(dil_outs) + [ob, gates]
    dil_spec = pl.BlockSpec((1, DIL_OUT // LANES, tm, LANES), lambda i: (i // nt, 0, i % nt, 0))
    in_specs = [pl.BlockSpec((tm, D_MODEL), row)] + [dil_spec] * len(dil_outs) \
        + [pl.BlockSpec((tm, ob.shape[1]), row), pl.BlockSpec((tm, gates.shape[1]), row)] \
        + [pl.BlockSpec(a.shape, const) for a in weights]
    return pl.pallas_call(
        _outproj_kernel, grid=(T // tm,), in_specs=in_specs,
        out_specs=[pl.BlockSpec((tm, D_MODEL), row), pl.BlockSpec((tm, D_MODEL), row),
                   pl.BlockSpec((tm, ROUTE_LANES), row), pl.BlockSpec((8, tm), lambda i: (0, i)),
                   pl.BlockSpec((8, ROUTE_LANES), const)],
        out_shape=[jax.ShapeDtypeStruct((T, D_MODEL), F32), jax.ShapeDtypeStruct((T, D_MODEL), F32),
                   jax.ShapeDtypeStruct((T, ROUTE_LANES), F32), jax.ShapeDtypeStruct((8, T), F32),
                   jax.ShapeDtypeStruct((8, ROUTE_LANES), F32)],
        scratch_shapes=[pltpu.VMEM((8, ROUTE_LANES), F32)],
        compiler_params=_cparams(("arbitrary",)), name="outproj_router",
    )(*acts, *weights)


GATHER_WINDOW = 32


def _gather_rows(src, idx):
    m = idx.shape[0]
    width = src.shape[1]
    info = plsc.get_sparse_core_info()
    n_workers = info.num_cores * info.num_subcores
    per_w = m // n_workers
    n_pairs = per_w // (2 * GATHER_WINDOW)
    assert n_pairs * 2 * GATHER_WINDOW * n_workers == m
    mesh = plsc.VectorSubcoreMesh(core_axis_name="core", subcore_axis_name="subcore")

    @functools.partial(
        pl.kernel, out_type=jax.ShapeDtypeStruct((m, width), src.dtype), mesh=mesh, name="gather_rows",
        scratch_types=[pltpu.VMEM((per_w,), jnp.int32),
                       pltpu.VMEM((2, GATHER_WINDOW, width), src.dtype),
                       pltpu.SemaphoreType.DMA((2,))])
    def gather(src_hbm, idx_hbm, out_hbm, idx_v, rows_v, sems):
        wid = lax.axis_index("subcore") * info.num_cores + lax.axis_index("core")
        base = wid * per_w
        pltpu.sync_copy(idx_hbm.at[pl.ds(base, per_w)], idx_v)

        def fetch(c, b):
            return pltpu.make_async_copy(src_hbm.at[idx_v.at[pl.ds(c * GATHER_WINDOW, GATHER_WINDOW)]],
                                         rows_v.at[b], sems.at[b])

        def flush(c, b):
            pltpu.sync_copy(rows_v.at[b], out_hbm.at[pl.ds(base + c * GATHER_WINDOW, GATHER_WINDOW)])

        fetch(0, 0).start()

        @pl.loop(0, n_pairs)
        def _(p):
            c = 2 * p
            fetch(c + 1, 1).start()
            fetch(c, 0).wait()
            flush(c, 0)

            @pl.when(p + 1 < n_pairs)
            def _():
                fetch(c + 2, 0).start()

            fetch(c + 1, 1).wait()
            flush(c + 1, 1)

    return gather(src, idx)


EXPERT_BLOCK = 512


def _expert_kernel(be_ref, bv_ref, bf_ref, bio_ref, xs_ref, wg_ref, wu_ref, wd_ref, y_ref,
                   wg_sc, wu_sc, wd_sc):
    i = pl.program_id(0)

    @pl.when(bf_ref[i] > 0)
    def _():
        wg_sc[...] = wg_ref[0].astype(BF16)
        wu_sc[...] = wu_ref[0].astype(BF16)
        wd_sc[...] = wd_ref[0].astype(BF16)

    @pl.when(bv_ref[i] > 0)
    def _():
        xb = xs_ref[...].astype(BF16)
        g = jnp.dot(xb, wg_sc[...], preferred_element_type=F32)
        u = jnp.dot(xb, wu_sc[...], preferred_element_type=F32)
        hb = (g * jax.nn.sigmoid(g) * u).astype(BF16)
        y_ref[...] = jnp.dot(hb, wd_sc[...], preferred_element_type=F32)


def _experts(xs, blk_expert, blk_valid, blk_first, blk_io, wg, wu, wd, layer):
    n_slots = xs.shape[0]
    nb = n_slots // EXPERT_BLOCK
    grid_spec = pltpu.PrefetchScalarGridSpec(
        num_scalar_prefetch=4, grid=(nb,),
        in_specs=[pl.BlockSpec((EXPERT_BLOCK, D_MODEL), lambda i, be, bv, bf, bio: (bio[i], 0)),
                  pl.BlockSpec((None, 1, D_MODEL, D_EXPERT), lambda i, be, bv, bf, bio: (layer, be[i], 0, 0)),
                  pl.BlockSpec((None, 1, D_MODEL, D_EXPERT), lambda i, be, bv, bf, bio: (layer, be[i], 0, 0)),
                  pl.BlockSpec((None, 1, D_EXPERT, D_MODEL), lambda i, be, bv, bf, bio: (layer, be[i], 0, 0))],
        out_specs=pl.BlockSpec((EXPERT_BLOCK, D_MODEL), lambda i, be, bv, bf, bio: (bio[i], 0)),
        scratch_shapes=[pltpu.VMEM((D_MODEL, D_EXPERT), BF16), pltpu.VMEM((D_MODEL, D_EXPERT), BF16),
                        pltpu.VMEM((D_EXPERT, D_MODEL), BF16)])
    return pl.pallas_call(
        _expert_kernel, grid_spec=grid_spec,
        out_shape=jax.ShapeDtypeStruct((n_slots, D_MODEL), F32),
        compiler_params=_cparams(("arbitrary",)), name="experts",
    )(blk_expert, blk_valid, blk_first, blk_io, xs, wg, wu, wd)


def _combine_kernel(x_ref, ya_ref, yb_ref, route_ref, g_ref, o_ref, *, final):
    route = route_ref[...]
    wa = route[:, 2:3]
    wb = route[:, 3:4]
    x = x_ref[...] + (ya_ref[...] * wa + yb_ref[...] * wb)
    if final:
        x = _rms(x, g_ref[...])
    o_ref[...] = x


def _combine(x, yg, route, g, final, tm):
    T = x.shape[0]
    nt = T // tm
    return pl.pallas_call(
        functools.partial(_combine_kernel, final=final), grid=(nt,),
        in_specs=[pl.BlockSpec((tm, D_MODEL), lambda i: (i, 0)),
                  pl.BlockSpec((tm, D_MODEL), lambda i: (i, 0)),
                  pl.BlockSpec((tm, D_MODEL), lambda i: (i + nt, 0)),
                  pl.BlockSpec((tm, ROUTE_LANES), lambda i: (i, 0)),
                  pl.BlockSpec((1, D_MODEL), lambda i: (0, 0))],
        out_specs=pl.BlockSpec((tm, D_MODEL), lambda i: (i, 0)),
        out_shape=jax.ShapeDtypeStruct((T, D_MODEL), F32),
        compiler_params=_cparams(("parallel",)), name="combine",
    )(x, yg, yg, route, g)


def _dispatch_plan(route_t, cnt, T):
    A = T * TOP_K
    expert = route_t[0:TOP_K].astype(jnp.int32)
    rank = route_t[4:4 + TOP_K].astype(jnp.int32)
    counts = cnt[0, :N_EXPERTS].astype(jnp.int32)
    padded = (counts + EXPERT_BLOCK - 1) // EXPERT_BLOCK * EXPERT_BLOCK
    pad_end = jnp.cumsum(padded)
    pad_start = pad_end - padded
    slot = pad_start[expert] + rank
    n_blocks = A // EXPERT_BLOCK + N_EXPERTS
    n_slots = n_blocks * EXPERT_BLOCK
    tok = jnp.broadcast_to(jnp.arange(T, dtype=jnp.int32)[None, :], (TOP_K, T))
    blk_start = jnp.arange(n_blocks, dtype=jnp.int32) * EXPERT_BLOCK
    blk_valid = (blk_start < pad_end[-1]).astype(jnp.int32)
    blk_io = jnp.minimum(jnp.arange(n_blocks, dtype=jnp.int32), jnp.sum(blk_valid) - 1)
    blk_expert = jnp.minimum(jnp.sum((blk_start[:, None] >= pad_end[None, :]).astype(jnp.int32), axis=1),
                             N_EXPERTS - 1)[blk_io]
    blk_rows = jnp.clip(pad_start[blk_expert] + counts[blk_expert] - blk_start, 0, EXPERT_BLOCK)
    blk_first = jnp.concatenate([jnp.ones((1,), jnp.int32),
                                 (blk_expert[1:] != blk_expert[:-1]).astype(jnp.int32)])
    scattered = jnp.zeros((n_slots,), jnp.int32).at[slot.reshape(A)].max(tok.reshape(A))
    filler = jnp.arange(n_slots, dtype=jnp.int32) % T
    used = (jnp.arange(EXPERT_BLOCK, dtype=jnp.int32)[None, :] < blk_rows[:, None]).reshape(n_slots)
    slot_tok = jnp.where(used, scattered, filler)
    comb_idx = slot.reshape(A)
    return slot_tok, blk_expert, blk_valid, blk_first, blk_io, comb_idx


def _rope_tables(seq, dim, period, first):
    half = dim // 2
    inv = ROPE_THETA ** (-jnp.arange(0, dim, 2, dtype=F32) / dim)
    ang = jnp.arange(seq, dtype=F32)[:, None] * inv[None, :]
    cos, sin = jnp.cos(ang), jnp.sin(ang)
    d = jnp.arange(LANES) % period - first
    in_a = (d >= 0) & (d < half)
    in_b = (d >= half) & (d < dim)
    idx = jnp.clip(jnp.where(in_b, d - half, d), 0, half - 1)
    c = jnp.where((in_a | in_b)[None, :], cos[:, idx], 1.0)
    sa = jnp.where(in_a[None, :], -sin[:, idx], 0.0)
    sb = jnp.where(in_b[None, :], sin[:, idx], 0.0)
    return c, sa, sb


def _pad_heads(w, width):
    k = w.shape[0]
    w = w.reshape(k, MLA_HEADS, width)
    return jnp.pad(w, ((0, 0), (0, 0), (0, MLA_SLOT - width))).reshape(k, MLA_PAD)


def _prep_layer(l, norm1_g, w_in, q_norm_g, w_uq, kv_norm_g, w_uk, w_uv, w_pa, w_pb, w_o,
                norm2_g, w_rg, b_rg, w_re, b_re, w_e_gate, w_e_up, w_e_down):
    w = w_in[l]
    o = 0
    parts = []
    for n in (DIL_WIDTH, DIL_WIDTH, DIL_WIDTH, Q_LORA, KV_LORA, MLA_ROPE, 2 * D_MODEL):
        parts.append(w[:, o:o + n])
        o += n
    wq, wk, wv, wcq, wckv, wkr, wg = parts
    wkr_pad = jnp.pad(wkr, ((0, 0), (MLA_NOPE, MLA_SLOT - MLA_NOPE - MLA_ROPE)))
    wr = jnp.pad(jnp.concatenate([w_re[l], w_rg[l]], axis=1),
                 ((0, 0), (0, ROUTE_LANES - N_EXPERTS - N_GROUPS)))
    wrh = wr.astype(BF16)
    wrl = (wr - wrh.astype(F32)).astype(BF16)
    br = jnp.pad(jnp.concatenate([b_re[l], b_rg[l]]), (0, ROUTE_LANES - N_EXPERTS - N_GROUPS))
    return dict(
        g1=norm1_g[l][None, :], wq=wq.astype(BF16), wk=wk.astype(BF16), wv=wv.astype(BF16),
        wcq=wcq.astype(BF16), wckv=wckv.astype(BF16), wkr=wkr_pad.astype(BF16), wg=wg.astype(BF16),
        qn=q_norm_g[l][None, :], kvn=kv_norm_g[l][None, :],
        wuq=_pad_heads(w_uq[l], MLA_QK).astype(BF16), wuk=_pad_heads(w_uk[l], MLA_NOPE).astype(BF16),
        wuv=_pad_heads(w_uv[l], MLA_V).astype(BF16),
        wpa=w_pa[l].astype(BF16), wpb=w_pb[l].astype(BF16), wo=w_o[l].astype(BF16),
        g2=norm2_g[l][None, :], wrh=wrh, wrl=wrl, br=br[None, :].astype(F32),
        weg=w_e_gate, weu=w_e_up, wed=w_e_down)


def _trunk(x3, layers, final_g, tm=256):
    batch, seq, _ = x3.shape
    T = batch * seq
    x = x3.reshape(T, D_MODEL)
    tabs = _rope_tables(seq, ROT_DIM, HEAD_DIM, 0) + _rope_tables(seq, MLA_ROPE, MLA_SLOT, MLA_NOPE)
    fg = final_g[None, :]
    for l, lw in enumerate(layers):
        outs = _inproj(x, lw, tabs, seq, tm)
        n_g = len(DIL_GROUPS)
        qd, kd, vd = outs[0:n_g], outs[n_g:2 * n_g], outs[2 * n_g:3 * n_g]
        qm, km, vm, gates = outs[3 * n_g:]
        dil = []
        for g, (_, d) in enumerate(DIL_GROUPS):
            dil.extend(_dilated_group(qd[g], kd[g], vd[g], batch, seq, g, d))
        ob = _mla(qm, km, vm, batch, seq)
        xm, h2, route, route_t, cnt = _outproj(x, dil, ob, gates, lw, seq, tm)
        slot_tok, blk_expert, blk_valid, blk_first, blk_io, comb_idx = _dispatch_plan(route_t, cnt, T)
        xs = _gather_rows(h2, slot_tok)
        ys = _experts(xs, blk_expert, blk_valid, blk_first, blk_io, lw['weg'], lw['weu'], lw['wed'], l)
        yg = _gather_rows(ys, comb_idx)
        x = _combine(xm, yg, route, fg, l == len(layers) - 1, tm)
    return x.reshape(batch, seq, D_MODEL)


def kernel(x_prompt, x_sample, norm1_g, w_in, q_norm_g, w_uq, kv_norm_g, w_uk, w_uv, w_pa, w_pb, w_o,
           norm2_g, w_rg, b_rg, w_re, b_re, w_e_gate, w_e_up, w_e_down, final_g):
    layers = [_prep_layer(l, norm1_g, w_in, q_norm_g, w_uq, kv_norm_g, w_uk, w_uv, w_pa, w_pb, w_o,
                          norm2_g, w_rg, b_rg, w_re, b_re, w_e_gate, w_e_up, w_e_down)
              for l in range(DEPTH)]
    return (_trunk(x_prompt, layers, final_g), _trunk(x_sample, layers, final_g))
```

```python
import functools

import jax
import jax.numpy as jnp
from jax import lax
from jax.experimental import pallas as pl
from jax.experimental.pallas import tpu as pltpu
from jax.experimental.pallas import tpu_sc as plsc

D_MODEL = 1024
DEPTH = 2
HEAD_DIM = 64
DIL_GROUPS = ((128, 1), (512, 4), (2048, 16))
DIL_HEADS_PER_GROUP = 4
DIL_WIDTH = 768
DIL_OUT = 256
DIL_SCALE = HEAD_DIM ** -0.5
ROT_DIM = 16
ROPE_THETA = 500000.0
DIL_HALF = 64

MLA_HEADS = 8
MLA_NOPE = 64
MLA_ROPE = 32
MLA_QK = 96
MLA_V = 64
MLA_OUT = 512
MLA_SCALE = MLA_QK ** -0.5
Q_LORA = 256
KV_LORA = 128

N_GROUPS = 8
EXPERTS_PER_GROUP = 8
N_EXPERTS = 64
TOP_K = 2
D_EXPERT = 512
EPS = 1e-6

LANES = 128
MLA_SLOT = LANES
MLA_PAD = MLA_HEADS * MLA_SLOT
ROUTE_LANES = LANES
VMEM_LIMIT = 56 * 1024 * 1024

BF16 = jnp.bfloat16
F32 = jnp.float32
NEG = -1e30
LOG2E = 1.4426950408889634


def _cparams(sem):
    return pltpu.CompilerParams(dimension_semantics=sem, vmem_limit_bytes=VMEM_LIMIT)


def _rms(t, g):
    return t * lax.rsqrt(jnp.mean(t * t, axis=-1, keepdims=True) + EPS) * g


def _rope_chunk(t, c, sa, sb, shift):
    return t * c + pltpu.roll(t, LANES - shift, 1) * sa + pltpu.roll(t, shift, 1) * sb


def _inproj_kernel(x_ref, g1_ref, wq_ref, wk_ref, wv_ref, wcq_ref, wckv_ref, wkr_ref, wg_ref,
                   qn_ref, kvn_ref, wuq_ref, wuk_ref, wuv_ref,
                   ca_ref, saa_ref, sab_ref, cb_ref, sba_ref, sbb_ref,
                   q0_ref, q1_ref, q2_ref, k0_ref, k1_ref, k2_ref, v0_ref, v1_ref, v2_ref,
                   qm_ref, km_ref, vm_ref, gate_ref, q_sc, k_sc, v_sc):
    x = x_ref[...]
    h = _rms(x, g1_ref[...]).astype(BF16)
    tm = x.shape[0]

    ca, saa, sab = ca_ref[...], saa_ref[...], sab_ref[...]
    q = jnp.dot(h, wq_ref[...], preferred_element_type=F32)
    k = jnp.dot(h, wk_ref[...], preferred_element_type=F32)
    v = jnp.dot(h, wv_ref[...], preferred_element_type=F32)
    for j in range(DIL_WIDTH // LANES):
        sl = slice(j * LANES, (j + 1) * LANES)
        q_sc[j] = _rope_chunk(q[:, sl], ca, saa, sab, ROT_DIM // 2) * DIL_SCALE
        k_sc[j] = _rope_chunk(k[:, sl], ca, saa, sab, ROT_DIM // 2)
        v_sc[j] = v[:, sl]
    halves = DIL_OUT // LANES
    for src, outs in ((q_sc, (q0_ref, q1_ref, q2_ref)), (k_sc, (k0_ref, k1_ref, k2_ref)),
                      (v_sc, (v0_ref, v1_ref, v2_ref))):
        for g, (_, dil) in enumerate(DIL_GROUPS):
            for hf in range(halves):
                for r in range(dil):
                    rows = pl.ds(r, tm // dil, stride=dil) if dil > 1 else slice(None)
                    outs[g][0, r, :, hf * LANES:(hf + 1) * LANES] = src[g * halves + hf, rows, :].astype(BF16)

    cb, sba, sbb = cb_ref[...], sba_ref[...], sbb_ref[...]
    cq = jnp.dot(h, wcq_ref[...], preferred_element_type=F32)
    cqn = _rms(cq, qn_ref[...]).astype(BF16)
    qm = jnp.dot(cqn, wuq_ref[...], preferred_element_type=F32) * (MLA_SCALE * LOG2E)
    ckv = jnp.dot(h, wckv_ref[...], preferred_element_type=F32)
    c = _rms(ckv, kvn_ref[...]).astype(BF16)
    kn = jnp.dot(c, wuk_ref[...], preferred_element_type=F32)
    vv = jnp.dot(c, wuv_ref[...], preferred_element_type=F32)
    kr = jnp.dot(h, wkr_ref[...], preferred_element_type=F32)
    kr = _rope_chunk(kr, cb, sba, sbb, MLA_ROPE // 2)
    lane = lax.broadcasted_iota(jnp.int32, (1, LANES), 1)
    ones_col = jnp.where(lane == MLA_V, 1.0, 0.0).astype(F32)
    for j in range(MLA_HEADS):
        sl = slice(j * LANES, (j + 1) * LANES)
        qm_ref[:, sl] = _rope_chunk(qm[:, sl], cb, sba, sbb, MLA_ROPE // 2).astype(BF16)
        km_ref[:, sl] = (kn[:, sl] + kr).astype(BF16)
        vm_ref[:, sl] = (vv[:, sl] + ones_col).astype(BF16)

    gate_ref[...] = jax.nn.sigmoid(jnp.dot(h, wg_ref[...], preferred_element_type=F32))


def _inproj(x, lw, tabs, seq, tm):
    T = x.shape[0]
    nt = seq // tm
    row = lambda i: (i, 0)
    const = lambda i: (0, 0)
    tab = lambda i: (i % nt, 0)

    def wspec(a):
        return pl.BlockSpec(a.shape, const, pipeline_mode=pl.Buffered(1))

    weights = [lw['g1'], lw['wq'], lw['wk'], lw['wv'], lw['wcq'], lw['wckv'], lw['wkr'], lw['wg'],
               lw['qn'], lw['kvn'], lw['wuq'], lw['wuk'], lw['wuv']]
    in_specs = ([pl.BlockSpec((tm, D_MODEL), row)] + [wspec(a) for a in weights]
                + [pl.BlockSpec((tm, LANES), tab)] * 6)
    batch = T // seq
    res_map = lambda i: (i // nt, 0, i % nt, 0)
    dil_shape = [jax.ShapeDtypeStruct((batch, d, seq // d, DIL_OUT), BF16) for _, d in DIL_GROUPS] * 3
    dil_specs = [pl.BlockSpec((1, d, tm // d, DIL_OUT), res_map) for _, d in DIL_GROUPS] * 3
    out_shape = dil_shape + [jax.ShapeDtypeStruct((T, MLA_PAD), BF16)] * 3 \
        + [jax.ShapeDtypeStruct((T, 2 * D_MODEL), F32)]
    out_specs = dil_specs + [pl.BlockSpec((tm, MLA_PAD), row)] * 3 + [pl.BlockSpec((tm, 2 * D_MODEL), row)]
    return pl.pallas_call(
        _inproj_kernel, grid=(T // tm,), in_specs=in_specs, out_specs=out_specs, out_shape=out_shape,
        scratch_shapes=[pltpu.VMEM((DIL_WIDTH // LANES, tm, LANES), F32)] * 3,
        compiler_params=_cparams(("parallel",)), name="inproj",
    )(x, *weights, *tabs)


DIL_STEP_TOKENS = 2048


def _dil_kernel(q_ref, kp_ref, kc_ref, kn_ref, vp_ref, vc_ref, vn_ref, o_ref, lse_ref, *, tq, dil, sub_len):
    i = pl.program_id(1)
    qb = LANES
    nh = DIL_HEADS_PER_GROUP
    head = lax.broadcasted_iota(jnp.int32, (1, DIL_OUT), 1) // HEAD_DIM
    qloc = lax.broadcasted_iota(jnp.int32, (nh * qb, 2 * qb), 0) % qb
    rel = lax.broadcasted_iota(jnp.int32, (nh * qb, 2 * qb), 1) - DIL_HALF - qloc
    band = jnp.abs(rel) <= DIL_HALF
    kcol = lax.broadcasted_iota(jnp.int32, (1, 2 * qb), 1) - DIL_HALF

    def window(p_ref, c_ref, n_ref, r, j):
        lo, hi = j * qb - DIL_HALF, j * qb + 2 * qb - DIL_HALF
        parts = []
        if lo < 0:
            parts.append(p_ref[0, r, qb + lo:qb, :])
            lo = 0
        parts.append(c_ref[0, r, lo:min(hi, tq), :])
        if hi > tq:
            parts.append(n_ref[0, r, 0:hi - tq, :])
        return jnp.concatenate(parts, axis=0) if len(parts) > 1 else parts[0]

    for r in range(dil):
        for j in range(tq // qb):
            q = q_ref[0, r, j * qb:(j + 1) * qb, :]
            k = window(kp_ref, kc_ref, kn_ref, r, j)
            v = window(vp_ref, vc_ref, vn_ref, r, j)
            qs = jnp.concatenate([jnp.where(head == hd, q, jnp.zeros_like(q)) for hd in range(nh)], axis=0)
            s = lax.dot_general(qs, k, (((1,), (1,)), ((), ())), preferred_element_type=F32)
            kpos = kcol + (i * tq + j * qb)
            s = jnp.where(band & (kpos >= 0) & (kpos < sub_len), s, NEG)
            m = jnp.max(s, axis=-1, keepdims=True)
            p = jnp.exp(s - m)
            den = jnp.sum(p, axis=-1, keepdims=True)
            oh = jnp.dot(p.astype(BF16), v, preferred_element_type=F32) * (1.0 / den)
            lse = m + jnp.log(den)
            o_acc = oh[0:qb]
            l_acc = jnp.broadcast_to(lse[0:qb], (qb, DIL_OUT))
            for hd in range(1, nh):
                o_acc = jnp.where(head == hd, oh[hd * qb:(hd + 1) * qb], o_acc)
                l_acc = jnp.where(head == hd, lse[hd * qb:(hd + 1) * qb], l_acc)
            rows = pl.ds(j * qb * dil + r, qb, stride=dil) if dil > 1 else slice(j * qb, (j + 1) * qb)
            for hf in range(DIL_OUT // LANES):
                o_ref[0, hf, rows, :] = o_acc[:, hf * LANES:(hf + 1) * LANES]
                lse_ref[0, hf, rows, :] = l_acc[:, hf * LANES:(hf + 1) * LANES]


def _dilated_group(qd, kd, vd, batch, seq, g, dil):
    L = seq // dil
    tq = min(L, DIL_STEP_TOKENS // dil)
    nq = L // tq
    hb = tq // LANES
    nhb = L // LANES
    cur = lambda b, i: (b, 0, i, 0)
    prev = lambda b, i: (b, 0, jnp.maximum(i * hb - 1, 0), 0)
    nxt = lambda b, i: (b, 0, jnp.minimum((i + 1) * hb, nhb - 1), 0)
    blk = (1, dil, tq, DIL_OUT)
    halo = (1, dil, LANES, DIL_OUT)
    halves = DIL_OUT // LANES
    out_blk = (1, halves, tq * dil, LANES)
    out_map = lambda b, i: (b, 0, i, 0)
    o, lse = pl.pallas_call(
        functools.partial(_dil_kernel, tq=tq, dil=dil, sub_len=L),
        grid=(batch, nq),
        in_specs=[pl.BlockSpec(blk, cur),
                  pl.BlockSpec(halo, prev), pl.BlockSpec(blk, cur), pl.BlockSpec(halo, nxt),
                  pl.BlockSpec(halo, prev), pl.BlockSpec(blk, cur), pl.BlockSpec(halo, nxt)],
        out_specs=[pl.BlockSpec(out_blk, out_map), pl.BlockSpec(out_blk, out_map)],
        out_shape=[jax.ShapeDtypeStruct((batch, halves, seq, LANES), F32)] * 2,
        compiler_params=_cparams(("parallel", "parallel")), name=f"dilated_g{g}",
    )(qd, kd, kd, kd, vd, vd, vd)
    return o, lse


def _mla_kernel(q_ref, k_ref, v_ref, o_ref, m_sc, acc_sc, *, tk, nk, unroll):
    nc = tk // LANES
    m_sc[...] = jnp.full(m_sc.shape, NEG, F32)
    acc_sc[...] = jnp.zeros(acc_sc.shape, F32)

    def step(off, hh):
        sl = slice(hh * LANES, (hh + 1) * LANES)
        q = q_ref[0, :, sl]
        k = k_ref[0, pl.ds(off, tk), sl]
        v = v_ref[0, pl.ds(off, tk), sl]
        s = lax.dot_general(q, k, (((1,), (1,)), ((), ())), preferred_element_type=F32)
        cols = [s[:, c * LANES:(c + 1) * LANES] for c in range(nc)]
        m_old = m_sc[hh]
        m_new = jnp.maximum(m_old, jnp.max(functools.reduce(jnp.maximum, cols), axis=-1, keepdims=True))
        p = jnp.concatenate([jnp.exp2(c - m_new) for c in cols], axis=1).astype(BF16)
        acc_sc[hh] = acc_sc[hh] * jnp.exp2(m_old - m_new) + jnp.dot(p, v, preferred_element_type=F32)
        m_sc[hh] = m_new

    def body(j, carry):
        for u in range(unroll):
            off = pl.multiple_of((j * unroll + u) * tk, tk)
            for hh in range(2):
                step(off, hh)
        return carry

    lax.fori_loop(0, nk // unroll, body, 0)
    outs = []
    for hh in range(2):
        acc = acc_sc[hh]
        outs.append(acc * (1.0 / acc[:, MLA_V:MLA_V + 1]))
    lane = lax.broadcasted_iota(jnp.int32, (1, LANES), 1)
    o_ref[0] = jnp.where(lane < MLA_V, outs[0], pltpu.roll(outs[1], MLA_V, 1)).astype(o_ref.dtype)


def _mla(qm, km, vm, batch, seq, tq=1024, tk=2048, unroll=1):
    tq = min(seq, tq)
    tk = min(seq, tk)
    nk = seq // tk
    unroll = unroll if nk % unroll == 0 else 1
    q3 = qm.reshape(batch, seq, MLA_PAD)
    k3 = km.reshape(batch, seq, MLA_PAD)
    v3 = vm.reshape(batch, seq, MLA_PAD)
    resident = lambda b, h, i: (b, 0, h)
    o = pl.pallas_call(
        functools.partial(_mla_kernel, tk=tk, nk=nk, unroll=unroll),
        grid=(batch, MLA_HEADS // 2, seq // tq),
        in_specs=[pl.BlockSpec((1, tq, 2 * LANES), lambda b, h, i: (b, i, h)),
                  pl.BlockSpec((1, seq, 2 * LANES), resident, pipeline_mode=pl.Buffered(1)),
                  pl.BlockSpec((1, seq, 2 * LANES), resident, pipeline_mode=pl.Buffered(1))],
        out_specs=pl.BlockSpec((1, tq, LANES), lambda b, h, i: (b, i, h)),
        out_shape=jax.ShapeDtypeStruct((batch, seq, MLA_OUT), BF16),
        scratch_shapes=[pltpu.VMEM((2, tq, LANES), F32), pltpu.VMEM((2, tq, LANES), F32)],
        compiler_params=_cparams(("parallel", "parallel", "arbitrary")), name="mla",
    )(q3, k3, v3)
    return o.reshape(batch * seq, MLA_OUT)


def _outproj_kernel(x_ref, o0_ref, l0_ref, o1_ref, l1_ref, o2_ref, l2_ref, ob_ref, gate_ref,
                    wpa_ref, wpb_ref, wo_ref, g2_ref, wrh_ref, wrl_ref, br_ref,
                    xm_ref, h2_ref, route_ref, cnt_ref, cnt_sc):
    def halves(ref):
        return jnp.concatenate([ref[0, hf] for hf in range(ref.shape[1])], axis=1)

    l0, l1, l2 = halves(l0_ref), halves(l1_ref), halves(l2_ref)
    m = jnp.maximum(jnp.maximum(l0, l1), l2)
    w0, w1, w2 = jnp.exp(l0 - m), jnp.exp(l1 - m), jnp.exp(l2 - m)
    oa = (w0 * halves(o0_ref) + w1 * halves(o1_ref) + w2 * halves(o2_ref)) / (w0 + w1 + w2)
    pa = jnp.dot(oa.astype(BF16), wpa_ref[...], preferred_element_type=F32)
    pb = jnp.dot(ob_ref[...], wpb_ref[...], preferred_element_type=F32)
    merged = gate_ref[:, :D_MODEL] * pa + gate_ref[:, D_MODEL:] * pb
    xm = x_ref[...] + jnp.dot(merged.astype(BF16), wo_ref[...], preferred_element_type=F32)
    xm_ref[...] = xm
    h2 = _rms(xm, g2_ref[...])
    h2_ref[...] = h2

    hi = h2.astype(BF16)
    lo = (h2 - hi.astype(F32)).astype(BF16)
    lg = (jnp.dot(hi, wrh_ref[...], preferred_element_type=F32)
          + jnp.dot(lo, wrh_ref[...], preferred_element_type=F32)
          + jnp.dot(hi, wrl_ref[...], preferred_element_type=F32)) + br_ref[...]
    lane = lax.broadcasted_iota(jnp.int32, lg.shape, 1)
    lanef = lane.astype(F32)
    big = float(ROUTE_LANES)
    gmask = (lane >= N_EXPERTS) & (lane < N_EXPERTS + N_GROUPS)
    gl = jnp.where(gmask, lg, NEG)
    gmax = jnp.max(gl, axis=-1, keepdims=True)
    gidx = jnp.min(jnp.where(gl == gmax, lanef, big), axis=-1, keepdims=True) - float(N_EXPERTS)
    p_grp = 1.0 / jnp.sum(jnp.where(gmask, jnp.exp(gl - gmax), 0.0), axis=-1, keepdims=True)
    emask = (lane // EXPERTS_PER_GROUP).astype(F32) == gidx
    el = jnp.where(emask, lg, NEG)
    m1 = jnp.max(el, axis=-1, keepdims=True)
    i1 = jnp.min(jnp.where(el == m1, lanef, big), axis=-1, keepdims=True)
    el2 = jnp.where(lanef == i1, NEG, el)
    m2 = jnp.max(el2, axis=-1, keepdims=True)
    i2 = jnp.min(jnp.where(el2 == m2, lanef, big), axis=-1, keepdims=True)
    r = jnp.exp(m2 - m1)
    wa = p_grp / (1.0 + r)
    wb = p_grp * r / (1.0 + r)
    @pl.when(pl.program_id(0) == 0)
    def _():
        cnt_sc[...] = jnp.zeros(cnt_sc.shape, F32)

    oh1 = lanef == i1
    oh2 = lanef == i2
    oh = jnp.where(oh1 | oh2, 1.0, 0.0)
    tm = lg.shape[0]
    earlier = (lax.broadcasted_iota(jnp.int32, (tm, tm), 0) > lax.broadcasted_iota(jnp.int32, (tm, tm), 1))
    before = jnp.dot(jnp.where(earlier, 1.0, 0.0).astype(BF16), oh.astype(BF16),
                     preferred_element_type=F32) + cnt_sc[0:1, :]
    r1 = jnp.sum(jnp.where(oh1, before, 0.0), axis=-1, keepdims=True)
    r2 = jnp.sum(jnp.where(oh2, before, 0.0), axis=-1, keepdims=True)
    cnt = cnt_sc[...] + jnp.sum(oh, axis=0, keepdims=True)
    cnt_sc[...] = cnt
    cnt_ref[...] = cnt
    route_ref[...] = jnp.where(lane == 0, i1, jnp.where(lane == 1, i2, jnp.where(lane == 2, wa,
                               jnp.where(lane == 3, wb, jnp.where(lane == 4, r1,
                                                                  jnp.where(lane == 5, r2, 0.0))))))


def _outproj(x, dil_outs, ob, gates, lw, seq, tm):
    T = x.shape[0]
    nt = seq // tm
    row = lambda i: (i, 0)
    const = lambda i: (0, 0)
    weights = [lw['wpa'], lw['wpb'], lw['wo'], lw['g2'], lw['wrh'], lw['wrl'], lw['br']]
    acts = [x] + list(dil_outs) + [ob, gates]
    dil_spec = pl.BlockSpec((1, DIL_OUT // LANES, tm, LANES), lambda i: (i // nt, 0, i % nt, 0))
    in_specs = [pl.BlockSpec((tm, D_MODEL), row)] + [dil_spec] * len(dil_outs) \
        + [pl.BlockSpec((tm, ob.shape[1]), row), pl.BlockSpec((tm, gates.shape[1]), row)] \
        + [pl.BlockSpec(a.shape, const, pipeline_mode=pl.Buffered(1)) for a in weights]
    return pl.pallas_call(
        _outproj_kernel, grid=(T // tm,), in_specs=in_specs,
        out_specs=[pl.BlockSpec((tm, D_MODEL), row), pl.BlockSpec((tm, D_MODEL), row),
                   pl.BlockSpec((tm, ROUTE_LANES), row), pl.BlockSpec((8, ROUTE_LANES), const)],
        out_shape=[jax.ShapeDtypeStruct((T, D_MODEL), F32), jax.ShapeDtypeStruct((T, D_MODEL), F32),
                   jax.ShapeDtypeStruct((T, ROUTE_LANES), F32), jax.ShapeDtypeStruct((8, ROUTE_LANES), F32)],
        scratch_shapes=[pltpu.VMEM((8, ROUTE_LANES), F32)],
        compiler_params=_cparams(("arbitrary",)), name="outproj_router",
    )(*acts, *weights)


GATHER_WINDOW = 32


def _gather_rows(src, idx):
    m = idx.shape[0]
    width = src.shape[1]
    info = plsc.get_sparse_core_info()
    n_workers = info.num_cores * info.num_subcores
    per_w = m // n_workers
    n_pairs = per_w // (2 * GATHER_WINDOW)
    assert n_pairs * 2 * GATHER_WINDOW * n_workers == m
    mesh = plsc.VectorSubcoreMesh(core_axis_name="core", subcore_axis_name="subcore")

    @functools.partial(
        pl.kernel, out_type=jax.ShapeDtypeStruct((m, width), src.dtype), mesh=mesh, name="gather_rows",
        scratch_types=[pltpu.VMEM((per_w,), jnp.int32),
                       pltpu.VMEM((2, GATHER_WINDOW, width), src.dtype),
                       pltpu.SemaphoreType.DMA((2,))])
    def gather(src_hbm, idx_hbm, out_hbm, idx_v, rows_v, sems):
        wid = lax.axis_index("subcore") * info.num_cores + lax.axis_index("core")
        base = wid * per_w
        pltpu.sync_copy(idx_hbm.at[pl.ds(base, per_w)], idx_v)

        def fetch(c, b):
            return pltpu.make_async_copy(src_hbm.at[idx_v.at[pl.ds(c * GATHER_WINDOW, GATHER_WINDOW)]],
                                         rows_v.at[b], sems.at[b])

        def flush(c, b):
            pltpu.sync_copy(rows_v.at[b], out_hbm.at[pl.ds(base + c * GATHER_WINDOW, GATHER_WINDOW)])

        fetch(0, 0).start()

        @pl.loop(0, n_pairs)
        def _(p):
            c = 2 * p
            fetch(c + 1, 1).start()
            fetch(c, 0).wait()
            flush(c, 0)

            @pl.when(p + 1 < n_pairs)
            def _():
                fetch(c + 2, 0).start()

            fetch(c + 1, 1).wait()
            flush(c + 1, 1)

    return gather(src, idx)


EXPERT_BLOCK = 512


def _expert_kernel(be_ref, bv_ref, bf_ref, bio_ref, xs_ref, wg_ref, wu_ref, wd_ref, y_ref,
                   wg_sc, wu_sc, wd_sc):
    i = pl.program_id(0)

    @pl.when(bf_ref[i] > 0)
    def _():
        wg_sc[...] = wg_ref[0].astype(BF16)
        wu_sc[...] = wu_ref[0].astype(BF16)
        wd_sc[...] = wd_ref[0].astype(BF16)

    @pl.when(bv_ref[i] > 0)
    def _():
        xb = xs_ref[...].astype(BF16)
        g = jnp.dot(xb, wg_sc[...], preferred_element_type=F32)
        u = jnp.dot(xb, wu_sc[...], preferred_element_type=F32)
        hb = (g * jax.nn.sigmoid(g) * u).astype(BF16)
        y_ref[...] = jnp.dot(hb, wd_sc[...], preferred_element_type=F32)


def _experts(xs, blk_expert, blk_valid, blk_first, blk_io, wg, wu, wd, layer):
    n_slots = xs.shape[0]
    nb = n_slots // EXPERT_BLOCK
    grid_spec = pltpu.PrefetchScalarGridSpec(
        num_scalar_prefetch=4, grid=(nb,),
        in_specs=[pl.BlockSpec((EXPERT_BLOCK, D_MODEL), lambda i, be, bv, bf, bio: (bio[i], 0)),
                  pl.BlockSpec((None, 1, D_MODEL, D_EXPERT), lambda i, be, bv, bf, bio: (layer, be[i], 0, 0)),
                  pl.BlockSpec((None, 1, D_MODEL, D_EXPERT), lambda i, be, bv, bf, bio: (layer, be[i], 0, 0)),
                  pl.BlockSpec((None, 1, D_EXPERT, D_MODEL), lambda i, be, bv, bf, bio: (layer, be[i], 0, 0))],
        out_specs=pl.BlockSpec((EXPERT_BLOCK, D_MODEL), lambda i, be, bv, bf, bio: (bio[i], 0)),
        scratch_shapes=[pltpu.VMEM((D_MODEL, D_EXPERT), BF16), pltpu.VMEM((D_MODEL, D_EXPERT), BF16),
                        pltpu.VMEM((D_EXPERT, D_MODEL), BF16)])
    return pl.pallas_call(
        _expert_kernel, grid_spec=grid_spec,
        out_shape=jax.ShapeDtypeStruct((n_slots, D_MODEL), F32),
        compiler_params=_cparams(("arbitrary",)), name="experts",
    )(blk_expert, blk_valid, blk_first, blk_io, xs, wg, wu, wd)


def _combine_kernel(x_ref, ya_ref, yb_ref, route_ref, g_ref, o_ref, *, final):
    route = route_ref[...]
    wa = route[:, 2:3]
    wb = route[:, 3:4]
    x = x_ref[...] + (ya_ref[...] * wa + yb_ref[...] * wb)
    if final:
        x = _rms(x, g_ref[...])
    o_ref[...] = x


def _combine(x, yg, route, g, final, tm):
    T = x.shape[0]
    nt = T // tm
    return pl.pallas_call(
        functools.partial(_combine_kernel, final=final), grid=(nt,),
        in_specs=[pl.BlockSpec((tm, D_MODEL), lambda i: (i, 0)),
                  pl.BlockSpec((tm, D_MODEL), lambda i: (i, 0)),
                  pl.BlockSpec((tm, D_MODEL), lambda i: (i + nt, 0)),
                  pl.BlockSpec((tm, ROUTE_LANES), lambda i: (i, 0)),
                  pl.BlockSpec((1, D_MODEL), lambda i: (0, 0))],
        out_specs=pl.BlockSpec((tm, D_MODEL), lambda i: (i, 0)),
        out_shape=jax.ShapeDtypeStruct((T, D_MODEL), F32),
        compiler_params=_cparams(("parallel",)), name="combine",
    )(x, yg, yg, route, g)


def _dispatch_plan(route, cnt, T):
    A = T * TOP_K
    expert = route[:, 0:TOP_K].astype(jnp.int32)
    rank = route[:, 4:4 + TOP_K].astype(jnp.int32)
    counts = cnt[0, :N_EXPERTS].astype(jnp.int32)
    padded = (counts + EXPERT_BLOCK - 1) // EXPERT_BLOCK * EXPERT_BLOCK
    pad_end = jnp.cumsum(padded)
    pad_start = pad_end - padded
    slot = pad_start[expert] + rank
    n_blocks = A // EXPERT_BLOCK + N_EXPERTS
    n_slots = n_blocks * EXPERT_BLOCK
    tok = jnp.broadcast_to(jnp.arange(T, dtype=jnp.int32)[:, None], (T, TOP_K))
    filler = jnp.arange(n_slots, dtype=jnp.int32) % T
    slot_tok = filler.at[slot.reshape(A)].set(tok.reshape(A), unique_indices=True)
    blk_start = jnp.arange(n_blocks, dtype=jnp.int32) * EXPERT_BLOCK
    blk_valid = (blk_start < pad_end[-1]).astype(jnp.int32)
    blk_io = jnp.minimum(jnp.arange(n_blocks, dtype=jnp.int32), jnp.sum(blk_valid) - 1)
    blk_expert = jnp.minimum(jnp.sum((blk_start[:, None] >= pad_end[None, :]).astype(jnp.int32), axis=1),
                             N_EXPERTS - 1)[blk_io]
    blk_first = jnp.concatenate([jnp.ones((1,), jnp.int32),
                                 (blk_expert[1:] != blk_expert[:-1]).astype(jnp.int32)])
    comb_idx = jnp.concatenate([slot[:, 0], slot[:, 1]])
    return slot_tok, blk_expert, blk_valid, blk_first, blk_io, comb_idx


def _rope_tables(seq, dim, period, first):
    half = dim // 2
    inv = ROPE_THETA ** (-jnp.arange(0, dim, 2, dtype=F32) / dim)
    ang = jnp.arange(seq, dtype=F32)[:, None] * inv[None, :]
    cos, sin = jnp.cos(ang), jnp.sin(ang)
    d = jnp.arange(LANES) % period - first
    in_a = (d >= 0) & (d < half)
    in_b = (d >= half) & (d < dim)
    idx = jnp.clip(jnp.where(in_b, d - half, d), 0, half - 1)
    c = jnp.where((in_a | in_b)[None, :], cos[:, idx], 1.0)
    sa = jnp.where(in_a[None, :], -sin[:, idx], 0.0)
    sb = jnp.where(in_b[None, :], sin[:, idx], 0.0)
    return c, sa, sb


def _pad_heads(w, width):
    k = w.shape[0]
    w = w.reshape(k, MLA_HEADS, width)
    return jnp.pad(w, ((0, 0), (0, 0), (0, MLA_SLOT - width))).reshape(k, MLA_PAD)


def _prep_layer(l, norm1_g, w_in, q_norm_g, w_uq, kv_norm_g, w_uk, w_uv, w_pa, w_pb, w_o,
                norm2_g, w_rg, b_rg, w_re, b_re, w_e_gate, w_e_up, w_e_down):
    w = w_in[l]
    o = 0
    parts = []
    for n in (DIL_WIDTH, DIL_WIDTH, DIL_WIDTH, Q_LORA, KV_LORA, MLA_ROPE, 2 * D_MODEL):
        parts.append(w[:, o:o + n])
        o += n
    wq, wk, wv, wcq, wckv, wkr, wg = parts
    wkr_pad = jnp.pad(wkr, ((0, 0), (MLA_NOPE, MLA_SLOT - MLA_NOPE - MLA_ROPE)))
    wr = jnp.pad(jnp.concatenate([w_re[l], w_rg[l]], axis=1),
                 ((0, 0), (0, ROUTE_LANES - N_EXPERTS - N_GROUPS)))
    wrh = wr.astype(BF16)
    wrl = (wr - wrh.astype(F32)).astype(BF16)
    br = jnp.pad(jnp.concatenate([b_re[l], b_rg[l]]), (0, ROUTE_LANES - N_EXPERTS - N_GROUPS))
    return dict(
        g1=norm1_g[l][None, :], wq=wq.astype(BF16), wk=wk.astype(BF16), wv=wv.astype(BF16),
        wcq=wcq.astype(BF16), wckv=wckv.astype(BF16), wkr=wkr_pad.astype(BF16), wg=wg.astype(BF16),
        qn=q_norm_g[l][None, :], kvn=kv_norm_g[l][None, :],
        wuq=_pad_heads(w_uq[l], MLA_QK).astype(BF16), wuk=_pad_heads(w_uk[l], MLA_NOPE).astype(BF16),
        wuv=_pad_heads(w_uv[l], MLA_V).astype(BF16),
        wpa=w_pa[l].astype(BF16), wpb=w_pb[l].astype(BF16), wo=w_o[l].astype(BF16),
        g2=norm2_g[l][None, :], wrh=wrh, wrl=wrl, br=br[None, :].astype(F32),
        weg=w_e_gate, weu=w_e_up, wed=w_e_down)


def _trunk(x3, layers, final_g, tm=512):
    batch, seq, _ = x3.shape
    T = batch * seq
    x = x3.reshape(T, D_MODEL)
    tabs = _rope_tables(seq, ROT_DIM, HEAD_DIM, 0) + _rope_tables(seq, MLA_ROPE, MLA_SLOT, MLA_NOPE)
    fg = final_g[None, :]
    for l, lw in enumerate(layers):
        outs = _inproj(x, lw, tabs, seq, tm)
        n_g = len(DIL_GROUPS)
        qd, kd, vd = outs[0:n_g], outs[n_g:2 * n_g], outs[2 * n_g:3 * n_g]
        qm, km, vm, gates = outs[3 * n_g:]
        dil = []
        for g, (_, d) in enumerate(DIL_GROUPS):
            dil.extend(_dilated_group(qd[g], kd[g], vd[g], batch, seq, g, d))
        ob = _mla(qm, km, vm, batch, seq)
        xm, h2, route, cnt = _outproj(x, dil, ob, gates, lw, seq, tm)
        slot_tok, blk_expert, blk_valid, blk_first, blk_io, comb_idx = _dispatch_plan(route, cnt, T)
        xs = _gather_rows(h2, slot_tok)
        ys = _experts(xs, blk_expert, blk_valid, blk_first, blk_io, lw['weg'], lw['weu'], lw['wed'], l)
        yg = _gather_rows(ys, comb_idx)
        x = _combine(xm, yg, route, fg, l == len(layers) - 1, tm)
    return x.reshape(batch, seq, D_MODEL)


def kernel(x_prompt, x_sample, norm1_g, w_in, q_norm_g, w_uq, kv_norm_g, w_uk, w_uv, w_pa, w_pb, w_o,
           norm2_g, w_rg, b_rg, w_re, b_re, w_e_gate, w_e_up, w_e_down, final_g):
    layers = [_prep_layer(l, norm1_g, w_in, q_norm_g, w_uq, kv_norm_g, w_uk, w_uv, w_pa, w_pb, w_o,
                          norm2_g, w_rg, b_rg, w_re, b_re, w_e_gate, w_e_up, w_e_down)
              for l in range(DEPTH)]
    return (_trunk(x_prompt, layers, final_g), _trunk(x_sample, layers, final_g))
```

```python
import functools

import jax
import jax.numpy as jnp
from jax import lax
from jax.experimental import pallas as pl
from jax.experimental.pallas import tpu as pltpu
from jax.experimental.pallas import tpu_sc as plsc

D_MODEL = 1024
DEPTH = 2
HEAD_DIM = 64
DIL_GROUPS = ((128, 1), (512, 4), (2048, 16))
DIL_HEADS_PER_GROUP = 4
DIL_WIDTH = 768
DIL_OUT = 256
DIL_SCALE = HEAD_DIM ** -0.5
ROT_DIM = 16
ROPE_THETA = 500000.0
DIL_HALF = 64

MLA_HEADS = 8
MLA_NOPE = 64
MLA_ROPE = 32
MLA_QK = 96
MLA_V = 64
MLA_OUT = 512
MLA_SCALE = MLA_QK ** -0.5
Q_LORA = 256
KV_LORA = 128

N_GROUPS = 8
EXPERTS_PER_GROUP = 8
N_EXPERTS = 64
TOP_K = 2
D_EXPERT = 512
EPS = 1e-6

LANES = 128
MLA_SLOT = LANES
MLA_PAD = MLA_HEADS * MLA_SLOT
ROUTE_LANES = LANES
VMEM_LIMIT = 56 * 1024 * 1024

BF16 = jnp.bfloat16
F32 = jnp.float32
NEG = -1e30
LOG2E = 1.4426950408889634


def _cparams(sem):
    return pltpu.CompilerParams(dimension_semantics=sem, vmem_limit_bytes=VMEM_LIMIT)


def _rms(t, g):
    return t * lax.rsqrt(jnp.mean(t * t, axis=-1, keepdims=True) + EPS) * g


def _rope_chunk(t, c, sa, sb, shift):
    return t * c + pltpu.roll(t, LANES - shift, 1) * sa + pltpu.roll(t, shift, 1) * sb


def _inproj_kernel(x_ref, g1_ref, wq_ref, wk_ref, wv_ref, wcq_ref, wckv_ref, wkr_ref, wg_ref,
                   qn_ref, kvn_ref, wuq_ref, wuk_ref, wuv_ref,
                   ca_ref, saa_ref, sab_ref, cb_ref, sba_ref, sbb_ref,
                   q0_ref, q1_ref, q2_ref, k0_ref, k1_ref, k2_ref, v0_ref, v1_ref, v2_ref,
                   qm_ref, km_ref, vm_ref, gate_ref, q_sc, k_sc, v_sc):
    x = x_ref[...]
    h = _rms(x, g1_ref[...]).astype(BF16)
    tm = x.shape[0]

    ca, saa, sab = ca_ref[...], saa_ref[...], sab_ref[...]
    q = jnp.dot(h, wq_ref[...], preferred_element_type=F32)
    k = jnp.dot(h, wk_ref[...], preferred_element_type=F32)
    v = jnp.dot(h, wv_ref[...], preferred_element_type=F32)
    for j in range(DIL_WIDTH // LANES):
        sl = slice(j * LANES, (j + 1) * LANES)
        q_sc[j] = _rope_chunk(q[:, sl], ca, saa, sab, ROT_DIM // 2) * DIL_SCALE
        k_sc[j] = _rope_chunk(k[:, sl], ca, saa, sab, ROT_DIM // 2)
        v_sc[j] = v[:, sl]
    halves = DIL_OUT // LANES
    for src, outs in ((q_sc, (q0_ref, q1_ref, q2_ref)), (k_sc, (k0_ref, k1_ref, k2_ref)),
                      (v_sc, (v0_ref, v1_ref, v2_ref))):
        for g, (_, dil) in enumerate(DIL_GROUPS):
            for hf in range(halves):
                for r in range(dil):
                    rows = pl.ds(r, tm // dil, stride=dil) if dil > 1 else slice(None)
                    outs[g][0, r, :, hf * LANES:(hf + 1) * LANES] = src[g * halves + hf, rows, :].astype(BF16)

    cb, sba, sbb = cb_ref[...], sba_ref[...], sbb_ref[...]
    cq = jnp.dot(h, wcq_ref[...], preferred_element_type=F32)
    cqn = _rms(cq, qn_ref[...]).astype(BF16)
    qm = jnp.dot(cqn, wuq_ref[...], preferred_element_type=F32) * (MLA_SCALE * LOG2E)
    ckv = jnp.dot(h, wckv_ref[...], preferred_element_type=F32)
    c = _rms(ckv, kvn_ref[...]).astype(BF16)
    kn = jnp.dot(c, wuk_ref[...], preferred_element_type=F32)
    vv = jnp.dot(c, wuv_ref[...], preferred_element_type=F32)
    kr = jnp.dot(h, wkr_ref[...], preferred_element_type=F32)
    kr = _rope_chunk(kr, cb, sba, sbb, MLA_ROPE // 2)
    lane = lax.broadcasted_iota(jnp.int32, (1, LANES), 1)
    ones_col = jnp.where(lane == MLA_V, 1.0, 0.0).astype(F32)
    for j in range(MLA_HEADS):
        sl = slice(j * LANES, (j + 1) * LANES)
        qm_ref[:, sl] = _rope_chunk(qm[:, sl], cb, sba, sbb, MLA_ROPE // 2).astype(BF16)
        km_ref[:, sl] = (kn[:, sl] + kr).astype(BF16)
        vm_ref[:, sl] = (vv[:, sl] + ones_col).astype(BF16)

    gate_ref[...] = jax.nn.sigmoid(jnp.dot(h, wg_ref[...], preferred_element_type=F32))


def _inproj(x, lw, tabs, seq, tm):
    T = x.shape[0]
    nt = seq // tm
    row = lambda i: (i, 0)
    const = lambda i: (0, 0)
    tab = lambda i: (i % nt, 0)

    def wspec(a):
        return pl.BlockSpec(a.shape, const, pipeline_mode=pl.Buffered(1))

    weights = [lw['g1'], lw['wq'], lw['wk'], lw['wv'], lw['wcq'], lw['wckv'], lw['wkr'], lw['wg'],
               lw['qn'], lw['kvn'], lw['wuq'], lw['wuk'], lw['wuv']]
    in_specs = ([pl.BlockSpec((tm, D_MODEL), row)] + [wspec(a) for a in weights]
                + [pl.BlockSpec((tm, LANES), tab)] * 6)
    batch = T // seq
    res_map = lambda i: (i // nt, 0, i % nt, 0)
    dil_shape = [jax.ShapeDtypeStruct((batch, d, seq // d, DIL_OUT), BF16) for _, d in DIL_GROUPS] * 3
    dil_specs = [pl.BlockSpec((1, d, tm // d, DIL_OUT), res_map) for _, d in DIL_GROUPS] * 3
    out_shape = dil_shape + [jax.ShapeDtypeStruct((T, MLA_PAD), BF16)] * 3 \
        + [jax.ShapeDtypeStruct((T, 2 * D_MODEL), F32)]
    out_specs = dil_specs + [pl.BlockSpec((tm, MLA_PAD), row)] * 3 + [pl.BlockSpec((tm, 2 * D_MODEL), row)]
    return pl.pallas_call(
        _inproj_kernel, grid=(T // tm,), in_specs=in_specs, out_specs=out_specs, out_shape=out_shape,
        scratch_shapes=[pltpu.VMEM((DIL_WIDTH // LANES, tm, LANES), F32)] * 3,
        compiler_params=_cparams(("parallel",)), name="inproj",
    )(x, *weights, *tabs)


DIL_STEP_TOKENS = 2048


def _dil_kernel(q_ref, kp_ref, kc_ref, kn_ref, vp_ref, vc_ref, vn_ref, o_ref, lse_ref, *, tq, dil, sub_len):
    i = pl.program_id(1)
    qb = LANES
    nh = DIL_HEADS_PER_GROUP
    head = lax.broadcasted_iota(jnp.int32, (1, DIL_OUT), 1) // HEAD_DIM
    qloc = lax.broadcasted_iota(jnp.int32, (nh * qb, 2 * qb), 0) % qb
    rel = lax.broadcasted_iota(jnp.int32, (nh * qb, 2 * qb), 1) - DIL_HALF - qloc
    band = jnp.abs(rel) <= DIL_HALF
    kcol = lax.broadcasted_iota(jnp.int32, (1, 2 * qb), 1) - DIL_HALF

    def window(p_ref, c_ref, n_ref, r, j):
        lo, hi = j * qb - DIL_HALF, j * qb + 2 * qb - DIL_HALF
        parts = []
        if lo < 0:
            parts.append(p_ref[0, r, qb + lo:qb, :])
            lo = 0
        parts.append(c_ref[0, r, lo:min(hi, tq), :])
        if hi > tq:
            parts.append(n_ref[0, r, 0:hi - tq, :])
        return jnp.concatenate(parts, axis=0) if len(parts) > 1 else parts[0]

    for r in range(dil):
        for j in range(tq // qb):
            q = q_ref[0, r, j * qb:(j + 1) * qb, :]
            k = window(kp_ref, kc_ref, kn_ref, r, j)
            v = window(vp_ref, vc_ref, vn_ref, r, j)
            qs = jnp.concatenate([jnp.where(head == hd, q, jnp.zeros_like(q)) for hd in range(nh)], axis=0)
            s = lax.dot_general(qs, k, (((1,), (1,)), ((), ())), preferred_element_type=F32)
            kpos = kcol + (i * tq + j * qb)
            s = jnp.where(band & (kpos >= 0) & (kpos < sub_len), s, NEG)
            m = jnp.max(s, axis=-1, keepdims=True)
            p = jnp.exp(s - m)
            den = jnp.sum(p, axis=-1, keepdims=True)
            oh = jnp.dot(p.astype(BF16), v, preferred_element_type=F32) * (1.0 / den)
            lse = m + jnp.log(den)
            o_acc = oh[0:qb]
            l_acc = jnp.broadcast_to(lse[0:qb], (qb, DIL_OUT))
            for hd in range(1, nh):
                o_acc = jnp.where(head == hd, oh[hd * qb:(hd + 1) * qb], o_acc)
                l_acc = jnp.where(head == hd, lse[hd * qb:(hd + 1) * qb], l_acc)
            rows = pl.ds(j * qb * dil + r, qb, stride=dil) if dil > 1 else slice(j * qb, (j + 1) * qb)
            for hf in range(DIL_OUT // LANES):
                o_ref[0, hf, rows, :] = o_acc[:, hf * LANES:(hf + 1) * LANES]
                lse_ref[0, hf, rows, :] = l_acc[:, hf * LANES:(hf + 1) * LANES]


def _dilated_group(qd, kd, vd, batch, seq, g, dil):
    L = seq // dil
    tq = min(L, DIL_STEP_TOKENS // dil)
    nq = L // tq
    hb = tq // LANES
    nhb = L // LANES
    cur = lambda b, i: (b, 0, i, 0)
    prev = lambda b, i: (b, 0, jnp.maximum(i * hb - 1, 0), 0)
    nxt = lambda b, i: (b, 0, jnp.minimum((i + 1) * hb, nhb - 1), 0)
    blk = (1, dil, tq, DIL_OUT)
    halo = (1, dil, LANES, DIL_OUT)
    halves = DIL_OUT // LANES
    out_blk = (1, halves, tq * dil, LANES)
    out_map = lambda b, i: (b, 0, i, 0)
    o, lse = pl.pallas_call(
        functools.partial(_dil_kernel, tq=tq, dil=dil, sub_len=L),
        grid=(batch, nq),
        in_specs=[pl.BlockSpec(blk, cur),
                  pl.BlockSpec(halo, prev), pl.BlockSpec(blk, cur), pl.BlockSpec(halo, nxt),
                  pl.BlockSpec(halo, prev), pl.BlockSpec(blk, cur), pl.BlockSpec(halo, nxt)],
        out_specs=[pl.BlockSpec(out_blk, out_map), pl.BlockSpec(out_blk, out_map)],
        out_shape=[jax.ShapeDtypeStruct((batch, halves, seq, LANES), F32)] * 2,
        compiler_params=_cparams(("parallel", "parallel")), name=f"dilated_g{g}",
    )(qd, kd, kd, kd, vd, vd, vd)
    return o, lse


def _mla_kernel(q_ref, k_ref, v_ref, o_ref, m_sc, acc_sc, *, tk, nk, unroll):
    nc = tk // LANES
    m_sc[...] = jnp.full(m_sc.shape, NEG, F32)
    acc_sc[...] = jnp.zeros(acc_sc.shape, F32)

    def step(off, hh):
        sl = slice(hh * LANES, (hh + 1) * LANES)
        q = q_ref[0, :, sl]
        k = k_ref[0, pl.ds(off, tk), sl]
        v = v_ref[0, pl.ds(off, tk), sl]
        s = lax.dot_general(q, k, (((1,), (1,)), ((), ())), preferred_element_type=F32)
        cols = [s[:, c * LANES:(c + 1) * LANES] for c in range(nc)]
        m_old = m_sc[hh]
        m_new = jnp.maximum(m_old, jnp.max(functools.reduce(jnp.maximum, cols), axis=-1, keepdims=True))
        p = jnp.concatenate([jnp.exp2(c - m_new) for c in cols], axis=1).astype(BF16)
        acc_sc[hh] = acc_sc[hh] * jnp.exp2(m_old - m_new) + jnp.dot(p, v, preferred_element_type=F32)
        m_sc[hh] = m_new

    def body(j, carry):
        for u in range(unroll):
            off = pl.multiple_of((j * unroll + u) * tk, tk)
            for hh in range(2):
                step(off, hh)
        return carry

    lax.fori_loop(0, nk // unroll, body, 0)
    outs = []
    for hh in range(2):
        acc = acc_sc[hh]
        outs.append(acc * (1.0 / acc[:, MLA_V:MLA_V + 1]))
    lane = lax.broadcasted_iota(jnp.int32, (1, LANES), 1)
    o_ref[0] = jnp.where(lane < MLA_V, outs[0], pltpu.roll(outs[1], MLA_V, 1)).astype(o_ref.dtype)


def _mla(qm, km, vm, batch, seq, tq=1024, tk=2048, unroll=1):
    tq = min(seq, tq)
    tk = min(seq, tk)
    nk = seq // tk
    unroll = unroll if nk % unroll == 0 else 1
    q3 = qm.reshape(batch, seq, MLA_PAD)
    k3 = km.reshape(batch, seq, MLA_PAD)
    v3 = vm.reshape(batch, seq, MLA_PAD)
    resident = lambda b, h, i: (b, 0, h)
    o = pl.pallas_call(
        functools.partial(_mla_kernel, tk=tk, nk=nk, unroll=unroll),
        grid=(batch, MLA_HEADS // 2, seq // tq),
        in_specs=[pl.BlockSpec((1, tq, 2 * LANES), lambda b, h, i: (b, i, h)),
                  pl.BlockSpec((1, seq, 2 * LANES), resident, pipeline_mode=pl.Buffered(1)),
                  pl.BlockSpec((1, seq, 2 * LANES), resident, pipeline_mode=pl.Buffered(1))],
        out_specs=pl.BlockSpec((1, tq, LANES), lambda b, h, i: (b, i, h)),
        out_shape=jax.ShapeDtypeStruct((batch, seq, MLA_OUT), BF16),
        scratch_shapes=[pltpu.VMEM((2, tq, LANES), F32), pltpu.VMEM((2, tq, LANES), F32)],
        compiler_params=_cparams(("parallel", "parallel", "arbitrary")), name="mla",
    )(q3, k3, v3)
    return o.reshape(batch * seq, MLA_OUT)


def _outproj_kernel(x_ref, o0_ref, l0_ref, o1_ref, l1_ref, o2_ref, l2_ref, ob_ref, gate_ref,
                    wpa_ref, wpb_ref, wo_ref, g2_ref, wrh_ref, wrl_ref, br_ref,
                    xm_ref, h2_ref, route_ref, cnt_ref, cnt_sc):
    def halves(ref):
        return jnp.concatenate([ref[0, hf] for hf in range(ref.shape[1])], axis=1)

    l0, l1, l2 = halves(l0_ref), halves(l1_ref), halves(l2_ref)
    m = jnp.maximum(jnp.maximum(l0, l1), l2)
    w0, w1, w2 = jnp.exp(l0 - m), jnp.exp(l1 - m), jnp.exp(l2 - m)
    oa = (w0 * halves(o0_ref) + w1 * halves(o1_ref) + w2 * halves(o2_ref)) / (w0 + w1 + w2)
    pa = jnp.dot(oa.astype(BF16), wpa_ref[...], preferred_element_type=F32)
    pb = jnp.dot(ob_ref[...], wpb_ref[...], preferred_element_type=F32)
    merged = gate_ref[:, :D_MODEL] * pa + gate_ref[:, D_MODEL:] * pb
    xm = x_ref[...] + jnp.dot(merged.astype(BF16), wo_ref[...], preferred_element_type=F32)
    xm_ref[...] = xm
    h2 = _rms(xm, g2_ref[...])
    h2_ref[...] = h2

    hi = h2.astype(BF16)
    lo = (h2 - hi.astype(F32)).astype(BF16)
    lg = (jnp.dot(hi, wrh_ref[...], preferred_element_type=F32)
          + jnp.dot(lo, wrh_ref[...], preferred_element_type=F32)
          + jnp.dot(hi, wrl_ref[...], preferred_element_type=F32)) + br_ref[...]
    lane = lax.broadcasted_iota(jnp.int32, lg.shape, 1)
    lanef = lane.astype(F32)
    big = float(ROUTE_LANES)
    gmask = (lane >= N_EXPERTS) & (lane < N_EXPERTS + N_GROUPS)
    gl = jnp.where(gmask, lg, NEG)
    gmax = jnp.max(gl, axis=-1, keepdims=True)
    gidx = jnp.min(jnp.where(gl == gmax, lanef, big), axis=-1, keepdims=True) - float(N_EXPERTS)
    p_grp = 1.0 / jnp.sum(jnp.where(gmask, jnp.exp(gl - gmax), 0.0), axis=-1, keepdims=True)
    emask = (lane // EXPERTS_PER_GROUP).astype(F32) == gidx
    el = jnp.where(emask, lg, NEG)
    m1 = jnp.max(el, axis=-1, keepdims=True)
    i1 = jnp.min(jnp.where(el == m1, lanef, big), axis=-1, keepdims=True)
    el2 = jnp.where(lanef == i1, NEG, el)
    m2 = jnp.max(el2, axis=-1, keepdims=True)
    i2 = jnp.min(jnp.where(el2 == m2, lanef, big), axis=-1, keepdims=True)
    r = jnp.exp(m2 - m1)
    wa = p_grp / (1.0 + r)
    wb = p_grp * r / (1.0 + r)
    @pl.when(pl.program_id(0) == 0)
    def _():
        cnt_sc[...] = jnp.zeros(cnt_sc.shape, F32)

    oh1 = lanef == i1
    oh2 = lanef == i2
    oh = jnp.where(oh1 | oh2, 1.0, 0.0)
    tm = lg.shape[0]
    earlier = (lax.broadcasted_iota(jnp.int32, (tm, tm), 0) > lax.broadcasted_iota(jnp.int32, (tm, tm), 1))
    before = jnp.dot(jnp.where(earlier, 1.0, 0.0).astype(BF16), oh.astype(BF16),
                     preferred_element_type=F32) + cnt_sc[0:1, :]
    r1 = jnp.sum(jnp.where(oh1, before, 0.0), axis=-1, keepdims=True)
    r2 = jnp.sum(jnp.where(oh2, before, 0.0), axis=-1, keepdims=True)
    cnt = cnt_sc[...] + jnp.sum(oh, axis=0, keepdims=True)
    cnt_sc[...] = cnt
    cnt_ref[...] = cnt
    route_ref[...] = jnp.where(lane == 0, i1, jnp.where(lane == 1, i2, jnp.where(lane == 2, wa,
                               jnp.where(lane == 3, wb, jnp.where(lane == 4, r1,
                                                                  jnp.where(lane == 5, r2, 0.0))))))


def _outproj(x, dil_outs, ob, gates, lw, seq, tm):
    T = x.shape[0]
    nt = seq // tm
    row = lambda i: (i, 0)
    const = lambda i: (0, 0)
    weights = [lw['wpa'], lw['wpb'], lw['wo'], lw['g2'], lw['wrh'], lw['wrl'], lw['br']]
    acts = [x] + list(dil_outs) + [ob, gates]
    dil_spec = pl.BlockSpec((1, DIL_OUT // LANES, tm, LANES), lambda i: (i // nt, 0, i % nt, 0))
    in_specs = [pl.BlockSpec((tm, D_MODEL), row)] + [dil_spec] * len(dil_outs) \
        + [pl.BlockSpec((tm, ob.shape[1]), row), pl.BlockSpec((tm, gates.shape[1]), row)] \
        + [pl.BlockSpec(a.shape, const, pipeline_mode=pl.Buffered(1)) for a in weights]
    return pl.pallas_call(
        _outproj_kernel, grid=(T // tm,), in_specs=in_specs,
        out_specs=[pl.BlockSpec((tm, D_MODEL), row), pl.BlockSpec((tm, D_MODEL), row),
                   pl.BlockSpec((tm, ROUTE_LANES), row), pl.BlockSpec((8, ROUTE_LANES), const)],
        out_shape=[jax.ShapeDtypeStruct((T, D_MODEL), F32), jax.ShapeDtypeStruct((T, D_MODEL), F32),
                   jax.ShapeDtypeStruct((T, ROUTE_LANES), F32), jax.ShapeDtypeStruct((8, ROUTE_LANES), F32)],
        scratch_shapes=[pltpu.VMEM((8, ROUTE_LANES), F32)],
        compiler_params=_cparams(("arbitrary",)), name="outproj_router",
    )(*acts, *weights)


GATHER_WINDOW = 32


def _gather_rows(src, idx):
    m = idx.shape[0]
    width = src.shape[1]
    info = plsc.get_sparse_core_info()
    n_workers = info.num_cores * info.num_subcores
    per_w = m // n_workers
    n_pairs = per_w // (2 * GATHER_WINDOW)
    assert n_pairs * 2 * GATHER_WINDOW * n_workers == m
    mesh = plsc.VectorSubcoreMesh(core_axis_name="core", subcore_axis_name="subcore")

    @functools.partial(
        pl.kernel, out_type=jax.ShapeDtypeStruct((m, width), src.dtype), mesh=mesh, name="gather_rows",
        scratch_types=[pltpu.VMEM((per_w,), jnp.int32),
                       pltpu.VMEM((2, GATHER_WINDOW, width), src.dtype),
                       pltpu.SemaphoreType.DMA((2,))])
    def gather(src_hbm, idx_hbm, out_hbm, idx_v, rows_v, sems):
        wid = lax.axis_index("subcore") * info.num_cores + lax.axis_index("core")
        base = wid * per_w
        pltpu.sync_copy(idx_hbm.at[pl.ds(base, per_w)], idx_v)

        def fetch(c, b):
            return pltpu.make_async_copy(src_hbm.at[idx_v.at[pl.ds(c * GATHER_WINDOW, GATHER_WINDOW)]],
                                         rows_v.at[b], sems.at[b])

        def flush(c, b):
            pltpu.sync_copy(rows_v.at[b], out_hbm.at[pl.ds(base + c * GATHER_WINDOW, GATHER_WINDOW)])

        fetch(0, 0).start()

        @pl.loop(0, n_pairs)
        def _(p):
            c = 2 * p
            fetch(c + 1, 1).start()
            fetch(c, 0).wait()
            flush(c, 0)

            @pl.when(p + 1 < n_pairs)
            def _():
                fetch(c + 2, 0).start()

            fetch(c + 1, 1).wait()
            flush(c + 1, 1)

    return gather(src, idx)


EXPERT_BLOCK = 512


def _expert_kernel(be_ref, bv_ref, bf_ref, bio_ref, xs_ref, wg_ref, wu_ref, wd_ref, y_ref,
                   wg_sc, wu_sc, wd_sc):
    i = pl.program_id(0)

    @pl.when(bf_ref[i] > 0)
    def _():
        wg_sc[...] = wg_ref[0].astype(BF16)
        wu_sc[...] = wu_ref[0].astype(BF16)
        wd_sc[...] = wd_ref[0].astype(BF16)

    @pl.when(bv_ref[i] > 0)
    def _():
        xb = xs_ref[...].astype(BF16)
        g = jnp.dot(xb, wg_sc[...], preferred_element_type=F32)
        u = jnp.dot(xb, wu_sc[...], preferred_element_type=F32)
        hb = (g * jax.nn.sigmoid(g) * u).astype(BF16)
        y_ref[...] = jnp.dot(hb, wd_sc[...], preferred_element_type=F32)


def _experts(xs, blk_expert, blk_valid, blk_first, blk_io, wg, wu, wd, layer):
    n_slots = xs.shape[0]
    nb = n_slots // EXPERT_BLOCK
    grid_spec = pltpu.PrefetchScalarGridSpec(
        num_scalar_prefetch=4, grid=(nb,),
        in_specs=[pl.BlockSpec((EXPERT_BLOCK, D_MODEL), lambda i, be, bv, bf, bio: (bio[i], 0)),
                  pl.BlockSpec((None, 1, D_MODEL, D_EXPERT), lambda i, be, bv, bf, bio: (layer, be[i], 0, 0)),
                  pl.BlockSpec((None, 1, D_MODEL, D_EXPERT), lambda i, be, bv, bf, bio: (layer, be[i], 0, 0)),
                  pl.BlockSpec((None, 1, D_EXPERT, D_MODEL), lambda i, be, bv, bf, bio: (layer, be[i], 0, 0))],
        out_specs=pl.BlockSpec((EXPERT_BLOCK, D_MODEL), lambda i, be, bv, bf, bio: (bio[i], 0)),
        scratch_shapes=[pltpu.VMEM((D_MODEL, D_EXPERT), BF16), pltpu.VMEM((D_MODEL, D_EXPERT), BF16),
                        pltpu.VMEM((D_EXPERT, D_MODEL), BF16)])
    return pl.pallas_call(
        _expert_kernel, grid_spec=grid_spec,
        out_shape=jax.ShapeDtypeStruct((n_slots, D_MODEL), F32),
        compiler_params=_cparams(("arbitrary",)), name="experts",
    )(blk_expert, blk_valid, blk_first, blk_io, xs, wg, wu, wd)


def _combine_kernel(x_ref, ya_ref, yb_ref, route_ref, g_ref, o_ref, *, final):
    route = route_ref[...]
    wa = route[:, 2:3]
    wb = route[:, 3:4]
    x = x_ref[...] + (ya_ref[...] * wa + yb_ref[...] * wb)
    if final:
        x = _rms(x, g_ref[...])
    o_ref[...] = x


def _combine(x, yg, route, g, final, tm):
    T = x.shape[0]
    nt = T // tm
    return pl.pallas_call(
        functools.partial(_combine_kernel, final=final), grid=(nt,),
        in_specs=[pl.BlockSpec((tm, D_MODEL), lambda i: (i, 0)),
                  pl.BlockSpec((tm, D_MODEL), lambda i: (i, 0)),
                  pl.BlockSpec((tm, D_MODEL), lambda i: (i + nt, 0)),
                  pl.BlockSpec((tm, ROUTE_LANES), lambda i: (i, 0)),
                  pl.BlockSpec((1, D_MODEL), lambda i: (0, 0))],
        out_specs=pl.BlockSpec((tm, D_MODEL), lambda i: (i, 0)),
        out_shape=jax.ShapeDtypeStruct((T, D_MODEL), F32),
        compiler_params=_cparams(("parallel",)), name="combine",
    )(x, yg, yg, route, g)


def _dispatch_plan(route, cnt, T):
    A = T * TOP_K
    expert = route[:, 0:TOP_K].astype(jnp.int32)
    rank = route[:, 4:4 + TOP_K].astype(jnp.int32)
    counts = cnt[0, :N_EXPERTS].astype(jnp.int32)
    padded = (counts + EXPERT_BLOCK - 1) // EXPERT_BLOCK * EXPERT_BLOCK
    pad_end = jnp.cumsum(padded)
    pad_start = pad_end - padded
    slot = pad_start[expert] + rank
    n_blocks = A // EXPERT_BLOCK + N_EXPERTS
    n_slots = n_blocks * EXPERT_BLOCK
    tok = jnp.broadcast_to(jnp.arange(T, dtype=jnp.int32)[:, None], (T, TOP_K))
    blk_start = jnp.arange(n_blocks, dtype=jnp.int32) * EXPERT_BLOCK
    blk_valid = (blk_start < pad_end[-1]).astype(jnp.int32)
    blk_io = jnp.minimum(jnp.arange(n_blocks, dtype=jnp.int32), jnp.sum(blk_valid) - 1)
    blk_expert = jnp.minimum(jnp.sum((blk_start[:, None] >= pad_end[None, :]).astype(jnp.int32), axis=1),
                             N_EXPERTS - 1)[blk_io]
    blk_first = jnp.concatenate([jnp.ones((1,), jnp.int32),
                                 (blk_expert[1:] != blk_expert[:-1]).astype(jnp.int32)])
    _, tok_sorted = lax.sort_key_val(slot.reshape(A), tok.reshape(A))
    shift = pad_start - (jnp.cumsum(counts) - counts)
    src = (blk_start - shift[blk_expert])[:, None] + jnp.arange(EXPERT_BLOCK, dtype=jnp.int32)[None, :]
    slot_tok = tok_sorted[jnp.clip(src, 0, A - 1)].reshape(n_slots)
    comb_idx = jnp.concatenate([slot[:, 0], slot[:, 1]])
    return slot_tok, blk_expert, blk_valid, blk_first, blk_io, comb_idx


def _rope_tables(seq, dim, period, first):
    half = dim // 2
    inv = ROPE_THETA ** (-jnp.arange(0, dim, 2, dtype=F32) / dim)
    ang = jnp.arange(seq, dtype=F32)[:, None] * inv[None, :]
    cos, sin = jnp.cos(ang), jnp.sin(ang)
    d = jnp.arange(LANES) % period - first
    in_a = (d >= 0) & (d < half)
    in_b = (d >= half) & (d < dim)
    idx = jnp.clip(jnp.where(in_b, d - half, d), 0, half - 1)
    c = jnp.where((in_a | in_b)[None, :], cos[:, idx], 1.0)
    sa = jnp.where(in_a[None, :], -sin[:, idx], 0.0)
    sb = jnp.where(in_b[None, :], sin[:, idx], 0.0)
    return c, sa, sb


def _pad_heads(w, width):
    k = w.shape[0]
    w = w.reshape(k, MLA_HEADS, width)
    return jnp.pad(w, ((0, 0), (0, 0), (0, MLA_SLOT - width))).reshape(k, MLA_PAD)


def _prep_layer(l, norm1_g, w_in, q_norm_g, w_uq, kv_norm_g, w_uk, w_uv, w_pa, w_pb, w_o,
                norm2_g, w_rg, b_rg, w_re, b_re, w_e_gate, w_e_up, w_e_down):
    w = w_in[l]
    o = 0
    parts = []
    for n in (DIL_WIDTH, DIL_WIDTH, DIL_WIDTH, Q_LORA, KV_LORA, MLA_ROPE, 2 * D_MODEL):
        parts.append(w[:, o:o + n])
        o += n
    wq, wk, wv, wcq, wckv, wkr, wg = parts
    wkr_pad = jnp.pad(wkr, ((0, 0), (MLA_NOPE, MLA_SLOT - MLA_NOPE - MLA_ROPE)))
    wr = jnp.pad(jnp.concatenate([w_re[l], w_rg[l]], axis=1),
                 ((0, 0), (0, ROUTE_LANES - N_EXPERTS - N_GROUPS)))
    wrh = wr.astype(BF16)
    wrl = (wr - wrh.astype(F32)).astype(BF16)
    br = jnp.pad(jnp.concatenate([b_re[l], b_rg[l]]), (0, ROUTE_LANES - N_EXPERTS - N_GROUPS))
    return dict(
        g1=norm1_g[l][None, :], wq=wq.astype(BF16), wk=wk.astype(BF16), wv=wv.astype(BF16),
        wcq=wcq.astype(BF16), wckv=wckv.astype(BF16), wkr=wkr_pad.astype(BF16), wg=wg.astype(BF16),
        qn=q_norm_g[l][None, :], kvn=kv_norm_g[l][None, :],
        wuq=_pad_heads(w_uq[l], MLA_QK).astype(BF16), wuk=_pad_heads(w_uk[l], MLA_NOPE).astype(BF16),
        wuv=_pad_heads(w_uv[l], MLA_V).astype(BF16),
        wpa=w_pa[l].astype(BF16), wpb=w_pb[l].astype(BF16), wo=w_o[l].astype(BF16),
        g2=norm2_g[l][None, :], wrh=wrh, wrl=wrl, br=br[None, :].astype(F32),
        weg=w_e_gate, weu=w_e_up, wed=w_e_down)


def _trunk(x3, layers, final_g, tm=512):
    batch, seq, _ = x3.shape
    T = batch * seq
    x = x3.reshape(T, D_MODEL)
    tabs = _rope_tables(seq, ROT_DIM, HEAD_DIM, 0) + _rope_tables(seq, MLA_ROPE, MLA_SLOT, MLA_NOPE)
    fg = final_g[None, :]
    for l, lw in enumerate(layers):
        outs = _inproj(x, lw, tabs, seq, tm)
        n_g = len(DIL_GROUPS)
        qd, kd, vd = outs[0:n_g], outs[n_g:2 * n_g], outs[2 * n_g:3 * n_g]
        qm, km, vm, gates = outs[3 * n_g:]
        dil = []
        for g, (_, d) in enumerate(DIL_GROUPS):
            dil.extend(_dilated_group(qd[g], kd[g], vd[g], batch, seq, g, d))
        ob = _mla(qm, km, vm, batch, seq)
        xm, h2, route, cnt = _outproj(x, dil, ob, gates, lw, seq, tm)
        slot_tok, blk_expert, blk_valid, blk_first, blk_io, comb_idx = _dispatch_plan(route, cnt, T)
        xs = _gather_rows(h2, slot_tok)
        ys = _experts(xs, blk_expert, blk_valid, blk_first, blk_io, lw['weg'], lw['weu'], lw['wed'], l)
        yg = _gather_rows(ys, comb_idx)
        x = _combine(xm, yg, route, fg, l == len(layers) - 1, tm)
    return x.reshape(batch, seq, D_MODEL)


def kernel(x_prompt, x_sample, norm1_g, w_in, q_norm_g, w_uq, kv_norm_g, w_uk, w_uv, w_pa, w_pb, w_o,
           norm2_g, w_rg, b_rg, w_re, b_re, w_e_gate, w_e_up, w_e_down, final_g):
    layers = [_prep_layer(l, norm1_g, w_in, q_norm_g, w_uq, kv_norm_g, w_uk, w_uv, w_pa, w_pb, w_o,
                          norm2_g, w_rg, b_rg, w_re, b_re, w_e_gate, w_e_up, w_e_down)
              for l in range(DEPTH)]
    return (_trunk(x_prompt, layers, final_g), _trunk(x_sample, layers, final_g))
```

```python
import functools

import jax
import jax.numpy as jnp
from jax import lax
from jax.experimental import pallas as pl
from jax.experimental.pallas import tpu as pltpu
from jax.experimental.pallas import tpu_sc as plsc

D_MODEL = 1024
DEPTH = 2
HEAD_DIM = 64
DIL_GROUPS = ((128, 1), (512, 4), (2048, 16))
DIL_HEADS_PER_GROUP = 4
DIL_WIDTH = 768
DIL_OUT = 256
DIL_SCALE = HEAD_DIM ** -0.5
ROT_DIM = 16
ROPE_THETA = 500000.0
DIL_HALF = 64

MLA_HEADS = 8
MLA_NOPE = 64
MLA_ROPE = 32
MLA_QK = 96
MLA_V = 64
MLA_OUT = 512
MLA_SCALE = MLA_QK ** -0.5
Q_LORA = 256
KV_LORA = 128

N_GROUPS = 8
EXPERTS_PER_GROUP = 8
N_EXPERTS = 64
TOP_K = 2
D_EXPERT = 512
EPS = 1e-6

LANES = 128
MLA_SLOT = LANES
MLA_PAD = MLA_HEADS * MLA_SLOT
ROUTE_LANES = LANES
VMEM_LIMIT = 56 * 1024 * 1024
MLA_KV_DOUBLE_BUFFER_MAX = 2 * 1024 * 1024

BF16 = jnp.bfloat16
F32 = jnp.float32
NEG = -1e30
LOG2E = 1.4426950408889634


def _cparams(sem):
    return pltpu.CompilerParams(dimension_semantics=sem, vmem_limit_bytes=VMEM_LIMIT)


def _rms(t, g):
    return t * lax.rsqrt(jnp.mean(t * t, axis=-1, keepdims=True) + EPS) * g


def _rope_chunk(t, c, sa, sb, shift):
    return t * c + pltpu.roll(t, LANES - shift, 1) * sa + pltpu.roll(t, shift, 1) * sb


def _inproj_kernel(x_ref, g1_ref, wq_ref, wk_ref, wv_ref, wcq_ref, wckv_ref, wkr_ref, wg_ref,
                   qn_ref, kvn_ref, wuq_ref, wuk_ref, wuv_ref,
                   ca_ref, saa_ref, sab_ref, cb_ref, sba_ref, sbb_ref,
                   q0_ref, q1_ref, q2_ref, k0_ref, k1_ref, k2_ref, v0_ref, v1_ref, v2_ref,
                   qm_ref, km_ref, vm_ref, gate_ref, q_sc, k_sc, v_sc):
    x = x_ref[...]
    h = _rms(x, g1_ref[...]).astype(BF16)
    tm = x.shape[0]

    ca, saa, sab = ca_ref[...], saa_ref[...], sab_ref[...]
    q = jnp.dot(h, wq_ref[...], preferred_element_type=F32)
    k = jnp.dot(h, wk_ref[...], preferred_element_type=F32)
    v = jnp.dot(h, wv_ref[...], preferred_element_type=F32)
    for j in range(DIL_WIDTH // LANES):
        sl = slice(j * LANES, (j + 1) * LANES)
        q_sc[j] = _rope_chunk(q[:, sl], ca, saa, sab, ROT_DIM // 2) * DIL_SCALE
        k_sc[j] = _rope_chunk(k[:, sl], ca, saa, sab, ROT_DIM // 2)
        v_sc[j] = v[:, sl]
    halves = DIL_OUT // LANES
    for src, outs in ((q_sc, (q0_ref, q1_ref, q2_ref)), (k_sc, (k0_ref, k1_ref, k2_ref)),
                      (v_sc, (v0_ref, v1_ref, v2_ref))):
        for g, (_, dil) in enumerate(DIL_GROUPS):
            for hf in range(halves):
                for r in range(dil):
                    rows = pl.ds(r, tm // dil, stride=dil) if dil > 1 else slice(None)
                    outs[g][0, r, :, hf * LANES:(hf + 1) * LANES] = src[g * halves + hf, rows, :].astype(BF16)

    cb, sba, sbb = cb_ref[...], sba_ref[...], sbb_ref[...]
    cq = jnp.dot(h, wcq_ref[...], preferred_element_type=F32)
    cqn = _rms(cq, qn_ref[...]).astype(BF16)
    qm = jnp.dot(cqn, wuq_ref[...], preferred_element_type=F32) * (MLA_SCALE * LOG2E)
    ckv = jnp.dot(h, wckv_ref[...], preferred_element_type=F32)
    c = _rms(ckv, kvn_ref[...]).astype(BF16)
    kn = jnp.dot(c, wuk_ref[...], preferred_element_type=F32)
    vv = jnp.dot(c, wuv_ref[...], preferred_element_type=F32)
    kr = jnp.dot(h, wkr_ref[...], preferred_element_type=F32)
    kr = _rope_chunk(kr, cb, sba, sbb, MLA_ROPE // 2)
    lane = lax.broadcasted_iota(jnp.int32, (1, LANES), 1)
    ones_col = jnp.where(lane == MLA_V, 1.0, 0.0).astype(F32)
    for j in range(MLA_HEADS):
        sl = slice(j * LANES, (j + 1) * LANES)
        qm_ref[:, sl] = _rope_chunk(qm[:, sl], cb, sba, sbb, MLA_ROPE // 2).astype(BF16)
        km_ref[:, sl] = (kn[:, sl] + kr).astype(BF16)
        vm_ref[:, sl] = (vv[:, sl] + ones_col).astype(BF16)

    gate_ref[...] = jax.nn.sigmoid(jnp.dot(h, wg_ref[...], preferred_element_type=F32))


def _inproj(x, lw, tabs, seq, tm):
    T = x.shape[0]
    nt = seq // tm
    row = lambda i: (i, 0)
    const = lambda i: (0, 0)
    tab = lambda i: (i % nt, 0)

    def wspec(a):
        return pl.BlockSpec(a.shape, const, pipeline_mode=pl.Buffered(1))

    weights = [lw['g1'], lw['wq'], lw['wk'], lw['wv'], lw['wcq'], lw['wckv'], lw['wkr'], lw['wg'],
               lw['qn'], lw['kvn'], lw['wuq'], lw['wuk'], lw['wuv']]
    in_specs = ([pl.BlockSpec((tm, D_MODEL), row)] + [wspec(a) for a in weights]
                + [pl.BlockSpec((tm, LANES), tab)] * 6)
    batch = T // seq
    res_map = lambda i: (i // nt, 0, i % nt, 0)
    dil_shape = [jax.ShapeDtypeStruct((batch, d, seq // d, DIL_OUT), BF16) for _, d in DIL_GROUPS] * 3
    dil_specs = [pl.BlockSpec((1, d, tm // d, DIL_OUT), res_map) for _, d in DIL_GROUPS] * 3
    out_shape = dil_shape + [jax.ShapeDtypeStruct((T, MLA_PAD), BF16)] * 3 \
        + [jax.ShapeDtypeStruct((T, 2 * D_MODEL), F32)]
    out_specs = dil_specs + [pl.BlockSpec((tm, MLA_PAD), row)] * 3 + [pl.BlockSpec((tm, 2 * D_MODEL), row)]
    return pl.pallas_call(
        _inproj_kernel, grid=(T // tm,), in_specs=in_specs, out_specs=out_specs, out_shape=out_shape,
        scratch_shapes=[pltpu.VMEM((DIL_WIDTH // LANES, tm, LANES), F32)] * 3,
        compiler_params=_cparams(("parallel",)), name="inproj",
    )(x, *weights, *tabs)


DIL_STEP_TOKENS = 2048


def _dil_kernel(q_ref, kp_ref, kc_ref, kn_ref, vp_ref, vc_ref, vn_ref, o_ref, lse_ref, *, tq, dil, sub_len):
    i = pl.program_id(1)
    qb = LANES
    nh = DIL_HEADS_PER_GROUP
    head = lax.broadcasted_iota(jnp.int32, (1, DIL_OUT), 1) // HEAD_DIM
    qloc = lax.broadcasted_iota(jnp.int32, (nh * qb, 2 * qb), 0) % qb
    rel = lax.broadcasted_iota(jnp.int32, (nh * qb, 2 * qb), 1) - DIL_HALF - qloc
    band = jnp.abs(rel) <= DIL_HALF
    kcol = lax.broadcasted_iota(jnp.int32, (1, 2 * qb), 1) - DIL_HALF

    def window(p_ref, c_ref, n_ref, r, j):
        lo, hi = j * qb - DIL_HALF, j * qb + 2 * qb - DIL_HALF
        parts = []
        if lo < 0:
            parts.append(p_ref[0, r, qb + lo:qb, :])
            lo = 0
        parts.append(c_ref[0, r, lo:min(hi, tq), :])
        if hi > tq:
            parts.append(n_ref[0, r, 0:hi - tq, :])
        return jnp.concatenate(parts, axis=0) if len(parts) > 1 else parts[0]

    for r in range(dil):
        for j in range(tq // qb):
            q = q_ref[0, r, j * qb:(j + 1) * qb, :]
            k = window(kp_ref, kc_ref, kn_ref, r, j)
            v = window(vp_ref, vc_ref, vn_ref, r, j)
            qs = jnp.concatenate([jnp.where(head == hd, q, jnp.zeros_like(q)) for hd in range(nh)], axis=0)
            s = lax.dot_general(qs, k, (((1,), (1,)), ((), ())), preferred_element_type=F32)
            kpos = kcol + (i * tq + j * qb)
            s = jnp.where(band & (kpos >= 0) & (kpos < sub_len), s, NEG)
            m = jnp.max(s, axis=-1, keepdims=True)
            p = jnp.exp(s - m)
            den = jnp.sum(p, axis=-1, keepdims=True)
            oh = jnp.dot(p.astype(BF16), v, preferred_element_type=F32) * (1.0 / den)
            lse = m + jnp.log(den)
            o_acc = oh[0:qb]
            l_acc = jnp.broadcast_to(lse[0:qb], (qb, DIL_OUT))
            for hd in range(1, nh):
                o_acc = jnp.where(head == hd, oh[hd * qb:(hd + 1) * qb], o_acc)
                l_acc = jnp.where(head == hd, lse[hd * qb:(hd + 1) * qb], l_acc)
            rows = pl.ds(j * qb * dil + r, qb, stride=dil) if dil > 1 else slice(j * qb, (j + 1) * qb)
            for hf in range(DIL_OUT // LANES):
                o_ref[0, hf, rows, :] = o_acc[:, hf * LANES:(hf + 1) * LANES]
                lse_ref[0, hf, rows, :] = l_acc[:, hf * LANES:(hf + 1) * LANES]


def _dilated_group(qd, kd, vd, batch, seq, g, dil):
    L = seq // dil
    tq = min(L, DIL_STEP_TOKENS // dil)
    nq = L // tq
    hb = tq // LANES
    nhb = L // LANES
    cur = lambda b, i: (b, 0, i, 0)
    prev = lambda b, i: (b, 0, jnp.maximum(i * hb - 1, 0), 0)
    nxt = lambda b, i: (b, 0, jnp.minimum((i + 1) * hb, nhb - 1), 0)
    blk = (1, dil, tq, DIL_OUT)
    halo = (1, dil, LANES, DIL_OUT)
    halves = DIL_OUT // LANES
    out_blk = (1, halves, tq * dil, LANES)
    out_map = lambda b, i: (b, 0, i, 0)
    o, lse = pl.pallas_call(
        functools.partial(_dil_kernel, tq=tq, dil=dil, sub_len=L),
        grid=(batch, nq),
        in_specs=[pl.BlockSpec(blk, cur),
                  pl.BlockSpec(halo, prev), pl.BlockSpec(blk, cur), pl.BlockSpec(halo, nxt),
                  pl.BlockSpec(halo, prev), pl.BlockSpec(blk, cur), pl.BlockSpec(halo, nxt)],
        out_specs=[pl.BlockSpec(out_blk, out_map), pl.BlockSpec(out_blk, out_map)],
        out_shape=[jax.ShapeDtypeStruct((batch, halves, seq, LANES), F32)] * 2,
        compiler_params=_cparams(("parallel", "parallel")), name=f"dilated_g{g}",
    )(qd, kd, kd, kd, vd, vd, vd)
    return o, lse


def _mla_kernel(q_ref, k_ref, v_ref, o_ref, m_sc, acc_sc, *, tk, nk, unroll):
    nc = tk // LANES
    m_sc[...] = jnp.full(m_sc.shape, NEG, F32)
    acc_sc[...] = jnp.zeros(acc_sc.shape, F32)

    def step(off, hh):
        sl = slice(hh * LANES, (hh + 1) * LANES)
        q = q_ref[0, :, sl]
        k = k_ref[0, pl.ds(off, tk), sl]
        v = v_ref[0, pl.ds(off, tk), sl]
        s = lax.dot_general(q, k, (((1,), (1,)), ((), ())), preferred_element_type=F32)
        cols = [s[:, c * LANES:(c + 1) * LANES] for c in range(nc)]
        m_old = m_sc[hh]
        m_new = jnp.maximum(m_old, jnp.max(functools.reduce(jnp.maximum, cols), axis=-1, keepdims=True))
        p = jnp.concatenate([jnp.exp2(c - m_new) for c in cols], axis=1).astype(BF16)
        acc_sc[hh] = acc_sc[hh] * jnp.exp2(m_old - m_new) + jnp.dot(p, v, preferred_element_type=F32)
        m_sc[hh] = m_new

    def body(j, carry):
        for u in range(unroll):
            off = pl.multiple_of((j * unroll + u) * tk, tk)
            for hh in range(2):
                step(off, hh)
        return carry

    lax.fori_loop(0, nk // unroll, body, 0)
    outs = []
    for hh in range(2):
        acc = acc_sc[hh]
        outs.append(acc * (1.0 / acc[:, MLA_V:MLA_V + 1]))
    lane = lax.broadcasted_iota(jnp.int32, (1, LANES), 1)
    o_ref[0] = jnp.where(lane < MLA_V, outs[0], pltpu.roll(outs[1], MLA_V, 1)).astype(o_ref.dtype)


def _mla(qm, km, vm, batch, seq, tq=1024, tk=2048, unroll=1):
    tq = min(seq, tq)
    tk = min(seq, tk)
    nk = seq // tk
    unroll = unroll if nk % unroll == 0 else 1
    q3 = qm.reshape(batch, seq, MLA_PAD)
    k3 = km.reshape(batch, seq, MLA_PAD)
    v3 = vm.reshape(batch, seq, MLA_PAD)
    resident = lambda b, h, i: (b, 0, h)
    kv_bytes = seq * 2 * LANES * 2
    kv_mode = dict(pipeline_mode=pl.Buffered(1)) if kv_bytes > MLA_KV_DOUBLE_BUFFER_MAX else {}
    o = pl.pallas_call(
        functools.partial(_mla_kernel, tk=tk, nk=nk, unroll=unroll),
        grid=(batch, MLA_HEADS // 2, seq // tq),
        in_specs=[pl.BlockSpec((1, tq, 2 * LANES), lambda b, h, i: (b, i, h)),
                  pl.BlockSpec((1, seq, 2 * LANES), resident, **kv_mode),
                  pl.BlockSpec((1, seq, 2 * LANES), resident, **kv_mode)],
        out_specs=pl.BlockSpec((1, tq, LANES), lambda b, h, i: (b, i, h)),
        out_shape=jax.ShapeDtypeStruct((batch, seq, MLA_OUT), BF16),
        scratch_shapes=[pltpu.VMEM((2, tq, LANES), F32), pltpu.VMEM((2, tq, LANES), F32)],
        compiler_params=_cparams(("parallel", "parallel", "arbitrary")), name="mla",
    )(q3, k3, v3)
    return o.reshape(batch * seq, MLA_OUT)


def _outproj_kernel(x_ref, o0_ref, l0_ref, o1_ref, l1_ref, o2_ref, l2_ref, ob_ref, gate_ref,
                    wpa_ref, wpb_ref, wo_ref, g2_ref, wrh_ref, wrl_ref, br_ref,
                    xm_ref, h2_ref, route_ref, cnt_ref, cnt_sc):
    def halves(ref):
        return jnp.concatenate([ref[0, hf] for hf in range(ref.shape[1])], axis=1)

    l0, l1, l2 = halves(l0_ref), halves(l1_ref), halves(l2_ref)
    m = jnp.maximum(jnp.maximum(l0, l1), l2)
    w0, w1, w2 = jnp.exp(l0 - m), jnp.exp(l1 - m), jnp.exp(l2 - m)
    oa = (w0 * halves(o0_ref) + w1 * halves(o1_ref) + w2 * halves(o2_ref)) / (w0 + w1 + w2)
    pa = jnp.dot(oa.astype(BF16), wpa_ref[...], preferred_element_type=F32)
    pb = jnp.dot(ob_ref[...], wpb_ref[...], preferred_element_type=F32)
    merged = gate_ref[:, :D_MODEL] * pa + gate_ref[:, D_MODEL:] * pb
    xm = x_ref[...] + jnp.dot(merged.astype(BF16), wo_ref[...], preferred_element_type=F32)
    xm_ref[...] = xm
    h2 = _rms(xm, g2_ref[...])
    h2_ref[...] = h2

    hi = h2.astype(BF16)
    lo = (h2 - hi.astype(F32)).astype(BF16)
    lg = (jnp.dot(hi, wrh_ref[...], preferred_element_type=F32)
          + jnp.dot(lo, wrh_ref[...], preferred_element_type=F32)
          + jnp.dot(hi, wrl_ref[...], preferred_element_type=F32)) + br_ref[...]
    lane = lax.broadcasted_iota(jnp.int32, lg.shape, 1)
    lanef = lane.astype(F32)
    big = float(ROUTE_LANES)
    gmask = (lane >= N_EXPERTS) & (lane < N_EXPERTS + N_GROUPS)
    gl = jnp.where(gmask, lg, NEG)
    gmax = jnp.max(gl, axis=-1, keepdims=True)
    gidx = jnp.min(jnp.where(gl == gmax, lanef, big), axis=-1, keepdims=True) - float(N_EXPERTS)
    p_grp = 1.0 / jnp.sum(jnp.where(gmask, jnp.exp(gl - gmax), 0.0), axis=-1, keepdims=True)
    emask = (lane // EXPERTS_PER_GROUP).astype(F32) == gidx
    el = jnp.where(emask, lg, NEG)
    m1 = jnp.max(el, axis=-1, keepdims=True)
    i1 = jnp.min(jnp.where(el == m1, lanef, big), axis=-1, keepdims=True)
    el2 = jnp.where(lanef == i1, NEG, el)
    m2 = jnp.max(el2, axis=-1, keepdims=True)
    i2 = jnp.min(jnp.where(el2 == m2, lanef, big), axis=-1, keepdims=True)
    r = jnp.exp(m2 - m1)
    wa = p_grp / (1.0 + r)
    wb = p_grp * r / (1.0 + r)
    @pl.when(pl.program_id(0) == 0)
    def _():
        cnt_sc[...] = jnp.zeros(cnt_sc.shape, F32)

    oh1 = lanef == i1
    oh2 = lanef == i2
    oh = jnp.where(oh1 | oh2, 1.0, 0.0)
    tm = lg.shape[0]
    earlier = (lax.broadcasted_iota(jnp.int32, (tm, tm), 0) > lax.broadcasted_iota(jnp.int32, (tm, tm), 1))
    before = jnp.dot(jnp.where(earlier, 1.0, 0.0).astype(BF16), oh.astype(BF16),
                     preferred_element_type=F32) + cnt_sc[0:1, :]
    r1 = jnp.sum(jnp.where(oh1, before, 0.0), axis=-1, keepdims=True)
    r2 = jnp.sum(jnp.where(oh2, before, 0.0), axis=-1, keepdims=True)
    cnt = cnt_sc[...] + jnp.sum(oh, axis=0, keepdims=True)
    cnt_sc[...] = cnt
    cnt_ref[...] = cnt
    route_ref[...] = jnp.where(lane == 0, i1, jnp.where(lane == 1, i2, jnp.where(lane == 2, wa,
                               jnp.where(lane == 3, wb, jnp.where(lane == 4, r1,
                                                                  jnp.where(lane == 5, r2, 0.0))))))


def _outproj(x, dil_outs, ob, gates, lw, seq, tm):
    T = x.shape[0]
    nt = seq // tm
    row = lambda i: (i, 0)
    const = lambda i: (0, 0)
    weights = [lw['wpa'], lw['wpb'], lw['wo'], lw['g2'], lw['wrh'], lw['wrl'], lw['br']]
    acts = [x] + list(dil_outs) + [ob, gates]
    dil_spec = pl.BlockSpec((1, DIL_OUT // LANES, tm, LANES), lambda i: (i // nt, 0, i % nt, 0))
    in_specs = [pl.BlockSpec((tm, D_MODEL), row)] + [dil_spec] * len(dil_outs) \
        + [pl.BlockSpec((tm, ob.shape[1]), row), pl.BlockSpec((tm, gates.shape[1]), row)] \
        + [pl.BlockSpec(a.shape, const, pipeline_mode=pl.Buffered(1)) for a in weights]
    return pl.pallas_call(
        _outproj_kernel, grid=(T // tm,), in_specs=in_specs,
        out_specs=[pl.BlockSpec((tm, D_MODEL), row), pl.BlockSpec((tm, D_MODEL), row),
                   pl.BlockSpec((tm, ROUTE_LANES), row), pl.BlockSpec((8, ROUTE_LANES), const)],
        out_shape=[jax.ShapeDtypeStruct((T, D_MODEL), F32), jax.ShapeDtypeStruct((T, D_MODEL), F32),
                   jax.ShapeDtypeStruct((T, ROUTE_LANES), F32), jax.ShapeDtypeStruct((8, ROUTE_LANES), F32)],
        scratch_shapes=[pltpu.VMEM((8, ROUTE_LANES), F32)],
        compiler_params=_cparams(("arbitrary",)), name="outproj_router",
    )(*acts, *weights)


GATHER_WINDOW = 32


def _gather_rows(src, idx):
    m = idx.shape[0]
    width = src.shape[1]
    info = plsc.get_sparse_core_info()
    n_workers = info.num_cores * info.num_subcores
    per_w = m // n_workers
    n_pairs = per_w // (2 * GATHER_WINDOW)
    assert n_pairs * 2 * GATHER_WINDOW * n_workers == m
    mesh = plsc.VectorSubcoreMesh(core_axis_name="core", subcore_axis_name="subcore")

    @functools.partial(
        pl.kernel, out_type=jax.ShapeDtypeStruct((m, width), src.dtype), mesh=mesh, name="gather_rows",
        scratch_types=[pltpu.VMEM((per_w,), jnp.int32),
                       pltpu.VMEM((2, GATHER_WINDOW, width), src.dtype),
                       pltpu.SemaphoreType.DMA((2,))])
    def gather(src_hbm, idx_hbm, out_hbm, idx_v, rows_v, sems):
        wid = lax.axis_index("subcore") * info.num_cores + lax.axis_index("core")
        base = wid * per_w
        pltpu.sync_copy(idx_hbm.at[pl.ds(base, per_w)], idx_v)

        def fetch(c, b):
            return pltpu.make_async_copy(src_hbm.at[idx_v.at[pl.ds(c * GATHER_WINDOW, GATHER_WINDOW)]],
                                         rows_v.at[b], sems.at[b])

        def flush(c, b):
            pltpu.sync_copy(rows_v.at[b], out_hbm.at[pl.ds(base + c * GATHER_WINDOW, GATHER_WINDOW)])

        fetch(0, 0).start()

        @pl.loop(0, n_pairs)
        def _(p):
            c = 2 * p
            fetch(c + 1, 1).start()
            fetch(c, 0).wait()
            flush(c, 0)

            @pl.when(p + 1 < n_pairs)
            def _():
                fetch(c + 2, 0).start()

            fetch(c + 1, 1).wait()
            flush(c + 1, 1)

    return gather(src, idx)


EXPERT_BLOCK = 512


def _expert_kernel(be_ref, bv_ref, bf_ref, bio_ref, xs_ref, wg_ref, wu_ref, wd_ref, y_ref,
                   wg_sc, wu_sc, wd_sc):
    i = pl.program_id(0)

    @pl.when(bf_ref[i] > 0)
    def _():
        wg_sc[...] = wg_ref[0].astype(BF16)
        wu_sc[...] = wu_ref[0].astype(BF16)
        wd_sc[...] = wd_ref[0].astype(BF16)

    @pl.when(bv_ref[i] > 0)
    def _():
        xb = xs_ref[...].astype(BF16)
        g = jnp.dot(xb, wg_sc[...], preferred_element_type=F32)
        u = jnp.dot(xb, wu_sc[...], preferred_element_type=F32)
        hb = (g * jax.nn.sigmoid(g) * u).astype(BF16)
        y_ref[...] = jnp.dot(hb, wd_sc[...], preferred_element_type=F32)


def _experts(xs, blk_expert, blk_valid, blk_first, blk_io, wg, wu, wd, layer):
    n_slots = xs.shape[0]
    nb = n_slots // EXPERT_BLOCK
    grid_spec = pltpu.PrefetchScalarGridSpec(
        num_scalar_prefetch=4, grid=(nb,),
        in_specs=[pl.BlockSpec((EXPERT_BLOCK, D_MODEL), lambda i, be, bv, bf, bio: (bio[i], 0)),
                  pl.BlockSpec((None, 1, D_MODEL, D_EXPERT), lambda i, be, bv, bf, bio: (layer, be[i], 0, 0)),
                  pl.BlockSpec((None, 1, D_MODEL, D_EXPERT), lambda i, be, bv, bf, bio: (layer, be[i], 0, 0)),
                  pl.BlockSpec((None, 1, D_EXPERT, D_MODEL), lambda i, be, bv, bf, bio: (layer, be[i], 0, 0))],
        out_specs=pl.BlockSpec((EXPERT_BLOCK, D_MODEL), lambda i, be, bv, bf, bio: (bio[i], 0)),
        scratch_shapes=[pltpu.VMEM((D_MODEL, D_EXPERT), BF16), pltpu.VMEM((D_MODEL, D_EXPERT), BF16),
                        pltpu.VMEM((D_EXPERT, D_MODEL), BF16)])
    return pl.pallas_call(
        _expert_kernel, grid_spec=grid_spec,
        out_shape=jax.ShapeDtypeStruct((n_slots, D_MODEL), F32),
        compiler_params=_cparams(("arbitrary",)), name="experts",
    )(blk_expert, blk_valid, blk_first, blk_io, xs, wg, wu, wd)


def _combine_kernel(x_ref, ya_ref, yb_ref, route_ref, g_ref, o_ref, *, final):
    route = route_ref[...]
    wa = route[:, 2:3]
    wb = route[:, 3:4]
    x = x_ref[...] + (ya_ref[...] * wa + yb_ref[...] * wb)
    if final:
        x = _rms(x, g_ref[...])
    o_ref[...] = x


def _combine(x, yg, route, g, final, tm):
    T = x.shape[0]
    nt = T // tm
    return pl.pallas_call(
        functools.partial(_combine_kernel, final=final), grid=(nt,),
        in_specs=[pl.BlockSpec((tm, D_MODEL), lambda i: (i, 0)),
                  pl.BlockSpec((tm, D_MODEL), lambda i: (i, 0)),
                  pl.BlockSpec((tm, D_MODEL), lambda i: (i + nt, 0)),
                  pl.BlockSpec((tm, ROUTE_LANES), lambda i: (i, 0)),
                  pl.BlockSpec((1, D_MODEL), lambda i: (0, 0))],
        out_specs=pl.BlockSpec((tm, D_MODEL), lambda i: (i, 0)),
        out_shape=jax.ShapeDtypeStruct((T, D_MODEL), F32),
        compiler_params=_cparams(("parallel",)), name="combine",
    )(x, yg, yg, route, g)


def _dispatch_plan(route, cnt, T):
    A = T * TOP_K
    expert = route[:, 0:TOP_K].astype(jnp.int32)
    rank = route[:, 4:4 + TOP_K].astype(jnp.int32)
    counts = cnt[0, :N_EXPERTS].astype(jnp.int32)
    padded = (counts + EXPERT_BLOCK - 1) // EXPERT_BLOCK * EXPERT_BLOCK
    pad_end = jnp.cumsum(padded)
    pad_start = pad_end - padded
    slot = pad_start[expert] + rank
    n_blocks = A // EXPERT_BLOCK + N_EXPERTS
    n_slots = n_blocks * EXPERT_BLOCK
    tok = jnp.broadcast_to(jnp.arange(T, dtype=jnp.int32)[:, None], (T, TOP_K))
    blk_start = jnp.arange(n_blocks, dtype=jnp.int32) * EXPERT_BLOCK
    blk_valid = (blk_start < pad_end[-1]).astype(jnp.int32)
    blk_io = jnp.minimum(jnp.arange(n_blocks, dtype=jnp.int32), jnp.sum(blk_valid) - 1)
    blk_expert = jnp.minimum(jnp.sum((blk_start[:, None] >= pad_end[None, :]).astype(jnp.int32), axis=1),
                             N_EXPERTS - 1)[blk_io]
    blk_first = jnp.concatenate([jnp.ones((1,), jnp.int32),
                                 (blk_expert[1:] != blk_expert[:-1]).astype(jnp.int32)])
    _, tok_sorted = lax.sort_key_val(slot.reshape(A), tok.reshape(A))
    shift = pad_start - (jnp.cumsum(counts) - counts)
    src = (blk_start - shift[blk_expert])[:, None] + jnp.arange(EXPERT_BLOCK, dtype=jnp.int32)[None, :]
    slot_tok = tok_sorted[src % A].reshape(n_slots)
    comb_idx = jnp.concatenate([slot[:, 0], slot[:, 1]])
    return slot_tok, blk_expert, blk_valid, blk_first, blk_io, comb_idx


def _rope_tables(seq, dim, period, first):
    half = dim // 2
    inv = ROPE_THETA ** (-jnp.arange(0, dim, 2, dtype=F32) / dim)
    ang = jnp.arange(seq, dtype=F32)[:, None] * inv[None, :]
    cos, sin = jnp.cos(ang), jnp.sin(ang)
    d = jnp.arange(LANES) % period - first
    in_a = (d >= 0) & (d < half)
    in_b = (d >= half) & (d < dim)
    idx = jnp.clip(jnp.where(in_b, d - half, d), 0, half - 1)
    c = jnp.where((in_a | in_b)[None, :], cos[:, idx], 1.0)
    sa = jnp.where(in_a[None, :], -sin[:, idx], 0.0)
    sb = jnp.where(in_b[None, :], sin[:, idx], 0.0)
    return c, sa, sb


def _pad_heads(w, width):
    k = w.shape[0]
    w = w.reshape(k, MLA_HEADS, width)
    return jnp.pad(w, ((0, 0), (0, 0), (0, MLA_SLOT - width))).reshape(k, MLA_PAD)


def _prep_layer(l, norm1_g, w_in, q_norm_g, w_uq, kv_norm_g, w_uk, w_uv, w_pa, w_pb, w_o,
                norm2_g, w_rg, b_rg, w_re, b_re, w_e_gate, w_e_up, w_e_down):
    w = w_in[l]
    o = 0
    parts = []
    for n in (DIL_WIDTH, DIL_WIDTH, DIL_WIDTH, Q_LORA, KV_LORA, MLA_ROPE, 2 * D_MODEL):
        parts.append(w[:, o:o + n])
        o += n
    wq, wk, wv, wcq, wckv, wkr, wg = parts
    wkr_pad = jnp.pad(wkr, ((0, 0), (MLA_NOPE, MLA_SLOT - MLA_NOPE - MLA_ROPE)))
    wr = jnp.pad(jnp.concatenate([w_re[l], w_rg[l]], axis=1),
                 ((0, 0), (0, ROUTE_LANES - N_EXPERTS - N_GROUPS)))
    wrh = wr.astype(BF16)
    wrl = (wr - wrh.astype(F32)).astype(BF16)
    br = jnp.pad(jnp.concatenate([b_re[l], b_rg[l]]), (0, ROUTE_LANES - N_EXPERTS - N_GROUPS))
    return dict(
        g1=norm1_g[l][None, :], wq=wq.astype(BF16), wk=wk.astype(BF16), wv=wv.astype(BF16),
        wcq=wcq.astype(BF16), wckv=wckv.astype(BF16), wkr=wkr_pad.astype(BF16), wg=wg.astype(BF16),
        qn=q_norm_g[l][None, :], kvn=kv_norm_g[l][None, :],
        wuq=_pad_heads(w_uq[l], MLA_QK).astype(BF16), wuk=_pad_heads(w_uk[l], MLA_NOPE).astype(BF16),
        wuv=_pad_heads(w_uv[l], MLA_V).astype(BF16),
        wpa=w_pa[l].astype(BF16), wpb=w_pb[l].astype(BF16), wo=w_o[l].astype(BF16),
        g2=norm2_g[l][None, :], wrh=wrh, wrl=wrl, br=br[None, :].astype(F32),
        weg=w_e_gate, weu=w_e_up, wed=w_e_down)


def _trunk(x3, layers, final_g, tm=512):
    batch, seq, _ = x3.shape
    T = batch * seq
    x = x3.reshape(T, D_MODEL)
    tabs = _rope_tables(seq, ROT_DIM, HEAD_DIM, 0) + _rope_tables(seq, MLA_ROPE, MLA_SLOT, MLA_NOPE)
    fg = final_g[None, :]
    for l, lw in enumerate(layers):
        outs = _inproj(x, lw, tabs, seq, tm)
        n_g = len(DIL_GROUPS)
        qd, kd, vd = outs[0:n_g], outs[n_g:2 * n_g], outs[2 * n_g:3 * n_g]
        qm, km, vm, gates = outs[3 * n_g:]
        dil = []
        for g, (_, d) in enumerate(DIL_GROUPS):
            dil.extend(_dilated_group(qd[g], kd[g], vd[g], batch, seq, g, d))
        ob = _mla(qm, km, vm, batch, seq)
        xm, h2, route, cnt = _outproj(x, dil, ob, gates, lw, seq, tm)
        slot_tok, blk_expert, blk_valid, blk_first, blk_io, comb_idx = _dispatch_plan(route, cnt, T)
        xs = _gather_rows(h2, slot_tok)
        ys = _experts(xs, blk_expert, blk_valid, blk_first, blk_io, lw['weg'], lw['weu'], lw['wed'], l)
        yg = _gather_rows(ys, comb_idx)
        x = _combine(xm, yg, route, fg, l == len(layers) - 1, tm)
    return x.reshape(batch, seq, D_MODEL)


def kernel(x_prompt, x_sample, norm1_g, w_in, q_norm_g, w_uq, kv_norm_g, w_uk, w_uv, w_pa, w_pb, w_o,
           norm2_g, w_rg, b_rg, w_re, b_re, w_e_gate, w_e_up, w_e_down, final_g):
    layers = [_prep_layer(l, norm1_g, w_in, q_norm_g, w_uq, kv_norm_g, w_uk, w_uv, w_pa, w_pb, w_o,
                          norm2_g, w_rg, b_rg, w_re, b_re, w_e_gate, w_e_up, w_e_down)
              for l in range(DEPTH)]
    return (_trunk(x_prompt, layers, final_g), _trunk(x_sample, layers, final_g))
```

```python
import functools

import jax
import jax.numpy as jnp
from jax import lax
from jax.experimental import pallas as pl
from jax.experimental.pallas import tpu as pltpu
from jax.experimental.pallas import tpu_sc as plsc

D_MODEL = 1024
DEPTH = 2
HEAD_DIM = 64
DIL_GROUPS = ((128, 1), (512, 4), (2048, 16))
DIL_HEADS_PER_GROUP = 4
DIL_WIDTH = 768
DIL_OUT = 256
DIL_SCALE = HEAD_DIM ** -0.5
ROT_DIM = 16
ROPE_THETA = 500000.0
DIL_HALF = 64

MLA_HEADS = 8
MLA_NOPE = 64
MLA_ROPE = 32
MLA_QK = 96
MLA_V = 64
MLA_OUT = 512
MLA_SCALE = MLA_QK ** -0.5
Q_LORA = 256
KV_LORA = 128

N_GROUPS = 8
EXPERTS_PER_GROUP = 8
N_EXPERTS = 64
TOP_K = 2
D_EXPERT = 512
EPS = 1e-6

LANES = 128
MLA_SLOT = LANES
MLA_PAD = MLA_HEADS * MLA_SLOT
ROUTE_LANES = LANES
VMEM_LIMIT = 56 * 1024 * 1024
MLA_KV_DOUBLE_BUFFER_MAX = 2 * 1024 * 1024

BF16 = jnp.bfloat16
F32 = jnp.float32
NEG = -1e30
LOG2E = 1.4426950408889634


def _cparams(sem):
    return pltpu.CompilerParams(dimension_semantics=sem, vmem_limit_bytes=VMEM_LIMIT)


def _rms(t, g):
    return t * lax.rsqrt(jnp.mean(t * t, axis=-1, keepdims=True) + EPS) * g


def _rope_chunk(t, c, sa, sb, shift):
    return t * c + pltpu.roll(t, LANES - shift, 1) * sa + pltpu.roll(t, shift, 1) * sb


def _inproj_kernel(x_ref, g1_ref, wq_ref, wk_ref, wv_ref, wcq_ref, wckv_ref, wkr_ref, wg_ref,
                   qn_ref, kvn_ref, wuq_ref, wuk_ref, wuv_ref,
                   ca_ref, saa_ref, sab_ref, cb_ref, sba_ref, sbb_ref,
                   q0_ref, q1_ref, q2_ref, k0_ref, k1_ref, k2_ref, v0_ref, v1_ref, v2_ref,
                   qm_ref, km_ref, vm_ref, gate_ref, q_sc, k_sc, v_sc):
    x = x_ref[...]
    h = _rms(x, g1_ref[...]).astype(BF16)
    tm = x.shape[0]

    ca, saa, sab = ca_ref[...], saa_ref[...], sab_ref[...]
    q = jnp.dot(h, wq_ref[...], preferred_element_type=F32)
    k = jnp.dot(h, wk_ref[...], preferred_element_type=F32)
    v = jnp.dot(h, wv_ref[...], preferred_element_type=F32)
    for j in range(DIL_WIDTH // LANES):
        sl = slice(j * LANES, (j + 1) * LANES)
        q_sc[j] = _rope_chunk(q[:, sl], ca, saa, sab, ROT_DIM // 2) * DIL_SCALE
        k_sc[j] = _rope_chunk(k[:, sl], ca, saa, sab, ROT_DIM // 2)
        v_sc[j] = v[:, sl]
    halves = DIL_OUT // LANES
    for src, outs in ((q_sc, (q0_ref, q1_ref, q2_ref)), (k_sc, (k0_ref, k1_ref, k2_ref)),
                      (v_sc, (v0_ref, v1_ref, v2_ref))):
        for g, (_, dil) in enumerate(DIL_GROUPS):
            for hf in range(halves):
                for r in range(dil):
                    rows = pl.ds(r, tm // dil, stride=dil) if dil > 1 else slice(None)
                    outs[g][0, r, :, hf * LANES:(hf + 1) * LANES] = src[g * halves + hf, rows, :].astype(BF16)

    cb, sba, sbb = cb_ref[...], sba_ref[...], sbb_ref[...]
    cq = jnp.dot(h, wcq_ref[...], preferred_element_type=F32)
    cqn = _rms(cq, qn_ref[...]).astype(BF16)
    qm = jnp.dot(cqn, wuq_ref[...], preferred_element_type=F32) * (MLA_SCALE * LOG2E)
    ckv = jnp.dot(h, wckv_ref[...], preferred_element_type=F32)
    c = _rms(ckv, kvn_ref[...]).astype(BF16)
    kn = jnp.dot(c, wuk_ref[...], preferred_element_type=F32)
    vv = jnp.dot(c, wuv_ref[...], preferred_element_type=F32)
    kr = jnp.dot(h, wkr_ref[...], preferred_element_type=F32)
    kr = _rope_chunk(kr, cb, sba, sbb, MLA_ROPE // 2)
    lane = lax.broadcasted_iota(jnp.int32, (1, LANES), 1)
    ones_col = jnp.where(lane == MLA_V, 1.0, 0.0).astype(F32)
    for j in range(MLA_HEADS):
        sl = slice(j * LANES, (j + 1) * LANES)
        qm_ref[:, sl] = _rope_chunk(qm[:, sl], cb, sba, sbb, MLA_ROPE // 2).astype(BF16)
        km_ref[:, sl] = (kn[:, sl] + kr).astype(BF16)
        vm_ref[:, sl] = (vv[:, sl] + ones_col).astype(BF16)

    gate_ref[...] = jax.nn.sigmoid(jnp.dot(h, wg_ref[...], preferred_element_type=F32))


def _inproj(x, lw, tabs, seq, tm):
    T = x.shape[0]
    nt = seq // tm
    row = lambda i: (i, 0)
    const = lambda i: (0, 0)
    tab = lambda i: (i % nt, 0)

    def wspec(a):
        return pl.BlockSpec(a.shape, const, pipeline_mode=pl.Buffered(1))

    weights = [lw['g1'], lw['wq'], lw['wk'], lw['wv'], lw['wcq'], lw['wckv'], lw['wkr'], lw['wg'],
               lw['qn'], lw['kvn'], lw['wuq'], lw['wuk'], lw['wuv']]
    in_specs = ([pl.BlockSpec((tm, D_MODEL), row)] + [wspec(a) for a in weights]
                + [pl.BlockSpec((tm, LANES), tab)] * 6)
    batch = T // seq
    res_map = lambda i: (i // nt, 0, i % nt, 0)
    dil_shape = [jax.ShapeDtypeStruct((batch, d, seq // d, DIL_OUT), BF16) for _, d in DIL_GROUPS] * 3
    dil_specs = [pl.BlockSpec((1, d, tm // d, DIL_OUT), res_map) for _, d in DIL_GROUPS] * 3
    out_shape = dil_shape + [jax.ShapeDtypeStruct((T, MLA_PAD), BF16)] * 3 \
        + [jax.ShapeDtypeStruct((T, 2 * D_MODEL), F32)]
    out_specs = dil_specs + [pl.BlockSpec((tm, MLA_PAD), row)] * 3 + [pl.BlockSpec((tm, 2 * D_MODEL), row)]
    return pl.pallas_call(
        _inproj_kernel, grid=(T // tm,), in_specs=in_specs, out_specs=out_specs, out_shape=out_shape,
        scratch_shapes=[pltpu.VMEM((DIL_WIDTH // LANES, tm, LANES), F32)] * 3,
        compiler_params=_cparams(("parallel",)), name="inproj",
    )(x, *weights, *tabs)


DIL_STEP_TOKENS = 2048


def _dil_kernel(q_ref, kp_ref, kc_ref, kn_ref, vp_ref, vc_ref, vn_ref, o_ref, lse_ref, *, tq, dil, sub_len):
    i = pl.program_id(1)
    qb = LANES
    nh = DIL_HEADS_PER_GROUP
    head = lax.broadcasted_iota(jnp.int32, (1, DIL_OUT), 1) // HEAD_DIM
    qloc = lax.broadcasted_iota(jnp.int32, (nh * qb, 2 * qb), 0) % qb
    rel = lax.broadcasted_iota(jnp.int32, (nh * qb, 2 * qb), 1) - DIL_HALF - qloc
    band = jnp.abs(rel) <= DIL_HALF
    kcol = lax.broadcasted_iota(jnp.int32, (1, 2 * qb), 1) - DIL_HALF

    def window(p_ref, c_ref, n_ref, r, j):
        lo, hi = j * qb - DIL_HALF, j * qb + 2 * qb - DIL_HALF
        parts = []
        if lo < 0:
            parts.append(p_ref[0, r, qb + lo:qb, :])
            lo = 0
        parts.append(c_ref[0, r, lo:min(hi, tq), :])
        if hi > tq:
            parts.append(n_ref[0, r, 0:hi - tq, :])
        return jnp.concatenate(parts, axis=0) if len(parts) > 1 else parts[0]

    for r in range(dil):
        for j in range(tq // qb):
            q = q_ref[0, r, j * qb:(j + 1) * qb, :]
            k = window(kp_ref, kc_ref, kn_ref, r, j)
            v = window(vp_ref, vc_ref, vn_ref, r, j)
            qs = jnp.concatenate([jnp.where(head == hd, q, jnp.zeros_like(q)) for hd in range(nh)], axis=0)
            s = lax.dot_general(qs, k, (((1,), (1,)), ((), ())), preferred_element_type=F32)
            kpos = kcol + (i * tq + j * qb)
            s = jnp.where(band & (kpos >= 0) & (kpos < sub_len), s, NEG)
            m = jnp.max(s, axis=-1, keepdims=True)
            p = jnp.exp(s - m)
            den = jnp.sum(p, axis=-1, keepdims=True)
            oh = jnp.dot(p.astype(BF16), v, preferred_element_type=F32) * (1.0 / den)
            lse = m + jnp.log(den)
            o_acc = oh[0:qb]
            l_acc = jnp.broadcast_to(lse[0:qb], (qb, DIL_OUT))
            for hd in range(1, nh):
                o_acc = jnp.where(head == hd, oh[hd * qb:(hd + 1) * qb], o_acc)
                l_acc = jnp.where(head == hd, lse[hd * qb:(hd + 1) * qb], l_acc)
            rows = pl.ds(j * qb * dil + r, qb, stride=dil) if dil > 1 else slice(j * qb, (j + 1) * qb)
            for hf in range(DIL_OUT // LANES):
                o_ref[0, hf, rows, :] = o_acc[:, hf * LANES:(hf + 1) * LANES]
                lse_ref[0, hf, rows, :] = l_acc[:, hf * LANES:(hf + 1) * LANES]


def _dilated_group(qd, kd, vd, batch, seq, g, dil):
    L = seq // dil
    tq = min(L, DIL_STEP_TOKENS // dil)
    nq = L // tq
    hb = tq // LANES
    nhb = L // LANES
    cur = lambda b, i: (b, 0, i, 0)
    prev = lambda b, i: (b, 0, jnp.maximum(i * hb - 1, 0), 0)
    nxt = lambda b, i: (b, 0, jnp.minimum((i + 1) * hb, nhb - 1), 0)
    blk = (1, dil, tq, DIL_OUT)
    halo = (1, dil, LANES, DIL_OUT)
    halves = DIL_OUT // LANES
    out_blk = (1, halves, tq * dil, LANES)
    out_map = lambda b, i: (b, 0, i, 0)
    o, lse = pl.pallas_call(
        functools.partial(_dil_kernel, tq=tq, dil=dil, sub_len=L),
        grid=(batch, nq),
        in_specs=[pl.BlockSpec(blk, cur),
                  pl.BlockSpec(halo, prev), pl.BlockSpec(blk, cur), pl.BlockSpec(halo, nxt),
                  pl.BlockSpec(halo, prev), pl.BlockSpec(blk, cur), pl.BlockSpec(halo, nxt)],
        out_specs=[pl.BlockSpec(out_blk, out_map), pl.BlockSpec(out_blk, out_map)],
        out_shape=[jax.ShapeDtypeStruct((batch, halves, seq, LANES), F32)] * 2,
        compiler_params=_cparams(("parallel", "parallel")), name=f"dilated_g{g}",
    )(qd, kd, kd, kd, vd, vd, vd)
    return o, lse


def _mla_kernel(*refs, tk, nk, unroll):
    q_ref, k_ref, v_ref = refs[:3]
    o_ref, m_sc, acc_sc = refs[-3:]
    nc = tk // LANES
    m_sc[...] = jnp.full(m_sc.shape, NEG, F32)
    acc_sc[...] = jnp.zeros(acc_sc.shape, F32)

    def step(off, hh):
        sl = slice(hh * LANES, (hh + 1) * LANES)
        q = q_ref[0, :, sl]
        k = k_ref[0, pl.ds(off, tk), sl]
        v = v_ref[0, pl.ds(off, tk), sl]
        s = lax.dot_general(q, k, (((1,), (1,)), ((), ())), preferred_element_type=F32)
        cols = [s[:, c * LANES:(c + 1) * LANES] for c in range(nc)]
        m_old = m_sc[hh]
        m_new = jnp.maximum(m_old, jnp.max(functools.reduce(jnp.maximum, cols), axis=-1, keepdims=True))
        p = jnp.concatenate([jnp.exp2(c - m_new) for c in cols], axis=1).astype(BF16)
        acc_sc[hh] = acc_sc[hh] * jnp.exp2(m_old - m_new) + jnp.dot(p, v, preferred_element_type=F32)
        m_sc[hh] = m_new

    def body(j, carry):
        for u in range(unroll):
            off = pl.multiple_of((j * unroll + u) * tk, tk)
            for hh in range(2):
                step(off, hh)
        return carry

    lax.fori_loop(0, nk // unroll, body, 0)
    outs = []
    for hh in range(2):
        acc = acc_sc[hh]
        outs.append(acc * (1.0 / acc[:, MLA_V:MLA_V + 1]))
    lane = lax.broadcasted_iota(jnp.int32, (1, LANES), 1)
    o_ref[0] = jnp.where(lane < MLA_V, outs[0], pltpu.roll(outs[1], MLA_V, 1)).astype(o_ref.dtype)


def _mla(qm, km, vm, batch, seq, after=None, tq=1024, tk=2048, unroll=1):
    tq = min(seq, tq)
    tk = min(seq, tk)
    nk = seq // tk
    unroll = unroll if nk % unroll == 0 else 1
    q3 = qm.reshape(batch, seq, MLA_PAD)
    k3 = km.reshape(batch, seq, MLA_PAD)
    v3 = vm.reshape(batch, seq, MLA_PAD)
    resident = lambda b, h, i: (b, 0, h)
    kv_bytes = seq * 2 * LANES * 2
    kv_mode = dict(pipeline_mode=pl.Buffered(1)) if kv_bytes > MLA_KV_DOUBLE_BUFFER_MAX else {}
    extra = [] if after is None else [after]
    o = pl.pallas_call(
        functools.partial(_mla_kernel, tk=tk, nk=nk, unroll=unroll),
        grid=(batch, MLA_HEADS // 2, seq // tq),
        in_specs=[pl.BlockSpec((1, tq, 2 * LANES), lambda b, h, i: (b, i, h)),
                  pl.BlockSpec((1, seq, 2 * LANES), resident, **kv_mode),
                  pl.BlockSpec((1, seq, 2 * LANES), resident, **kv_mode)]
        + [pl.BlockSpec(memory_space=pl.ANY)] * len(extra),
        out_specs=pl.BlockSpec((1, tq, LANES), lambda b, h, i: (b, i, h)),
        out_shape=jax.ShapeDtypeStruct((batch, seq, MLA_OUT), BF16),
        scratch_shapes=[pltpu.VMEM((2, tq, LANES), F32), pltpu.VMEM((2, tq, LANES), F32)],
        compiler_params=_cparams(("parallel", "parallel", "arbitrary")), name="mla",
    )(q3, k3, v3, *extra)
    return o.reshape(batch * seq, MLA_OUT)


def _outproj_kernel(x_ref, o0_ref, l0_ref, o1_ref, l1_ref, o2_ref, l2_ref, ob_ref, gate_ref,
                    wpa_ref, wpb_ref, wo_ref, g2_ref, wrh_ref, wrl_ref, br_ref,
                    xm_ref, h2_ref, route_ref, cnt_ref, cnt_sc):
    def halves(ref):
        return jnp.concatenate([ref[0, hf] for hf in range(ref.shape[1])], axis=1)

    l0, l1, l2 = halves(l0_ref), halves(l1_ref), halves(l2_ref)
    m = jnp.maximum(jnp.maximum(l0, l1), l2)
    w0, w1, w2 = jnp.exp(l0 - m), jnp.exp(l1 - m), jnp.exp(l2 - m)
    oa = (w0 * halves(o0_ref) + w1 * halves(o1_ref) + w2 * halves(o2_ref)) / (w0 + w1 + w2)
    pa = jnp.dot(oa.astype(BF16), wpa_ref[...], preferred_element_type=F32)
    pb = jnp.dot(ob_ref[...], wpb_ref[...], preferred_element_type=F32)
    merged = gate_ref[:, :D_MODEL] * pa + gate_ref[:, D_MODEL:] * pb
    xm = x_ref[...] + jnp.dot(merged.astype(BF16), wo_ref[...], preferred_element_type=F32)
    xm_ref[...] = xm
    h2 = _rms(xm, g2_ref[...])
    h2_ref[...] = h2

    hi = h2.astype(BF16)
    lo = (h2 - hi.astype(F32)).astype(BF16)
    lg = (jnp.dot(hi, wrh_ref[...], preferred_element_type=F32)
          + jnp.dot(lo, wrh_ref[...], preferred_element_type=F32)
          + jnp.dot(hi, wrl_ref[...], preferred_element_type=F32)) + br_ref[...]
    lane = lax.broadcasted_iota(jnp.int32, lg.shape, 1)
    lanef = lane.astype(F32)
    big = float(ROUTE_LANES)
    gmask = (lane >= N_EXPERTS) & (lane < N_EXPERTS + N_GROUPS)
    gl = jnp.where(gmask, lg, NEG)
    gmax = jnp.max(gl, axis=-1, keepdims=True)
    gidx = jnp.min(jnp.where(gl == gmax, lanef, big), axis=-1, keepdims=True) - float(N_EXPERTS)
    p_grp = 1.0 / jnp.sum(jnp.where(gmask, jnp.exp(gl - gmax), 0.0), axis=-1, keepdims=True)
    emask = (lane // EXPERTS_PER_GROUP).astype(F32) == gidx
    el = jnp.where(emask, lg, NEG)
    m1 = jnp.max(el, axis=-1, keepdims=True)
    i1 = jnp.min(jnp.where(el == m1, lanef, big), axis=-1, keepdims=True)
    el2 = jnp.where(lanef == i1, NEG, el)
    m2 = jnp.max(el2, axis=-1, keepdims=True)
    i2 = jnp.min(jnp.where(el2 == m2, lanef, big), axis=-1, keepdims=True)
    r = jnp.exp(m2 - m1)
    wa = p_grp / (1.0 + r)
    wb = p_grp * r / (1.0 + r)
    @pl.when(pl.program_id(0) == 0)
    def _():
        cnt_sc[...] = jnp.zeros(cnt_sc.shape, F32)

    oh1 = lanef == i1
    oh2 = lanef == i2
    oh = jnp.where(oh1 | oh2, 1.0, 0.0)
    tm = lg.shape[0]
    earlier = (lax.broadcasted_iota(jnp.int32, (tm, tm), 0) > lax.broadcasted_iota(jnp.int32, (tm, tm), 1))
    before = jnp.dot(jnp.where(earlier, 1.0, 0.0).astype(BF16), oh.astype(BF16),
                     preferred_element_type=F32) + cnt_sc[0:1, :]
    r1 = jnp.sum(jnp.where(oh1, before, 0.0), axis=-1, keepdims=True)
    r2 = jnp.sum(jnp.where(oh2, before, 0.0), axis=-1, keepdims=True)
    cnt = cnt_sc[...] + jnp.sum(oh, axis=0, keepdims=True)
    cnt_sc[...] = cnt
    cnt_ref[...] = cnt
    route_ref[...] = jnp.where(lane == 0, i1, jnp.where(lane == 1, i2, jnp.where(lane == 2, wa,
                               jnp.where(lane == 3, wb, jnp.where(lane == 4, r1,
                                                                  jnp.where(lane == 5, r2, 0.0))))))


def _outproj(x, dil_outs, ob, gates, lw, seq, tm):
    T = x.shape[0]
    nt = seq // tm
    row = lambda i: (i, 0)
    const = lambda i: (0, 0)
    weights = [lw['wpa'], lw['wpb'], lw['wo'], lw['g2'], lw['wrh'], lw['wrl'], lw['br']]
    acts = [x] + list(dil_outs) + [ob, gates]
    dil_spec = pl.BlockSpec((1, DIL_OUT // LANES, tm, LANES), lambda i: (i // nt, 0, i % nt, 0))
    in_specs = [pl.BlockSpec((tm, D_MODEL), row)] + [dil_spec] * len(dil_outs) \
        + [pl.BlockSpec((tm, ob.shape[1]), row), pl.BlockSpec((tm, gates.shape[1]), row)] \
        + [pl.BlockSpec(a.shape, const, pipeline_mode=pl.Buffered(1)) for a in weights]
    return pl.pallas_call(
        _outproj_kernel, grid=(T // tm,), in_specs=in_specs,
        out_specs=[pl.BlockSpec((tm, D_MODEL), row), pl.BlockSpec((tm, D_MODEL), row),
                   pl.BlockSpec((tm, ROUTE_LANES), row), pl.BlockSpec((8, ROUTE_LANES), const)],
        out_shape=[jax.ShapeDtypeStruct((T, D_MODEL), F32), jax.ShapeDtypeStruct((T, D_MODEL), F32),
                   jax.ShapeDtypeStruct((T, ROUTE_LANES), F32), jax.ShapeDtypeStruct((8, ROUTE_LANES), F32)],
        scratch_shapes=[pltpu.VMEM((8, ROUTE_LANES), F32)],
        compiler_params=_cparams(("arbitrary",)), name="outproj_router",
    )(*acts, *weights)


GATHER_WINDOW = 32


def _gather_rows(src, idx):
    m = idx.shape[0]
    width = src.shape[1]
    info = plsc.get_sparse_core_info()
    n_workers = info.num_cores * info.num_subcores
    per_w = m // n_workers
    n_pairs = per_w // (2 * GATHER_WINDOW)
    assert n_pairs * 2 * GATHER_WINDOW * n_workers == m
    mesh = plsc.VectorSubcoreMesh(core_axis_name="core", subcore_axis_name="subcore")

    @functools.partial(
        pl.kernel, out_type=jax.ShapeDtypeStruct((m, width), src.dtype), mesh=mesh, name="gather_rows",
        scratch_types=[pltpu.VMEM((per_w,), jnp.int32),
                       pltpu.VMEM((2, GATHER_WINDOW, width), src.dtype),
                       pltpu.SemaphoreType.DMA((2,))])
    def gather(src_hbm, idx_hbm, out_hbm, idx_v, rows_v, sems):
        wid = lax.axis_index("subcore") * info.num_cores + lax.axis_index("core")
        base = wid * per_w
        pltpu.sync_copy(idx_hbm.at[pl.ds(base, per_w)], idx_v)

        def fetch(c, b):
            return pltpu.make_async_copy(src_hbm.at[idx_v.at[pl.ds(c * GATHER_WINDOW, GATHER_WINDOW)]],
                                         rows_v.at[b], sems.at[b])

        def flush(c, b):
            pltpu.sync_copy(rows_v.at[b], out_hbm.at[pl.ds(base + c * GATHER_WINDOW, GATHER_WINDOW)])

        fetch(0, 0).start()

        @pl.loop(0, n_pairs)
        def _(p):
            c = 2 * p
            fetch(c + 1, 1).start()
            fetch(c, 0).wait()
            flush(c, 0)

            @pl.when(p + 1 < n_pairs)
            def _():
                fetch(c + 2, 0).start()

            fetch(c + 1, 1).wait()
            flush(c + 1, 1)

    return gather(src, idx)


EXPERT_BLOCK = 512


def _expert_kernel(be_ref, bv_ref, bf_ref, bio_ref, xs_ref, wg_ref, wu_ref, wd_ref, y_ref,
                   wg_sc, wu_sc, wd_sc):
    i = pl.program_id(0)

    @pl.when(bf_ref[i] > 0)
    def _():
        wg_sc[...] = wg_ref[0].astype(BF16)
        wu_sc[...] = wu_ref[0].astype(BF16)
        wd_sc[...] = wd_ref[0].astype(BF16)

    @pl.when(bv_ref[i] > 0)
    def _():
        xb = xs_ref[...].astype(BF16)
        g = jnp.dot(xb, wg_sc[...], preferred_element_type=F32)
        u = jnp.dot(xb, wu_sc[...], preferred_element_type=F32)
        hb = (g * jax.nn.sigmoid(g) * u).astype(BF16)
        y_ref[...] = jnp.dot(hb, wd_sc[...], preferred_element_type=F32)


def _experts(xs, blk_expert, blk_valid, blk_first, blk_io, wg, wu, wd, layer):
    n_slots = xs.shape[0]
    nb = n_slots // EXPERT_BLOCK
    grid_spec = pltpu.PrefetchScalarGridSpec(
        num_scalar_prefetch=4, grid=(nb,),
        in_specs=[pl.BlockSpec((EXPERT_BLOCK, D_MODEL), lambda i, be, bv, bf, bio: (bio[i], 0)),
                  pl.BlockSpec((None, 1, D_MODEL, D_EXPERT), lambda i, be, bv, bf, bio: (layer, be[i], 0, 0)),
                  pl.BlockSpec((None, 1, D_MODEL, D_EXPERT), lambda i, be, bv, bf, bio: (layer, be[i], 0, 0)),
                  pl.BlockSpec((None, 1, D_EXPERT, D_MODEL), lambda i, be, bv, bf, bio: (layer, be[i], 0, 0))],
        out_specs=pl.BlockSpec((EXPERT_BLOCK, D_MODEL), lambda i, be, bv, bf, bio: (bio[i], 0)),
        scratch_shapes=[pltpu.VMEM((D_MODEL, D_EXPERT), BF16), pltpu.VMEM((D_MODEL, D_EXPERT), BF16),
                        pltpu.VMEM((D_EXPERT, D_MODEL), BF16)])
    return pl.pallas_call(
        _expert_kernel, grid_spec=grid_spec,
        out_shape=jax.ShapeDtypeStruct((n_slots, D_MODEL), F32),
        compiler_params=_cparams(("arbitrary",)), name="experts",
    )(blk_expert, blk_valid, blk_first, blk_io, xs, wg, wu, wd)


def _combine_kernel(x_ref, ya_ref, yb_ref, route_ref, g_ref, o_ref, *, final):
    route = route_ref[...]
    wa = route[:, 2:3]
    wb = route[:, 3:4]
    x = x_ref[...] + (ya_ref[...] * wa + yb_ref[...] * wb)
    if final:
        x = _rms(x, g_ref[...])
    o_ref[...] = x


def _combine(x, yg, route, g, final, tm):
    T = x.shape[0]
    nt = T // tm
    return pl.pallas_call(
        functools.partial(_combine_kernel, final=final), grid=(nt,),
        in_specs=[pl.BlockSpec((tm, D_MODEL), lambda i: (i, 0)),
                  pl.BlockSpec((tm, D_MODEL), lambda i: (i, 0)),
                  pl.BlockSpec((tm, D_MODEL), lambda i: (i + nt, 0)),
                  pl.BlockSpec((tm, ROUTE_LANES), lambda i: (i, 0)),
                  pl.BlockSpec((1, D_MODEL), lambda i: (0, 0))],
        out_specs=pl.BlockSpec((tm, D_MODEL), lambda i: (i, 0)),
        out_shape=jax.ShapeDtypeStruct((T, D_MODEL), F32),
        compiler_params=_cparams(("parallel",)), name="combine",
    )(x, yg, yg, route, g)


def _slots_kernel(route_ref, start_ref, slot_ref):
    route = route_ref[...]
    lanef = lax.broadcasted_iota(jnp.int32, route.shape, 1).astype(F32)
    lane = lax.broadcasted_iota(jnp.int32, route.shape, 1)
    start = start_ref[...]

    def slot(k):
        first = jnp.sum(jnp.where(lanef == route[:, k:k + 1], start, 0.0), axis=-1, keepdims=True)
        return first + route[:, 4 + k:5 + k]

    slab = jnp.where(lane == 0, slot(0), jnp.where(lane == 1, slot(1), 0.0))
    slot_ref[...] = jnp.transpose(slab)[0:8, :]


def _slots(route, pad_start, tm):
    T = route.shape[0]
    start = jnp.pad(pad_start.astype(F32), (0, ROUTE_LANES - N_EXPERTS))[None, :]
    out = pl.pallas_call(
        _slots_kernel, grid=(T // tm,),
        in_specs=[pl.BlockSpec((tm, ROUTE_LANES), lambda i: (i, 0)),
                  pl.BlockSpec((1, ROUTE_LANES), lambda i: (0, 0))],
        out_specs=pl.BlockSpec((8, tm), lambda i: (0, i)),
        out_shape=jax.ShapeDtypeStruct((8, T), F32),
        compiler_params=_cparams(("parallel",)), name="slots",
    )(route, start)
    return out[0:TOP_K].astype(jnp.int32)


def _dispatch_plan(route, cnt, T, tm):
    A = T * TOP_K
    counts = cnt[0, :N_EXPERTS].astype(jnp.int32)
    padded = (counts + EXPERT_BLOCK - 1) // EXPERT_BLOCK * EXPERT_BLOCK
    pad_end = jnp.cumsum(padded)
    pad_start = pad_end - padded
    slot = _slots(route, pad_start, tm)
    n_blocks = A // EXPERT_BLOCK + N_EXPERTS
    n_slots = n_blocks * EXPERT_BLOCK
    tok = jnp.broadcast_to(jnp.arange(T, dtype=jnp.int32)[None, :], (TOP_K, T))
    blk_start = jnp.arange(n_blocks, dtype=jnp.int32) * EXPERT_BLOCK
    blk_valid = (blk_start < pad_end[-1]).astype(jnp.int32)
    blk_io = jnp.minimum(jnp.arange(n_blocks, dtype=jnp.int32), jnp.sum(blk_valid) - 1)
    blk_expert = jnp.minimum(jnp.sum((blk_start[:, None] >= pad_end[None, :]).astype(jnp.int32), axis=1),
                             N_EXPERTS - 1)[blk_io]
    blk_first = jnp.concatenate([jnp.ones((1,), jnp.int32),
                                 (blk_expert[1:] != blk_expert[:-1]).astype(jnp.int32)])
    _, tok_sorted = lax.sort_key_val(slot.reshape(A), tok.reshape(A))
    shift = pad_start - (jnp.cumsum(counts) - counts)
    src = (blk_start - shift[blk_expert])[:, None] + jnp.arange(EXPERT_BLOCK, dtype=jnp.int32)[None, :]
    slot_tok = tok_sorted[src % A].reshape(n_slots)
    comb_idx = slot.reshape(A)
    return slot_tok, blk_expert, blk_valid, blk_first, blk_io, comb_idx


def _rope_tables(seq, dim, period, first):
    half = dim // 2
    inv = ROPE_THETA ** (-jnp.arange(0, dim, 2, dtype=F32) / dim)
    ang = jnp.arange(seq, dtype=F32)[:, None] * inv[None, :]
    cos, sin = jnp.cos(ang), jnp.sin(ang)
    d = jnp.arange(LANES) % period - first
    in_a = (d >= 0) & (d < half)
    in_b = (d >= half) & (d < dim)
    idx = jnp.clip(jnp.where(in_b, d - half, d), 0, half - 1)
    c = jnp.where((in_a | in_b)[None, :], cos[:, idx], 1.0)
    sa = jnp.where(in_a[None, :], -sin[:, idx], 0.0)
    sb = jnp.where(in_b[None, :], sin[:, idx], 0.0)
    return c, sa, sb


def _pad_heads(w, width):
    k = w.shape[0]
    w = w.reshape(k, MLA_HEADS, width)
    return jnp.pad(w, ((0, 0), (0, 0), (0, MLA_SLOT - width))).reshape(k, MLA_PAD)


def _prep_layer(l, norm1_g, w_in, q_norm_g, w_uq, kv_norm_g, w_uk, w_uv, w_pa, w_pb, w_o,
                norm2_g, w_rg, b_rg, w_re, b_re, w_e_gate, w_e_up, w_e_down):
    w = w_in[l]
    o = 0
    parts = []
    for n in (DIL_WIDTH, DIL_WIDTH, DIL_WIDTH, Q_LORA, KV_LORA, MLA_ROPE, 2 * D_MODEL):
        parts.append(w[:, o:o + n])
        o += n
    wq, wk, wv, wcq, wckv, wkr, wg = parts
    wkr_pad = jnp.pad(wkr, ((0, 0), (MLA_NOPE, MLA_SLOT - MLA_NOPE - MLA_ROPE)))
    wr = jnp.pad(jnp.concatenate([w_re[l], w_rg[l]], axis=1),
                 ((0, 0), (0, ROUTE_LANES - N_EXPERTS - N_GROUPS)))
    wrh = wr.astype(BF16)
    wrl = (wr - wrh.astype(F32)).astype(BF16)
    br = jnp.pad(jnp.concatenate([b_re[l], b_rg[l]]), (0, ROUTE_LANES - N_EXPERTS - N_GROUPS))
    return dict(
        g1=norm1_g[l][None, :], wq=wq.astype(BF16), wk=wk.astype(BF16), wv=wv.astype(BF16),
        wcq=wcq.astype(BF16), wckv=wckv.astype(BF16), wkr=wkr_pad.astype(BF16), wg=wg.astype(BF16),
        qn=q_norm_g[l][None, :], kvn=kv_norm_g[l][None, :],
        wuq=_pad_heads(w_uq[l], MLA_QK).astype(BF16), wuk=_pad_heads(w_uk[l], MLA_NOPE).astype(BF16),
        wuv=_pad_heads(w_uv[l], MLA_V).astype(BF16),
        wpa=w_pa[l].astype(BF16), wpb=w_pb[l].astype(BF16), wo=w_o[l].astype(BF16),
        g2=norm2_g[l][None, :], wrh=wrh, wrl=wrl, br=br[None, :].astype(F32),
        weg=w_e_gate, weu=w_e_up, wed=w_e_down)


def _trunk(x3, layers, final_g, after=None, tm=512):
    batch, seq, _ = x3.shape
    T = batch * seq
    x = x3.reshape(T, D_MODEL)
    tabs = _rope_tables(seq, ROT_DIM, HEAD_DIM, 0) + _rope_tables(seq, MLA_ROPE, MLA_SLOT, MLA_NOPE)
    fg = final_g[None, :]
    dispatched = []
    for l, lw in enumerate(layers):
        outs = _inproj(x, lw, tabs, seq, tm)
        n_g = len(DIL_GROUPS)
        qd, kd, vd = outs[0:n_g], outs[n_g:2 * n_g], outs[2 * n_g:3 * n_g]
        qm, km, vm, gates = outs[3 * n_g:]
        dil = []
        for g, (_, d) in enumerate(DIL_GROUPS):
            dil.extend(_dilated_group(qd[g], kd[g], vd[g], batch, seq, g, d))
        ob = _mla(qm, km, vm, batch, seq, after=None if after is None else after[l])
        xm, h2, route, cnt = _outproj(x, dil, ob, gates, lw, seq, tm)
        slot_tok, blk_expert, blk_valid, blk_first, blk_io, comb_idx = _dispatch_plan(route, cnt, T, tm)
        xs = _gather_rows(h2, slot_tok)
        dispatched.append(xs)
        ys = _experts(xs, blk_expert, blk_valid, blk_first, blk_io, lw['weg'], lw['weu'], lw['wed'], l)
        yg = _gather_rows(ys, comb_idx)
        x = _combine(xm, yg, route, fg, l == len(layers) - 1, tm)
    return x.reshape(batch, seq, D_MODEL), dispatched


def kernel(x_prompt, x_sample, norm1_g, w_in, q_norm_g, w_uq, kv_norm_g, w_uk, w_uv, w_pa, w_pb, w_o,
           norm2_g, w_rg, b_rg, w_re, b_re, w_e_gate, w_e_up, w_e_down, final_g):
    layers = [_prep_layer(l, norm1_g, w_in, q_norm_g, w_uq, kv_norm_g, w_uk, w_uv, w_pa, w_pb, w_o,
                          norm2_g, w_rg, b_rg, w_re, b_re, w_e_gate, w_e_up, w_e_down)
              for l in range(DEPTH)]
    y_prompt, dispatched = _trunk(x_prompt, layers, final_g)
    y_sample, _ = _trunk(x_sample, layers, final_g, after=dispatched)
    return (y_prompt, y_sample)
```

```python
import functools

import jax
import jax.numpy as jnp
from jax import lax
from jax.experimental import pallas as pl
from jax.experimental.pallas import tpu as pltpu
from jax.experimental.pallas import tpu_sc as plsc

D_MODEL = 1024
DEPTH = 2
HEAD_DIM = 64
DIL_GROUPS = ((128, 1), (512, 4), (2048, 16))
DIL_HEADS_PER_GROUP = 4
DIL_WIDTH = 768
DIL_OUT = 256
DIL_SCALE = HEAD_DIM ** -0.5
ROT_DIM = 16
ROPE_THETA = 500000.0
DIL_HALF = 64

MLA_HEADS = 8
MLA_NOPE = 64
MLA_ROPE = 32
MLA_QK = 96
MLA_V = 64
MLA_OUT = 512
MLA_SCALE = MLA_QK ** -0.5
Q_LORA = 256
KV_LORA = 128

N_GROUPS = 8
EXPERTS_PER_GROUP = 8
N_EXPERTS = 64
TOP_K = 2
D_EXPERT = 512
EPS = 1e-6

LANES = 128
MLA_SLOT = LANES
MLA_PAD = MLA_HEADS * MLA_SLOT
ROUTE_LANES = LANES
VMEM_LIMIT = 56 * 1024 * 1024
MLA_KV_DOUBLE_BUFFER_MAX = 2 * 1024 * 1024

BF16 = jnp.bfloat16
F32 = jnp.float32
NEG = -1e30
LOG2E = 1.4426950408889634


def _cparams(sem):
    return pltpu.CompilerParams(dimension_semantics=sem, vmem_limit_bytes=VMEM_LIMIT)


def _rms(t, g):
    return t * lax.rsqrt(jnp.mean(t * t, axis=-1, keepdims=True) + EPS) * g


def _rope_chunk(t, c, sa, sb, shift):
    return t * c + pltpu.roll(t, LANES - shift, 1) * sa + pltpu.roll(t, shift, 1) * sb


def _inproj_kernel(x_ref, g1_ref, wq_ref, wk_ref, wv_ref, wcq_ref, wckv_ref, wkr_ref, wg_ref,
                   qn_ref, kvn_ref, wuq_ref, wuk_ref, wuv_ref,
                   ca_ref, saa_ref, sab_ref, cb_ref, sba_ref, sbb_ref,
                   q0_ref, q1_ref, q2_ref, k0_ref, k1_ref, k2_ref, v0_ref, v1_ref, v2_ref,
                   qm_ref, km_ref, vm_ref, gate_ref, q_sc, k_sc, v_sc):
    x = x_ref[...]
    h = _rms(x, g1_ref[...]).astype(BF16)
    tm = x.shape[0]

    ca, saa, sab = ca_ref[...], saa_ref[...], sab_ref[...]
    q = jnp.dot(h, wq_ref[...], preferred_element_type=F32)
    k = jnp.dot(h, wk_ref[...], preferred_element_type=F32)
    v = jnp.dot(h, wv_ref[...], preferred_element_type=F32)
    for j in range(DIL_WIDTH // LANES):
        sl = slice(j * LANES, (j + 1) * LANES)
        q_sc[j] = _rope_chunk(q[:, sl], ca, saa, sab, ROT_DIM // 2) * DIL_SCALE
        k_sc[j] = _rope_chunk(k[:, sl], ca, saa, sab, ROT_DIM // 2)
        v_sc[j] = v[:, sl]
    halves = DIL_OUT // LANES
    for src, outs in ((q_sc, (q0_ref, q1_ref, q2_ref)), (k_sc, (k0_ref, k1_ref, k2_ref)),
                      (v_sc, (v0_ref, v1_ref, v2_ref))):
        for g, (_, dil) in enumerate(DIL_GROUPS):
            for hf in range(halves):
                for r in range(dil):
                    rows = pl.ds(r, tm // dil, stride=dil) if dil > 1 else slice(None)
                    outs[g][0, r, :, hf * LANES:(hf + 1) * LANES] = src[g * halves + hf, rows, :].astype(BF16)

    cb, sba, sbb = cb_ref[...], sba_ref[...], sbb_ref[...]
    cq = jnp.dot(h, wcq_ref[...], preferred_element_type=F32)
    cqn = _rms(cq, qn_ref[...]).astype(BF16)
    qm = jnp.dot(cqn, wuq_ref[...], preferred_element_type=F32) * (MLA_SCALE * LOG2E)
    ckv = jnp.dot(h, wckv_ref[...], preferred_element_type=F32)
    c = _rms(ckv, kvn_ref[...]).astype(BF16)
    kn = jnp.dot(c, wuk_ref[...], preferred_element_type=F32)
    vv = jnp.dot(c, wuv_ref[...], preferred_element_type=F32)
    kr = jnp.dot(h, wkr_ref[...], preferred_element_type=F32)
    kr = _rope_chunk(kr, cb, sba, sbb, MLA_ROPE // 2)
    lane = lax.broadcasted_iota(jnp.int32, (1, LANES), 1)
    ones_col = jnp.where(lane == MLA_V, 1.0, 0.0).astype(F32)
    for j in range(MLA_HEADS):
        sl = slice(j * LANES, (j + 1) * LANES)
        qm_ref[:, sl] = _rope_chunk(qm[:, sl], cb, sba, sbb, MLA_ROPE // 2).astype(BF16)
        km_ref[:, sl] = (kn[:, sl] + kr).astype(BF16)
        vm_ref[:, sl] = (vv[:, sl] + ones_col).astype(BF16)

    gate_ref[...] = jax.nn.sigmoid(jnp.dot(h, wg_ref[...], preferred_element_type=F32))


def _inproj(x, lw, tabs, seq, tm):
    T = x.shape[0]
    nt = seq // tm
    row = lambda i: (i, 0)
    const = lambda i: (0, 0)
    tab = lambda i: (i % nt, 0)

    def wspec(a):
        return pl.BlockSpec(a.shape, const, pipeline_mode=pl.Buffered(1))

    weights = [lw['g1'], lw['wq'], lw['wk'], lw['wv'], lw['wcq'], lw['wckv'], lw['wkr'], lw['wg'],
               lw['qn'], lw['kvn'], lw['wuq'], lw['wuk'], lw['wuv']]
    in_specs = ([pl.BlockSpec((tm, D_MODEL), row)] + [wspec(a) for a in weights]
                + [pl.BlockSpec((tm, LANES), tab)] * 6)
    batch = T // seq
    res_map = lambda i: (i // nt, 0, i % nt, 0)
    dil_shape = [jax.ShapeDtypeStruct((batch, d, seq // d, DIL_OUT), BF16) for _, d in DIL_GROUPS] * 3
    dil_specs = [pl.BlockSpec((1, d, tm // d, DIL_OUT), res_map) for _, d in DIL_GROUPS] * 3
    out_shape = dil_shape + [jax.ShapeDtypeStruct((T, MLA_PAD), BF16)] * 3 \
        + [jax.ShapeDtypeStruct((T, 2 * D_MODEL), F32)]
    out_specs = dil_specs + [pl.BlockSpec((tm, MLA_PAD), row)] * 3 + [pl.BlockSpec((tm, 2 * D_MODEL), row)]
    return pl.pallas_call(
        _inproj_kernel, grid=(T // tm,), in_specs=in_specs, out_specs=out_specs, out_shape=out_shape,
        scratch_shapes=[pltpu.VMEM((DIL_WIDTH // LANES, tm, LANES), F32)] * 3,
        compiler_params=_cparams(("parallel",)), name="inproj",
    )(x, *weights, *tabs)


DIL_STEP_TOKENS = 2048


def _dil_kernel(q_ref, kp_ref, kc_ref, kn_ref, vp_ref, vc_ref, vn_ref, o_ref, lse_ref, *, tq, dil, sub_len):
    i = pl.program_id(1)
    qb = LANES
    nh = DIL_HEADS_PER_GROUP
    head = lax.broadcasted_iota(jnp.int32, (1, DIL_OUT), 1) // HEAD_DIM
    qloc = lax.broadcasted_iota(jnp.int32, (nh * qb, 2 * qb), 0) % qb
    rel = lax.broadcasted_iota(jnp.int32, (nh * qb, 2 * qb), 1) - DIL_HALF - qloc
    band = jnp.abs(rel) <= DIL_HALF
    kcol = lax.broadcasted_iota(jnp.int32, (1, 2 * qb), 1) - DIL_HALF

    def window(p_ref, c_ref, n_ref, r, j):
        lo, hi = j * qb - DIL_HALF, j * qb + 2 * qb - DIL_HALF
        parts = []
        if lo < 0:
            parts.append(p_ref[0, r, qb + lo:qb, :])
            lo = 0
        parts.append(c_ref[0, r, lo:min(hi, tq), :])
        if hi > tq:
            parts.append(n_ref[0, r, 0:hi - tq, :])
        return jnp.concatenate(parts, axis=0) if len(parts) > 1 else parts[0]

    for r in range(dil):
        for j in range(tq // qb):
            q = q_ref[0, r, j * qb:(j + 1) * qb, :]
            k = window(kp_ref, kc_ref, kn_ref, r, j)
            v = window(vp_ref, vc_ref, vn_ref, r, j)
            qs = jnp.concatenate([jnp.where(head == hd, q, jnp.zeros_like(q)) for hd in range(nh)], axis=0)
            s = lax.dot_general(qs, k, (((1,), (1,)), ((), ())), preferred_element_type=F32)
            kpos = kcol + (i * tq + j * qb)
            s = jnp.where(band & (kpos >= 0) & (kpos < sub_len), s, NEG)
            m = jnp.max(s, axis=-1, keepdims=True)
            p = jnp.exp(s - m)
            den = jnp.sum(p, axis=-1, keepdims=True)
            oh = jnp.dot(p.astype(BF16), v, preferred_element_type=F32) * (1.0 / den)
            lse = m + jnp.log(den)
            o_acc = oh[0:qb]
            l_acc = jnp.broadcast_to(lse[0:qb], (qb, DIL_OUT))
            for hd in range(1, nh):
                o_acc = jnp.where(head == hd, oh[hd * qb:(hd + 1) * qb], o_acc)
                l_acc = jnp.where(head == hd, lse[hd * qb:(hd + 1) * qb], l_acc)
            rows = pl.ds(j * qb * dil + r, qb, stride=dil) if dil > 1 else slice(j * qb, (j + 1) * qb)
            for hf in range(DIL_OUT // LANES):
                o_ref[0, hf, rows, :] = o_acc[:, hf * LANES:(hf + 1) * LANES]
                lse_ref[0, hf, rows, :] = l_acc[:, hf * LANES:(hf + 1) * LANES]


def _dilated_group(qd, kd, vd, batch, seq, g, dil):
    L = seq // dil
    tq = min(L, DIL_STEP_TOKENS // dil)
    nq = L // tq
    hb = tq // LANES
    nhb = L // LANES
    cur = lambda b, i: (b, 0, i, 0)
    prev = lambda b, i: (b, 0, jnp.maximum(i * hb - 1, 0), 0)
    nxt = lambda b, i: (b, 0, jnp.minimum((i + 1) * hb, nhb - 1), 0)
    blk = (1, dil, tq, DIL_OUT)
    halo = (1, dil, LANES, DIL_OUT)
    halves = DIL_OUT // LANES
    out_blk = (1, halves, tq * dil, LANES)
    out_map = lambda b, i: (b, 0, i, 0)
    o, lse = pl.pallas_call(
        functools.partial(_dil_kernel, tq=tq, dil=dil, sub_len=L),
        grid=(batch, nq),
        in_specs=[pl.BlockSpec(blk, cur),
                  pl.BlockSpec(halo, prev), pl.BlockSpec(blk, cur), pl.BlockSpec(halo, nxt),
                  pl.BlockSpec(halo, prev), pl.BlockSpec(blk, cur), pl.BlockSpec(halo, nxt)],
        out_specs=[pl.BlockSpec(out_blk, out_map), pl.BlockSpec(out_blk, out_map)],
        out_shape=[jax.ShapeDtypeStruct((batch, halves, seq, LANES), F32)] * 2,
        compiler_params=_cparams(("parallel", "parallel")), name=f"dilated_g{g}",
    )(qd, kd, kd, kd, vd, vd, vd)
    return o, lse


def _mla_kernel(q_ref, k_ref, v_ref, o_ref, m_sc, acc_sc, *, tk, nk, unroll):
    nc = tk // LANES
    m_sc[...] = jnp.full(m_sc.shape, NEG, F32)
    acc_sc[...] = jnp.zeros(acc_sc.shape, F32)

    def step(off, hh):
        sl = slice(hh * LANES, (hh + 1) * LANES)
        q = q_ref[0, :, sl]
        k = k_ref[0, pl.ds(off, tk), sl]
        v = v_ref[0, pl.ds(off, tk), sl]
        s = lax.dot_general(q, k, (((1,), (1,)), ((), ())), preferred_element_type=F32)
        cols = [s[:, c * LANES:(c + 1) * LANES] for c in range(nc)]
        m_old = m_sc[hh]
        m_new = jnp.maximum(m_old, jnp.max(functools.reduce(jnp.maximum, cols), axis=-1, keepdims=True))
        p = jnp.concatenate([jnp.exp2(c - m_new) for c in cols], axis=1).astype(BF16)
        acc_sc[hh] = acc_sc[hh] * jnp.exp2(m_old - m_new) + jnp.dot(p, v, preferred_element_type=F32)
        m_sc[hh] = m_new

    def body(j, carry):
        for u in range(unroll):
            off = pl.multiple_of((j * unroll + u) * tk, tk)
            for hh in range(2):
                step(off, hh)
        return carry

    lax.fori_loop(0, nk // unroll, body, 0)
    outs = []
    for hh in range(2):
        acc = acc_sc[hh]
        outs.append(acc * (1.0 / acc[:, MLA_V:MLA_V + 1]))
    lane = lax.broadcasted_iota(jnp.int32, (1, LANES), 1)
    o_ref[0] = jnp.where(lane < MLA_V, outs[0], pltpu.roll(outs[1], MLA_V, 1)).astype(o_ref.dtype)


def _mla(qm, km, vm, batch, seq, tq=1024, tk=2048, unroll=1):
    tq = min(seq, tq)
    tk = min(seq, tk)
    nk = seq // tk
    unroll = unroll if nk % unroll == 0 else 1
    q3 = qm.reshape(batch, seq, MLA_PAD)
    k3 = km.reshape(batch, seq, MLA_PAD)
    v3 = vm.reshape(batch, seq, MLA_PAD)
    resident = lambda b, h, i: (b, 0, h)
    kv_bytes = seq * 2 * LANES * 2
    kv_mode = dict(pipeline_mode=pl.Buffered(1)) if kv_bytes > MLA_KV_DOUBLE_BUFFER_MAX else {}
    o = pl.pallas_call(
        functools.partial(_mla_kernel, tk=tk, nk=nk, unroll=unroll),
        grid=(batch, MLA_HEADS // 2, seq // tq),
        in_specs=[pl.BlockSpec((1, tq, 2 * LANES), lambda b, h, i: (b, i, h)),
                  pl.BlockSpec((1, seq, 2 * LANES), resident, **kv_mode),
                  pl.BlockSpec((1, seq, 2 * LANES), resident, **kv_mode)],
        out_specs=pl.BlockSpec((1, tq, LANES), lambda b, h, i: (b, i, h)),
        out_shape=jax.ShapeDtypeStruct((batch, seq, MLA_OUT), BF16),
        scratch_shapes=[pltpu.VMEM((2, tq, LANES), F32), pltpu.VMEM((2, tq, LANES), F32)],
        compiler_params=_cparams(("parallel", "parallel", "arbitrary")), name="mla",
    )(q3, k3, v3)
    return o.reshape(batch * seq, MLA_OUT)


def _outproj_kernel(x_ref, o0_ref, l0_ref, o1_ref, l1_ref, o2_ref, l2_ref, ob_ref, gate_ref,
                    wpa_ref, wpb_ref, wo_ref, g2_ref, wrh_ref, wrl_ref, br_ref,
                    xm_ref, h2_ref, route_ref, cnt_ref, cnt_sc):
    def halves(ref):
        return jnp.concatenate([ref[0, hf] for hf in range(ref.shape[1])], axis=1)

    l0, l1, l2 = halves(l0_ref), halves(l1_ref), halves(l2_ref)
    m = jnp.maximum(jnp.maximum(l0, l1), l2)
    w0, w1, w2 = jnp.exp(l0 - m), jnp.exp(l1 - m), jnp.exp(l2 - m)
    oa = (w0 * halves(o0_ref) + w1 * halves(o1_ref) + w2 * halves(o2_ref)) / (w0 + w1 + w2)
    pa = jnp.dot(oa.astype(BF16), wpa_ref[...], preferred_element_type=F32)
    pb = jnp.dot(ob_ref[...], wpb_ref[...], preferred_element_type=F32)
    merged = gate_ref[:, :D_MODEL] * pa + gate_ref[:, D_MODEL:] * pb
    xm = x_ref[...] + jnp.dot(merged.astype(BF16), wo_ref[...], preferred_element_type=F32)
    xm_ref[...] = xm
    h2 = _rms(xm, g2_ref[...])
    h2_ref[...] = h2

    hi = h2.astype(BF16)
    lo = (h2 - hi.astype(F32)).astype(BF16)
    lg = (jnp.dot(hi, wrh_ref[...], preferred_element_type=F32)
          + jnp.dot(lo, wrh_ref[...], preferred_element_type=F32)
          + jnp.dot(hi, wrl_ref[...], preferred_element_type=F32)) + br_ref[...]
    lane = lax.broadcasted_iota(jnp.int32, lg.shape, 1)
    lanef = lane.astype(F32)
    big = float(ROUTE_LANES)
    gmask = (lane >= N_EXPERTS) & (lane < N_EXPERTS + N_GROUPS)
    gl = jnp.where(gmask, lg, NEG)
    gmax = jnp.max(gl, axis=-1, keepdims=True)
    gidx = jnp.min(jnp.where(gl == gmax, lanef, big), axis=-1, keepdims=True) - float(N_EXPERTS)
    p_grp = 1.0 / jnp.sum(jnp.where(gmask, jnp.exp(gl - gmax), 0.0), axis=-1, keepdims=True)
    emask = (lane // EXPERTS_PER_GROUP).astype(F32) == gidx
    el = jnp.where(emask, lg, NEG)
    m1 = jnp.max(el, axis=-1, keepdims=True)
    i1 = jnp.min(jnp.where(el == m1, lanef, big), axis=-1, keepdims=True)
    el2 = jnp.where(lanef == i1, NEG, el)
    m2 = jnp.max(el2, axis=-1, keepdims=True)
    i2 = jnp.min(jnp.where(el2 == m2, lanef, big), axis=-1, keepdims=True)
    r = jnp.exp(m2 - m1)
    wa = p_grp / (1.0 + r)
    wb = p_grp * r / (1.0 + r)
    @pl.when(pl.program_id(0) == 0)
    def _():
        cnt_sc[...] = jnp.zeros(cnt_sc.shape, F32)

    oh1 = lanef == i1
    oh2 = lanef == i2
    oh = jnp.where(oh1 | oh2, 1.0, 0.0)
    tm = lg.shape[0]
    earlier = (lax.broadcasted_iota(jnp.int32, (tm, tm), 0) > lax.broadcasted_iota(jnp.int32, (tm, tm), 1))
    before = jnp.dot(jnp.where(earlier, 1.0, 0.0).astype(BF16), oh.astype(BF16),
                     preferred_element_type=F32) + cnt_sc[0:1, :]
    r1 = jnp.sum(jnp.where(oh1, before, 0.0), axis=-1, keepdims=True)
    r2 = jnp.sum(jnp.where(oh2, before, 0.0), axis=-1, keepdims=True)
    cnt = cnt_sc[...] + jnp.sum(oh, axis=0, keepdims=True)
    cnt_sc[...] = cnt
    cnt_ref[...] = cnt
    route_ref[...] = jnp.where(lane == 0, i1, jnp.where(lane == 1, i2, jnp.where(lane == 2, wa,
                               jnp.where(lane == 3, wb, jnp.where(lane == 4, r1,
                                                                  jnp.where(lane == 5, r2, 0.0))))))


def _outproj(x, dil_outs, ob, gates, lw, seq, tm):
    T = x.shape[0]
    nt = seq // tm
    row = lambda i: (i, 0)
    const = lambda i: (0, 0)
    weights = [lw['wpa'], lw['wpb'], lw['wo'], lw['g2'], lw['wrh'], lw['wrl'], lw['br']]
    acts = [x] + list(dil_outs) + [ob, gates]
    dil_spec = pl.BlockSpec((1, DIL_OUT // LANES, tm, LANES), lambda i: (i // nt, 0, i % nt, 0))
    in_specs = [pl.BlockSpec((tm, D_MODEL), row)] + [dil_spec] * len(dil_outs) \
        + [pl.BlockSpec((tm, ob.shape[1]), row), pl.BlockSpec((tm, gates.shape[1]), row)] \
        + [pl.BlockSpec(a.shape, const, pipeline_mode=pl.Buffered(1)) for a in weights]
    return pl.pallas_call(
        _outproj_kernel, grid=(T // tm,), in_specs=in_specs,
        out_specs=[pl.BlockSpec((tm, D_MODEL), row), pl.BlockSpec((tm, D_MODEL), row),
                   pl.BlockSpec((tm, ROUTE_LANES), row), pl.BlockSpec((8, ROUTE_LANES), const)],
        out_shape=[jax.ShapeDtypeStruct((T, D_MODEL), F32), jax.ShapeDtypeStruct((T, D_MODEL), F32),
                   jax.ShapeDtypeStruct((T, ROUTE_LANES), F32), jax.ShapeDtypeStruct((8, ROUTE_LANES), F32)],
        scratch_shapes=[pltpu.VMEM((8, ROUTE_LANES), F32)],
        compiler_params=_cparams(("arbitrary",)), name="outproj_router",
    )(*acts, *weights)


GATHER_WINDOW = 32


def _gather_rows(src, idx):
    m = idx.shape[0]
    width = src.shape[1]
    info = plsc.get_sparse_core_info()
    n_workers = info.num_cores * info.num_subcores
    per_w = m // n_workers
    n_pairs = per_w // (2 * GATHER_WINDOW)
    assert n_pairs * 2 * GATHER_WINDOW * n_workers == m
    mesh = plsc.VectorSubcoreMesh(core_axis_name="core", subcore_axis_name="subcore")

    @functools.partial(
        pl.kernel, out_type=jax.ShapeDtypeStruct((m, width), src.dtype), mesh=mesh, name="gather_rows",
        scratch_types=[pltpu.VMEM((per_w,), jnp.int32),
                       pltpu.VMEM((2, GATHER_WINDOW, width), src.dtype),
                       pltpu.SemaphoreType.DMA((2,))])
    def gather(src_hbm, idx_hbm, out_hbm, idx_v, rows_v, sems):
        wid = lax.axis_index("subcore") * info.num_cores + lax.axis_index("core")
        base = wid * per_w
        pltpu.sync_copy(idx_hbm.at[pl.ds(base, per_w)], idx_v)

        def fetch(c, b):
            return pltpu.make_async_copy(src_hbm.at[idx_v.at[pl.ds(c * GATHER_WINDOW, GATHER_WINDOW)]],
                                         rows_v.at[b], sems.at[b])

        def flush(c, b):
            pltpu.sync_copy(rows_v.at[b], out_hbm.at[pl.ds(base + c * GATHER_WINDOW, GATHER_WINDOW)])

        fetch(0, 0).start()

        @pl.loop(0, n_pairs)
        def _(p):
            c = 2 * p
            fetch(c + 1, 1).start()
            fetch(c, 0).wait()
            flush(c, 0)

            @pl.when(p + 1 < n_pairs)
            def _():
                fetch(c + 2, 0).start()

            fetch(c + 1, 1).wait()
            flush(c + 1, 1)

    return gather(src, idx)


EXPERT_BLOCK = 512


def _expert_kernel(be_ref, bv_ref, bf_ref, bio_ref, xs_ref, wg_ref, wu_ref, wd_ref, y_ref,
                   wg_sc, wu_sc, wd_sc):
    i = pl.program_id(0)

    @pl.when(bf_ref[i] > 0)
    def _():
        wg_sc[...] = wg_ref[0].astype(BF16)
        wu_sc[...] = wu_ref[0].astype(BF16)
        wd_sc[...] = wd_ref[0].astype(BF16)

    @pl.when(bv_ref[i] > 0)
    def _():
        xb = xs_ref[...].astype(BF16)
        g = jnp.dot(xb, wg_sc[...], preferred_element_type=F32)
        u = jnp.dot(xb, wu_sc[...], preferred_element_type=F32)
        hb = (g * jax.nn.sigmoid(g) * u).astype(BF16)
        y_ref[...] = jnp.dot(hb, wd_sc[...], preferred_element_type=F32)


def _experts(xs, blk_expert, blk_valid, blk_first, blk_io, wg, wu, wd, layer):
    n_slots = xs.shape[0]
    nb = n_slots // EXPERT_BLOCK
    grid_spec = pltpu.PrefetchScalarGridSpec(
        num_scalar_prefetch=4, grid=(nb,),
        in_specs=[pl.BlockSpec((EXPERT_BLOCK, D_MODEL), lambda i, be, bv, bf, bio: (bio[i], 0)),
                  pl.BlockSpec((None, 1, D_MODEL, D_EXPERT), lambda i, be, bv, bf, bio: (layer, be[i], 0, 0)),
                  pl.BlockSpec((None, 1, D_MODEL, D_EXPERT), lambda i, be, bv, bf, bio: (layer, be[i], 0, 0)),
                  pl.BlockSpec((None, 1, D_EXPERT, D_MODEL), lambda i, be, bv, bf, bio: (layer, be[i], 0, 0))],
        out_specs=pl.BlockSpec((EXPERT_BLOCK, D_MODEL), lambda i, be, bv, bf, bio: (bio[i], 0)),
        scratch_shapes=[pltpu.VMEM((D_MODEL, D_EXPERT), BF16), pltpu.VMEM((D_MODEL, D_EXPERT), BF16),
                        pltpu.VMEM((D_EXPERT, D_MODEL), BF16)])
    return pl.pallas_call(
        _expert_kernel, grid_spec=grid_spec,
        out_shape=jax.ShapeDtypeStruct((n_slots, D_MODEL), F32),
        compiler_params=_cparams(("arbitrary",)), name="experts",
    )(blk_expert, blk_valid, blk_first, blk_io, xs, wg, wu, wd)


def _combine_kernel(x_ref, ya_ref, yb_ref, route_ref, g_ref, o_ref, *, final):
    route = route_ref[...]
    wa = route[:, 2:3]
    wb = route[:, 3:4]
    x = x_ref[...] + (ya_ref[...] * wa + yb_ref[...] * wb)
    if final:
        x = _rms(x, g_ref[...])
    o_ref[...] = x


def _combine(x, yg, route, g, final, tm):
    T = x.shape[0]
    nt = T // tm
    return pl.pallas_call(
        functools.partial(_combine_kernel, final=final), grid=(nt,),
        in_specs=[pl.BlockSpec((tm, D_MODEL), lambda i: (i, 0)),
                  pl.BlockSpec((tm, D_MODEL), lambda i: (i, 0)),
                  pl.BlockSpec((tm, D_MODEL), lambda i: (i + nt, 0)),
                  pl.BlockSpec((tm, ROUTE_LANES), lambda i: (i, 0)),
                  pl.BlockSpec((1, D_MODEL), lambda i: (0, 0))],
        out_specs=pl.BlockSpec((tm, D_MODEL), lambda i: (i, 0)),
        out_shape=jax.ShapeDtypeStruct((T, D_MODEL), F32),
        compiler_params=_cparams(("parallel",)), name="combine",
    )(x, yg, yg, route, g)


def _slots_kernel(route_ref, start_ref, slot_ref):
    route = route_ref[...]
    lanef = lax.broadcasted_iota(jnp.int32, route.shape, 1).astype(F32)
    lane = lax.broadcasted_iota(jnp.int32, route.shape, 1)
    start = start_ref[...]

    def slot(k):
        first = jnp.sum(jnp.where(lanef == route[:, k:k + 1], start, 0.0), axis=-1, keepdims=True)
        return first + route[:, 4 + k:5 + k]

    slab = jnp.where(lane == 0, slot(0), jnp.where(lane == 1, slot(1), 0.0))
    slot_ref[...] = jnp.transpose(slab)[0:8, :]


def _slots(route, pad_start, tm):
    T = route.shape[0]
    start = jnp.pad(pad_start.astype(F32), (0, ROUTE_LANES - N_EXPERTS))[None, :]
    out = pl.pallas_call(
        _slots_kernel, grid=(T // tm,),
        in_specs=[pl.BlockSpec((tm, ROUTE_LANES), lambda i: (i, 0)),
                  pl.BlockSpec((1, ROUTE_LANES), lambda i: (0, 0))],
        out_specs=pl.BlockSpec((8, tm), lambda i: (0, i)),
        out_shape=jax.ShapeDtypeStruct((8, T), F32),
        compiler_params=_cparams(("parallel",)), name="slots",
    )(route, start)
    return out[0:TOP_K].astype(jnp.int32)


def _dispatch_plan(route, cnt, T, tm):
    A = T * TOP_K
    counts = cnt[0, :N_EXPERTS].astype(jnp.int32)
    padded = (counts + EXPERT_BLOCK - 1) // EXPERT_BLOCK * EXPERT_BLOCK
    pad_end = jnp.cumsum(padded)
    pad_start = pad_end - padded
    slot = _slots(route, pad_start, tm)
    n_blocks = A // EXPERT_BLOCK + N_EXPERTS
    n_slots = n_blocks * EXPERT_BLOCK
    tok = jnp.broadcast_to(jnp.arange(T, dtype=jnp.int32)[None, :], (TOP_K, T))
    blk_start = jnp.arange(n_blocks, dtype=jnp.int32) * EXPERT_BLOCK
    blk_valid = (blk_start < pad_end[-1]).astype(jnp.int32)
    blk_io = jnp.minimum(jnp.arange(n_blocks, dtype=jnp.int32), jnp.sum(blk_valid) - 1)
    blk_expert = jnp.minimum(jnp.sum((blk_start[:, None] >= pad_end[None, :]).astype(jnp.int32), axis=1),
                             N_EXPERTS - 1)[blk_io]
    blk_first = jnp.concatenate([jnp.ones((1,), jnp.int32),
                                 (blk_expert[1:] != blk_expert[:-1]).astype(jnp.int32)])
    _, tok_sorted = lax.sort_key_val(slot.reshape(A), tok.reshape(A))
    shift = pad_start - (jnp.cumsum(counts) - counts)
    src = (blk_start - shift[blk_expert])[:, None] + jnp.arange(EXPERT_BLOCK, dtype=jnp.int32)[None, :]
    slot_tok = tok_sorted[src % A].reshape(n_slots)
    comb_idx = slot.reshape(A)
    return slot_tok, blk_expert, blk_valid, blk_first, blk_io, comb_idx


def _rope_tables(seq, dim, period, first):
    half = dim // 2
    inv = ROPE_THETA ** (-jnp.arange(0, dim, 2, dtype=F32) / dim)
    ang = jnp.arange(seq, dtype=F32)[:, None] * inv[None, :]
    cos, sin = jnp.cos(ang), jnp.sin(ang)
    d = jnp.arange(LANES) % period - first
    in_a = (d >= 0) & (d < half)
    in_b = (d >= half) & (d < dim)
    idx = jnp.clip(jnp.where(in_b, d - half, d), 0, half - 1)
    c = jnp.where((in_a | in_b)[None, :], cos[:, idx], 1.0)
    sa = jnp.where(in_a[None, :], -sin[:, idx], 0.0)
    sb = jnp.where(in_b[None, :], sin[:, idx], 0.0)
    return c, sa, sb


def _pad_heads(w, width):
    k = w.shape[0]
    w = w.reshape(k, MLA_HEADS, width)
    return jnp.pad(w, ((0, 0), (0, 0), (0, MLA_SLOT - width))).reshape(k, MLA_PAD)


def _prep_layer(l, norm1_g, w_in, q_norm_g, w_uq, kv_norm_g, w_uk, w_uv, w_pa, w_pb, w_o,
                norm2_g, w_rg, b_rg, w_re, b_re, w_e_gate, w_e_up, w_e_down):
    w = w_in[l]
    o = 0
    parts = []
    for n in (DIL_WIDTH, DIL_WIDTH, DIL_WIDTH, Q_LORA, KV_LORA, MLA_ROPE, 2 * D_MODEL):
        parts.append(w[:, o:o + n])
        o += n
    wq, wk, wv, wcq, wckv, wkr, wg = parts
    wkr_pad = jnp.pad(wkr, ((0, 0), (MLA_NOPE, MLA_SLOT - MLA_NOPE - MLA_ROPE)))
    wr = jnp.pad(jnp.concatenate([w_re[l], w_rg[l]], axis=1),
                 ((0, 0), (0, ROUTE_LANES - N_EXPERTS - N_GROUPS)))
    wrh = wr.astype(BF16)
    wrl = (wr - wrh.astype(F32)).astype(BF16)
    br = jnp.pad(jnp.concatenate([b_re[l], b_rg[l]]), (0, ROUTE_LANES - N_EXPERTS - N_GROUPS))
    return dict(
        g1=norm1_g[l][None, :], wq=wq.astype(BF16), wk=wk.astype(BF16), wv=wv.astype(BF16),
        wcq=wcq.astype(BF16), wckv=wckv.astype(BF16), wkr=wkr_pad.astype(BF16), wg=wg.astype(BF16),
        qn=q_norm_g[l][None, :], kvn=kv_norm_g[l][None, :],
        wuq=_pad_heads(w_uq[l], MLA_QK).astype(BF16), wuk=_pad_heads(w_uk[l], MLA_NOPE).astype(BF16),
        wuv=_pad_heads(w_uv[l], MLA_V).astype(BF16),
        wpa=w_pa[l].astype(BF16), wpb=w_pb[l].astype(BF16), wo=w_o[l].astype(BF16),
        g2=norm2_g[l][None, :], wrh=wrh, wrl=wrl, br=br[None, :].astype(F32),
        weg=w_e_gate, weu=w_e_up, wed=w_e_down)


def _trunk(x3, layers, final_g, tm=512):
    batch, seq, _ = x3.shape
    T = batch * seq
    x = x3.reshape(T, D_MODEL)
    tabs = _rope_tables(seq, ROT_DIM, HEAD_DIM, 0) + _rope_tables(seq, MLA_ROPE, MLA_SLOT, MLA_NOPE)
    fg = final_g[None, :]
    for l, lw in enumerate(layers):
        outs = _inproj(x, lw, tabs, seq, tm)
        n_g = len(DIL_GROUPS)
        qd, kd, vd = outs[0:n_g], outs[n_g:2 * n_g], outs[2 * n_g:3 * n_g]
        qm, km, vm, gates = outs[3 * n_g:]
        dil = []
        for g, (_, d) in enumerate(DIL_GROUPS):
            dil.extend(_dilated_group(qd[g], kd[g], vd[g], batch, seq, g, d))
        ob = _mla(qm, km, vm, batch, seq)
        xm, h2, route, cnt = _outproj(x, dil, ob, gates, lw, seq, tm)
        slot_tok, blk_expert, blk_valid, blk_first, blk_io, comb_idx = _dispatch_plan(route, cnt, T, tm)
        xs = _gather_rows(h2, slot_tok)
        ys = _experts(xs, blk_expert, blk_valid, blk_first, blk_io, lw['weg'], lw['weu'], lw['wed'], l)
        yg = _gather_rows(ys, comb_idx)
        x = _combine(xm, yg, route, fg, l == len(layers) - 1, tm)
    return x.reshape(batch, seq, D_MODEL)


def kernel(x_prompt, x_sample, norm1_g, w_in, q_norm_g, w_uq, kv_norm_g, w_uk, w_uv, w_pa, w_pb, w_o,
           norm2_g, w_rg, b_rg, w_re, b_re, w_e_gate, w_e_up, w_e_down, final_g):
    layers = [_prep_layer(l, norm1_g, w_in, q_norm_g, w_uq, kv_norm_g, w_uk, w_uv, w_pa, w_pb, w_o,
                          norm2_g, w_rg, b_rg, w_re, b_re, w_e_gate, w_e_up, w_e_down)
              for l in range(DEPTH)]
    y_sample = _trunk(x_sample, layers, final_g)
    y_prompt = _trunk(x_prompt, layers, final_g)
    return (y_prompt, y_sample)
```

```python
import functools

import jax
import jax.numpy as jnp
from jax import lax
from jax.experimental import pallas as pl
from jax.experimental.pallas import tpu as pltpu
from jax.experimental.pallas import tpu_sc as plsc

D_MODEL = 1024
DEPTH = 2
HEAD_DIM = 64
DIL_GROUPS = ((128, 1), (512, 4), (2048, 16))
DIL_HEADS_PER_GROUP = 4
DIL_WIDTH = 768
DIL_OUT = 256
DIL_SCALE = HEAD_DIM ** -0.5
ROT_DIM = 16
ROPE_THETA = 500000.0
DIL_HALF = 64

MLA_HEADS = 8
MLA_NOPE = 64
MLA_ROPE = 32
MLA_QK = 96
MLA_V = 64
MLA_OUT = 512
MLA_SCALE = MLA_QK ** -0.5
Q_LORA = 256
KV_LORA = 128

N_GROUPS = 8
EXPERTS_PER_GROUP = 8
N_EXPERTS = 64
TOP_K = 2
D_EXPERT = 512
EPS = 1e-6

LANES = 128
MLA_SLOT = LANES
MLA_PAD = MLA_HEADS * MLA_SLOT
ROUTE_LANES = LANES
VMEM_LIMIT = 56 * 1024 * 1024
MLA_KV_DOUBLE_BUFFER_MAX = 2 * 1024 * 1024

BF16 = jnp.bfloat16
F32 = jnp.float32
NEG = -1e30
LOG2E = 1.4426950408889634


def _cparams(sem):
    return pltpu.CompilerParams(dimension_semantics=sem, vmem_limit_bytes=VMEM_LIMIT)


def _rms(t, g):
    return t * lax.rsqrt(jnp.mean(t * t, axis=-1, keepdims=True) + EPS) * g


def _rope_chunk(t, c, sa, sb, shift):
    return t * c + pltpu.roll(t, LANES - shift, 1) * sa + pltpu.roll(t, shift, 1) * sb


def _inproj_kernel(x_ref, g1_ref, wq_ref, wk_ref, wv_ref, wcq_ref, wckv_ref, wkr_ref, wg_ref,
                   qn_ref, kvn_ref, wuq_ref, wuk_ref, wuv_ref,
                   ca_ref, saa_ref, sab_ref, cb_ref, sba_ref, sbb_ref,
                   q0_ref, q1_ref, q2_ref, k0_ref, k1_ref, k2_ref, v0_ref, v1_ref, v2_ref,
                   qm_ref, km_ref, vm_ref, gate_ref, q_sc, k_sc, v_sc):
    x = x_ref[...]
    h = _rms(x, g1_ref[...]).astype(BF16)
    tm = x.shape[0]

    ca, saa, sab = ca_ref[...], saa_ref[...], sab_ref[...]
    q = jnp.dot(h, wq_ref[...], preferred_element_type=F32)
    k = jnp.dot(h, wk_ref[...], preferred_element_type=F32)
    v = jnp.dot(h, wv_ref[...], preferred_element_type=F32)
    for j in range(DIL_WIDTH // LANES):
        sl = slice(j * LANES, (j + 1) * LANES)
        q_sc[j] = _rope_chunk(q[:, sl], ca, saa, sab, ROT_DIM // 2) * DIL_SCALE
        k_sc[j] = _rope_chunk(k[:, sl], ca, saa, sab, ROT_DIM // 2)
        v_sc[j] = v[:, sl]
    halves = DIL_OUT // LANES
    for src, outs in ((q_sc, (q0_ref, q1_ref, q2_ref)), (k_sc, (k0_ref, k1_ref, k2_ref)),
                      (v_sc, (v0_ref, v1_ref, v2_ref))):
        for g, (_, dil) in enumerate(DIL_GROUPS):
            for hf in range(halves):
                for r in range(dil):
                    rows = pl.ds(r, tm // dil, stride=dil) if dil > 1 else slice(None)
                    outs[g][0, r, :, hf * LANES:(hf + 1) * LANES] = src[g * halves + hf, rows, :].astype(BF16)

    cb, sba, sbb = cb_ref[...], sba_ref[...], sbb_ref[...]
    cq = jnp.dot(h, wcq_ref[...], preferred_element_type=F32)
    cqn = _rms(cq, qn_ref[...]).astype(BF16)
    qm = jnp.dot(cqn, wuq_ref[...], preferred_element_type=F32) * (MLA_SCALE * LOG2E)
    ckv = jnp.dot(h, wckv_ref[...], preferred_element_type=F32)
    c = _rms(ckv, kvn_ref[...]).astype(BF16)
    kn = jnp.dot(c, wuk_ref[...], preferred_element_type=F32)
    vv = jnp.dot(c, wuv_ref[...], preferred_element_type=F32)
    kr = jnp.dot(h, wkr_ref[...], preferred_element_type=F32)
    kr = _rope_chunk(kr, cb, sba, sbb, MLA_ROPE // 2)
    lane = lax.broadcasted_iota(jnp.int32, (1, LANES), 1)
    ones_col = jnp.where(lane == MLA_V, 1.0, 0.0).astype(F32)
    for j in range(MLA_HEADS):
        sl = slice(j * LANES, (j + 1) * LANES)
        qm_ref[:, sl] = _rope_chunk(qm[:, sl], cb, sba, sbb, MLA_ROPE // 2).astype(BF16)
        km_ref[:, sl] = (kn[:, sl] + kr).astype(BF16)
        vm_ref[:, sl] = (vv[:, sl] + ones_col).astype(BF16)

    gate_ref[...] = jax.nn.sigmoid(jnp.dot(h, wg_ref[...], preferred_element_type=F32))


def _inproj(x, lw, tabs, seq, tm):
    T = x.shape[0]
    nt = seq // tm
    row = lambda i: (i, 0)
    const = lambda i: (0, 0)
    tab = lambda i: (i % nt, 0)

    def wspec(a):
        return pl.BlockSpec(a.shape, const, pipeline_mode=pl.Buffered(1))

    weights = [lw['g1'], lw['wq'], lw['wk'], lw['wv'], lw['wcq'], lw['wckv'], lw['wkr'], lw['wg'],
               lw['qn'], lw['kvn'], lw['wuq'], lw['wuk'], lw['wuv']]
    in_specs = ([pl.BlockSpec((tm, D_MODEL), row)] + [wspec(a) for a in weights]
                + [pl.BlockSpec((tm, LANES), tab)] * 6)
    batch = T // seq
    res_map = lambda i: (i // nt, 0, i % nt, 0)
    dil_shape = [jax.ShapeDtypeStruct((batch, d, seq // d, DIL_OUT), BF16) for _, d in DIL_GROUPS] * 3
    dil_specs = [pl.BlockSpec((1, d, tm // d, DIL_OUT), res_map) for _, d in DIL_GROUPS] * 3
    out_shape = dil_shape + [jax.ShapeDtypeStruct((T, MLA_PAD), BF16)] * 3 \
        + [jax.ShapeDtypeStruct((T, 2 * D_MODEL), F32)]
    out_specs = dil_specs + [pl.BlockSpec((tm, MLA_PAD), row)] * 3 + [pl.BlockSpec((tm, 2 * D_MODEL), row)]
    return pl.pallas_call(
        _inproj_kernel, grid=(T // tm,), in_specs=in_specs, out_specs=out_specs, out_shape=out_shape,
        scratch_shapes=[pltpu.VMEM((DIL_WIDTH // LANES, tm, LANES), F32)] * 3,
        compiler_params=_cparams(("parallel",)), name="inproj",
    )(x, *weights, *tabs)


DIL_STEP_TOKENS = 2048


def _dil_kernel(q_ref, kp_ref, kc_ref, kn_ref, vp_ref, vc_ref, vn_ref, o_ref, lse_ref, *, tq, dil, sub_len):
    i = pl.program_id(1)
    qb = LANES
    nh = DIL_HEADS_PER_GROUP
    head = lax.broadcasted_iota(jnp.int32, (1, DIL_OUT), 1) // HEAD_DIM
    qloc = lax.broadcasted_iota(jnp.int32, (nh * qb, 2 * qb), 0) % qb
    rel = lax.broadcasted_iota(jnp.int32, (nh * qb, 2 * qb), 1) - DIL_HALF - qloc
    band = jnp.abs(rel) <= DIL_HALF
    kcol = lax.broadcasted_iota(jnp.int32, (1, 2 * qb), 1) - DIL_HALF

    def window(p_ref, c_ref, n_ref, r, j):
        lo, hi = j * qb - DIL_HALF, j * qb + 2 * qb - DIL_HALF
        parts = []
        if lo < 0:
            parts.append(p_ref[0, r, qb + lo:qb, :])
            lo = 0
        parts.append(c_ref[0, r, lo:min(hi, tq), :])
        if hi > tq:
            parts.append(n_ref[0, r, 0:hi - tq, :])
        return jnp.concatenate(parts, axis=0) if len(parts) > 1 else parts[0]

    for r in range(dil):
        for j in range(tq // qb):
            q = q_ref[0, r, j * qb:(j + 1) * qb, :]
            k = window(kp_ref, kc_ref, kn_ref, r, j)
            v = window(vp_ref, vc_ref, vn_ref, r, j)
            qs = jnp.concatenate([jnp.where(head == hd, q, jnp.zeros_like(q)) for hd in range(nh)], axis=0)
            s = lax.dot_general(qs, k, (((1,), (1,)), ((), ())), preferred_element_type=F32)
            kpos = kcol + (i * tq + j * qb)
            s = jnp.where(band & (kpos >= 0) & (kpos < sub_len), s, NEG)
            m = jnp.max(s, axis=-1, keepdims=True)
            p = jnp.exp(s - m)
            den = jnp.sum(p, axis=-1, keepdims=True)
            oh = jnp.dot(p.astype(BF16), v, preferred_element_type=F32) * (1.0 / den)
            lse = m + jnp.log(den)
            o_acc = oh[0:qb]
            l_acc = jnp.broadcast_to(lse[0:qb], (qb, DIL_OUT))
            for hd in range(1, nh):
                o_acc = jnp.where(head == hd, oh[hd * qb:(hd + 1) * qb], o_acc)
                l_acc = jnp.where(head == hd, lse[hd * qb:(hd + 1) * qb], l_acc)
            rows = pl.ds(j * qb * dil + r, qb, stride=dil) if dil > 1 else slice(j * qb, (j + 1) * qb)
            for hf in range(DIL_OUT // LANES):
                o_ref[0, hf, rows, :] = o_acc[:, hf * LANES:(hf + 1) * LANES]
                lse_ref[0, hf, rows, :] = l_acc[:, hf * LANES:(hf + 1) * LANES]


def _dilated_group(qd, kd, vd, batch, seq, g, dil):
    L = seq // dil
    tq = min(L, DIL_STEP_TOKENS // dil)
    nq = L // tq
    hb = tq // LANES
    nhb = L // LANES
    cur = lambda b, i: (b, 0, i, 0)
    prev = lambda b, i: (b, 0, jnp.maximum(i * hb - 1, 0), 0)
    nxt = lambda b, i: (b, 0, jnp.minimum((i + 1) * hb, nhb - 1), 0)
    blk = (1, dil, tq, DIL_OUT)
    halo = (1, dil, LANES, DIL_OUT)
    halves = DIL_OUT // LANES
    out_blk = (1, halves, tq * dil, LANES)
    out_map = lambda b, i: (b, 0, i, 0)
    o, lse = pl.pallas_call(
        functools.partial(_dil_kernel, tq=tq, dil=dil, sub_len=L),
        grid=(batch, nq),
        in_specs=[pl.BlockSpec(blk, cur),
                  pl.BlockSpec(halo, prev), pl.BlockSpec(blk, cur), pl.BlockSpec(halo, nxt),
                  pl.BlockSpec(halo, prev), pl.BlockSpec(blk, cur), pl.BlockSpec(halo, nxt)],
        out_specs=[pl.BlockSpec(out_blk, out_map), pl.BlockSpec(out_blk, out_map)],
        out_shape=[jax.ShapeDtypeStruct((batch, halves, seq, LANES), F32)] * 2,
        compiler_params=_cparams(("parallel", "parallel")), name=f"dilated_g{g}",
    )(qd, kd, kd, kd, vd, vd, vd)
    return o, lse


def _mla_kernel(q_ref, k_ref, v_ref, o_ref, m_sc, acc_sc, *, tk, nk, unroll):
    nc = tk // LANES
    m_sc[...] = jnp.full(m_sc.shape, NEG, F32)
    acc_sc[...] = jnp.zeros(acc_sc.shape, F32)

    def step(off, hh):
        sl = slice(hh * LANES, (hh + 1) * LANES)
        q = q_ref[0, :, sl]
        k = k_ref[0, pl.ds(off, tk), sl]
        v = v_ref[0, pl.ds(off, tk), sl]
        s = lax.dot_general(q, k, (((1,), (1,)), ((), ())), preferred_element_type=F32)
        cols = [s[:, c * LANES:(c + 1) * LANES] for c in range(nc)]
        m_old = m_sc[hh]
        m_new = jnp.maximum(m_old, jnp.max(functools.reduce(jnp.maximum, cols), axis=-1, keepdims=True))
        p = jnp.concatenate([jnp.exp2(c - m_new) for c in cols], axis=1).astype(BF16)
        acc_sc[hh] = acc_sc[hh] * jnp.exp2(m_old - m_new) + jnp.dot(p, v, preferred_element_type=F32)
        m_sc[hh] = m_new

    def body(j, carry):
        for u in range(unroll):
            off = pl.multiple_of((j * unroll + u) * tk, tk)
            for hh in range(2):
                step(off, hh)
        return carry

    lax.fori_loop(0, nk // unroll, body, 0)
    outs = []
    for hh in range(2):
        acc = acc_sc[hh]
        outs.append(acc * (1.0 / acc[:, MLA_V:MLA_V + 1]))
    lane = lax.broadcasted_iota(jnp.int32, (1, LANES), 1)
    o_ref[0] = jnp.where(lane < MLA_V, outs[0], pltpu.roll(outs[1], MLA_V, 1)).astype(o_ref.dtype)


def _mla(qm, km, vm, batch, seq, tq=1024, tk=2048, unroll=1):
    tq = min(seq, tq)
    tk = min(seq, tk)
    nk = seq // tk
    unroll = unroll if nk % unroll == 0 else 1
    q3 = qm.reshape(batch, seq, MLA_PAD)
    k3 = km.reshape(batch, seq, MLA_PAD)
    v3 = vm.reshape(batch, seq, MLA_PAD)
    resident = lambda b, h, i: (b, 0, h)
    kv_bytes = seq * 2 * LANES * 2
    kv_mode = dict(pipeline_mode=pl.Buffered(1)) if kv_bytes > MLA_KV_DOUBLE_BUFFER_MAX else {}
    o = pl.pallas_call(
        functools.partial(_mla_kernel, tk=tk, nk=nk, unroll=unroll),
        grid=(batch, MLA_HEADS // 2, seq // tq),
        in_specs=[pl.BlockSpec((1, tq, 2 * LANES), lambda b, h, i: (b, i, h)),
                  pl.BlockSpec((1, seq, 2 * LANES), resident, **kv_mode),
                  pl.BlockSpec((1, seq, 2 * LANES), resident, **kv_mode)],
        out_specs=pl.BlockSpec((1, tq, LANES), lambda b, h, i: (b, i, h)),
        out_shape=jax.ShapeDtypeStruct((batch, seq, MLA_OUT), BF16),
        scratch_shapes=[pltpu.VMEM((2, tq, LANES), F32), pltpu.VMEM((2, tq, LANES), F32)],
        compiler_params=_cparams(("parallel", "parallel", "arbitrary")), name="mla",
    )(q3, k3, v3)
    return o.reshape(batch * seq, MLA_OUT)


def _outproj_kernel(x_ref, o0_ref, l0_ref, o1_ref, l1_ref, o2_ref, l2_ref, ob_ref, gate_ref,
                    wpa_ref, wpb_ref, wo_ref, g2_ref, wrh_ref, wrl_ref, br_ref,
                    xm_ref, h2_ref, route_ref, cnt_ref, cnt_sc):
    def halves(ref):
        return jnp.concatenate([ref[0, hf] for hf in range(ref.shape[1])], axis=1)

    l0, l1, l2 = halves(l0_ref), halves(l1_ref), halves(l2_ref)
    m = jnp.maximum(jnp.maximum(l0, l1), l2)
    w0, w1, w2 = jnp.exp(l0 - m), jnp.exp(l1 - m), jnp.exp(l2 - m)
    oa = (w0 * halves(o0_ref) + w1 * halves(o1_ref) + w2 * halves(o2_ref)) / (w0 + w1 + w2)
    pa = jnp.dot(oa.astype(BF16), wpa_ref[...], preferred_element_type=F32)
    pb = jnp.dot(ob_ref[...], wpb_ref[...], preferred_element_type=F32)
    merged = gate_ref[:, :D_MODEL] * pa + gate_ref[:, D_MODEL:] * pb
    xm = x_ref[...] + jnp.dot(merged.astype(BF16), wo_ref[...], preferred_element_type=F32)
    xm_ref[...] = xm
    h2 = _rms(xm, g2_ref[...])
    h2_ref[...] = h2

    hi = h2.astype(BF16)
    lo = (h2 - hi.astype(F32)).astype(BF16)
    lg = (jnp.dot(hi, wrh_ref[...], preferred_element_type=F32)
          + jnp.dot(lo, wrh_ref[...], preferred_element_type=F32)
          + jnp.dot(hi, wrl_ref[...], preferred_element_type=F32)) + br_ref[...]
    lane = lax.broadcasted_iota(jnp.int32, lg.shape, 1)
    lanef = lane.astype(F32)
    big = float(ROUTE_LANES)
    gmask = (lane >= N_EXPERTS) & (lane < N_EXPERTS + N_GROUPS)
    gl = jnp.where(gmask, lg, NEG)
    gmax = jnp.max(gl, axis=-1, keepdims=True)
    gidx = jnp.min(jnp.where(gl == gmax, lanef, big), axis=-1, keepdims=True) - float(N_EXPERTS)
    p_grp = 1.0 / jnp.sum(jnp.where(gmask, jnp.exp(gl - gmax), 0.0), axis=-1, keepdims=True)
    emask = (lane // EXPERTS_PER_GROUP).astype(F32) == gidx
    el = jnp.where(emask, lg, NEG)
    m1 = jnp.max(el, axis=-1, keepdims=True)
    i1 = jnp.min(jnp.where(el == m1, lanef, big), axis=-1, keepdims=True)
    el2 = jnp.where(lanef == i1, NEG, el)
    m2 = jnp.max(el2, axis=-1, keepdims=True)
    i2 = jnp.min(jnp.where(el2 == m2, lanef, big), axis=-1, keepdims=True)
    r = jnp.exp(m2 - m1)
    wa = p_grp / (1.0 + r)
    wb = p_grp * r / (1.0 + r)
    @pl.when(pl.program_id(0) == 0)
    def _():
        cnt_sc[...] = jnp.zeros(cnt_sc.shape, F32)

    oh1 = lanef == i1
    oh2 = lanef == i2
    oh = jnp.where(oh1 | oh2, 1.0, 0.0)
    tm = lg.shape[0]
    earlier = (lax.broadcasted_iota(jnp.int32, (tm, tm), 0) > lax.broadcasted_iota(jnp.int32, (tm, tm), 1))
    before = jnp.dot(jnp.where(earlier, 1.0, 0.0).astype(BF16), oh.astype(BF16),
                     preferred_element_type=F32) + cnt_sc[0:1, :]
    r1 = jnp.sum(jnp.where(oh1, before, 0.0), axis=-1, keepdims=True)
    r2 = jnp.sum(jnp.where(oh2, before, 0.0), axis=-1, keepdims=True)
    cnt = cnt_sc[...] + jnp.sum(oh, axis=0, keepdims=True)
    cnt_sc[...] = cnt
    cnt_ref[...] = cnt
    route_ref[...] = jnp.where(lane == 0, i1, jnp.where(lane == 1, i2, jnp.where(lane == 2, wa,
                               jnp.where(lane == 3, wb, jnp.where(lane == 4, r1,
                                                                  jnp.where(lane == 5, r2, 0.0))))))


def _outproj(x, dil_outs, ob, gates, lw, seq, tm):
    T = x.shape[0]
    nt = seq // tm
    row = lambda i: (i, 0)
    const = lambda i: (0, 0)
    weights = [lw['wpa'], lw['wpb'], lw['wo'], lw['g2'], lw['wrh'], lw['wrl'], lw['br']]
    acts = [x] + list(dil_outs) + [ob, gates]
    dil_spec = pl.BlockSpec((1, DIL_OUT // LANES, tm, LANES), lambda i: (i // nt, 0, i % nt, 0))
    in_specs = [pl.BlockSpec((tm, D_MODEL), row)] + [dil_spec] * len(dil_outs) \
        + [pl.BlockSpec((tm, ob.shape[1]), row), pl.BlockSpec((tm, gates.shape[1]), row)] \
        + [pl.BlockSpec(a.shape, const, pipeline_mode=pl.Buffered(1)) for a in weights]
    return pl.pallas_call(
        _outproj_kernel, grid=(T // tm,), in_specs=in_specs,
        out_specs=[pl.BlockSpec((tm, D_MODEL), row), pl.BlockSpec((tm, D_MODEL), row),
                   pl.BlockSpec((tm, ROUTE_LANES), row), pl.BlockSpec((8, ROUTE_LANES), const)],
        out_shape=[jax.ShapeDtypeStruct((T, D_MODEL), F32), jax.ShapeDtypeStruct((T, D_MODEL), F32),
                   jax.ShapeDtypeStruct((T, ROUTE_LANES), F32), jax.ShapeDtypeStruct((8, ROUTE_LANES), F32)],
        scratch_shapes=[pltpu.VMEM((8, ROUTE_LANES), F32)],
        compiler_params=_cparams(("arbitrary",)), name="outproj_router",
    )(*acts, *weights)


GATHER_WINDOW = 32


def _gather_rows(src, idx):
    m = idx.shape[0]
    width = src.shape[1]
    info = plsc.get_sparse_core_info()
    n_workers = info.num_cores * info.num_subcores
    per_w = m // n_workers
    n_pairs = per_w // (2 * GATHER_WINDOW)
    assert n_pairs * 2 * GATHER_WINDOW * n_workers == m
    mesh = plsc.VectorSubcoreMesh(core_axis_name="core", subcore_axis_name="subcore")

    @functools.partial(
        pl.kernel, out_type=jax.ShapeDtypeStruct((m, width), src.dtype), mesh=mesh, name="gather_rows",
        scratch_types=[pltpu.VMEM((per_w,), jnp.int32),
                       pltpu.VMEM((2, GATHER_WINDOW, width), src.dtype),
                       pltpu.SemaphoreType.DMA((2,))])
    def gather(src_hbm, idx_hbm, out_hbm, idx_v, rows_v, sems):
        wid = lax.axis_index("subcore") * info.num_cores + lax.axis_index("core")
        base = wid * per_w
        pltpu.sync_copy(idx_hbm.at[pl.ds(base, per_w)], idx_v)

        def fetch(c, b):
            return pltpu.make_async_copy(src_hbm.at[idx_v.at[pl.ds(c * GATHER_WINDOW, GATHER_WINDOW)]],
                                         rows_v.at[b], sems.at[b])

        def flush(c, b):
            pltpu.sync_copy(rows_v.at[b], out_hbm.at[pl.ds(base + c * GATHER_WINDOW, GATHER_WINDOW)])

        fetch(0, 0).start()

        @pl.loop(0, n_pairs)
        def _(p):
            c = 2 * p
            fetch(c + 1, 1).start()
            fetch(c, 0).wait()
            flush(c, 0)

            @pl.when(p + 1 < n_pairs)
            def _():
                fetch(c + 2, 0).start()

            fetch(c + 1, 1).wait()
            flush(c + 1, 1)

    return gather(src, idx)


EXPERT_BLOCK = 256


def _expert_kernel(be_ref, bv_ref, bf_ref, bio_ref, xs_ref, wg_ref, wu_ref, wd_ref, y_ref,
                   wg_sc, wu_sc, wd_sc):
    i = pl.program_id(0)

    @pl.when(bf_ref[i] > 0)
    def _():
        wg_sc[...] = wg_ref[0].astype(BF16)
        wu_sc[...] = wu_ref[0].astype(BF16)
        wd_sc[...] = wd_ref[0].astype(BF16)

    @pl.when(bv_ref[i] > 0)
    def _():
        xb = xs_ref[...].astype(BF16)
        g = jnp.dot(xb, wg_sc[...], preferred_element_type=F32)
        u = jnp.dot(xb, wu_sc[...], preferred_element_type=F32)
        hb = (g * jax.nn.sigmoid(g) * u).astype(BF16)
        y_ref[...] = jnp.dot(hb, wd_sc[...], preferred_element_type=F32)


def _experts(xs, blk_expert, blk_valid, blk_first, blk_io, wg, wu, wd, layer):
    n_slots = xs.shape[0]
    nb = n_slots // EXPERT_BLOCK
    grid_spec = pltpu.PrefetchScalarGridSpec(
        num_scalar_prefetch=4, grid=(nb,),
        in_specs=[pl.BlockSpec((EXPERT_BLOCK, D_MODEL), lambda i, be, bv, bf, bio: (bio[i], 0)),
                  pl.BlockSpec((None, 1, D_MODEL, D_EXPERT), lambda i, be, bv, bf, bio: (layer, be[i], 0, 0)),
                  pl.BlockSpec((None, 1, D_MODEL, D_EXPERT), lambda i, be, bv, bf, bio: (layer, be[i], 0, 0)),
                  pl.BlockSpec((None, 1, D_EXPERT, D_MODEL), lambda i, be, bv, bf, bio: (layer, be[i], 0, 0))],
        out_specs=pl.BlockSpec((EXPERT_BLOCK, D_MODEL), lambda i, be, bv, bf, bio: (bio[i], 0)),
        scratch_shapes=[pltpu.VMEM((D_MODEL, D_EXPERT), BF16), pltpu.VMEM((D_MODEL, D_EXPERT), BF16),
                        pltpu.VMEM((D_EXPERT, D_MODEL), BF16)])
    return pl.pallas_call(
        _expert_kernel, grid_spec=grid_spec,
        out_shape=jax.ShapeDtypeStruct((n_slots, D_MODEL), F32),
        compiler_params=_cparams(("arbitrary",)), name="experts",
    )(blk_expert, blk_valid, blk_first, blk_io, xs, wg, wu, wd)


def _combine_kernel(x_ref, ya_ref, yb_ref, route_ref, g_ref, o_ref, *, final):
    route = route_ref[...]
    wa = route[:, 2:3]
    wb = route[:, 3:4]
    x = x_ref[...] + (ya_ref[...] * wa + yb_ref[...] * wb)
    if final:
        x = _rms(x, g_ref[...])
    o_ref[...] = x


def _combine(x, yg, route, g, final, tm):
    T = x.shape[0]
    nt = T // tm
    return pl.pallas_call(
        functools.partial(_combine_kernel, final=final), grid=(nt,),
        in_specs=[pl.BlockSpec((tm, D_MODEL), lambda i: (i, 0)),
                  pl.BlockSpec((tm, D_MODEL), lambda i: (i, 0)),
                  pl.BlockSpec((tm, D_MODEL), lambda i: (i + nt, 0)),
                  pl.BlockSpec((tm, ROUTE_LANES), lambda i: (i, 0)),
                  pl.BlockSpec((1, D_MODEL), lambda i: (0, 0))],
        out_specs=pl.BlockSpec((tm, D_MODEL), lambda i: (i, 0)),
        out_shape=jax.ShapeDtypeStruct((T, D_MODEL), F32),
        compiler_params=_cparams(("parallel",)), name="combine",
    )(x, yg, yg, route, g)


def _slots_kernel(route_ref, start_ref, slot_ref):
    route = route_ref[...]
    lanef = lax.broadcasted_iota(jnp.int32, route.shape, 1).astype(F32)
    lane = lax.broadcasted_iota(jnp.int32, route.shape, 1)
    start = start_ref[...]

    def slot(k):
        first = jnp.sum(jnp.where(lanef == route[:, k:k + 1], start, 0.0), axis=-1, keepdims=True)
        return first + route[:, 4 + k:5 + k]

    slab = jnp.where(lane == 0, slot(0), jnp.where(lane == 1, slot(1), 0.0))
    slot_ref[...] = jnp.transpose(slab)[0:8, :]


def _slots(route, pad_start, tm):
    T = route.shape[0]
    start = jnp.pad(pad_start.astype(F32), (0, ROUTE_LANES - N_EXPERTS))[None, :]
    out = pl.pallas_call(
        _slots_kernel, grid=(T // tm,),
        in_specs=[pl.BlockSpec((tm, ROUTE_LANES), lambda i: (i, 0)),
                  pl.BlockSpec((1, ROUTE_LANES), lambda i: (0, 0))],
        out_specs=pl.BlockSpec((8, tm), lambda i: (0, i)),
        out_shape=jax.ShapeDtypeStruct((8, T), F32),
        compiler_params=_cparams(("parallel",)), name="slots",
    )(route, start)
    return out[0:TOP_K].astype(jnp.int32)


def _dispatch_plan(route, cnt, T, tm):
    A = T * TOP_K
    counts = cnt[0, :N_EXPERTS].astype(jnp.int32)
    padded = (counts + EXPERT_BLOCK - 1) // EXPERT_BLOCK * EXPERT_BLOCK
    pad_end = jnp.cumsum(padded)
    pad_start = pad_end - padded
    slot = _slots(route, pad_start, tm)
    n_blocks = A // EXPERT_BLOCK + N_EXPERTS
    n_slots = n_blocks * EXPERT_BLOCK
    tok = jnp.broadcast_to(jnp.arange(T, dtype=jnp.int32)[None, :], (TOP_K, T))
    blk_start = jnp.arange(n_blocks, dtype=jnp.int32) * EXPERT_BLOCK
    blk_valid = (blk_start < pad_end[-1]).astype(jnp.int32)
    blk_io = jnp.minimum(jnp.arange(n_blocks, dtype=jnp.int32), jnp.sum(blk_valid) - 1)
    blk_expert = jnp.minimum(jnp.sum((blk_start[:, None] >= pad_end[None, :]).astype(jnp.int32), axis=1),
                             N_EXPERTS - 1)[blk_io]
    blk_first = jnp.concatenate([jnp.ones((1,), jnp.int32),
                                 (blk_expert[1:] != blk_expert[:-1]).astype(jnp.int32)])
    _, tok_sorted = lax.sort_key_val(slot.reshape(A), tok.reshape(A))
    shift = pad_start - (jnp.cumsum(counts) - counts)
    src = (blk_start - shift[blk_expert])[:, None] + jnp.arange(EXPERT_BLOCK, dtype=jnp.int32)[None, :]
    slot_tok = tok_sorted[src % A].reshape(n_slots)
    comb_idx = slot.reshape(A)
    return slot_tok, blk_expert, blk_valid, blk_first, blk_io, comb_idx


def _rope_tables(seq, dim, period, first):
    half = dim // 2
    inv = ROPE_THETA ** (-jnp.arange(0, dim, 2, dtype=F32) / dim)
    ang = jnp.arange(seq, dtype=F32)[:, None] * inv[None, :]
    cos, sin = jnp.cos(ang), jnp.sin(ang)
    d = jnp.arange(LANES) % period - first
    in_a = (d >= 0) & (d < half)
    in_b = (d >= half) & (d < dim)
    idx = jnp.clip(jnp.where(in_b, d - half, d), 0, half - 1)
    c = jnp.where((in_a | in_b)[None, :], cos[:, idx], 1.0)
    sa = jnp.where(in_a[None, :], -sin[:, idx], 0.0)
    sb = jnp.where(in_b[None, :], sin[:, idx], 0.0)
    return c, sa, sb


def _pad_heads(w, width):
    k = w.shape[0]
    w = w.reshape(k, MLA_HEADS, width)
    return jnp.pad(w, ((0, 0), (0, 0), (0, MLA_SLOT - width))).reshape(k, MLA_PAD)


def _prep_layer(l, norm1_g, w_in, q_norm_g, w_uq, kv_norm_g, w_uk, w_uv, w_pa, w_pb, w_o,
                norm2_g, w_rg, b_rg, w_re, b_re, w_e_gate, w_e_up, w_e_down):
    w = w_in[l]
    o = 0
    parts = []
    for n in (DIL_WIDTH, DIL_WIDTH, DIL_WIDTH, Q_LORA, KV_LORA, MLA_ROPE, 2 * D_MODEL):
        parts.append(w[:, o:o + n])
        o += n
    wq, wk, wv, wcq, wckv, wkr, wg = parts
    wkr_pad = jnp.pad(wkr, ((0, 0), (MLA_NOPE, MLA_SLOT - MLA_NOPE - MLA_ROPE)))
    wr = jnp.pad(jnp.concatenate([w_re[l], w_rg[l]], axis=1),
                 ((0, 0), (0, ROUTE_LANES - N_EXPERTS - N_GROUPS)))
    wrh = wr.astype(BF16)
    wrl = (wr - wrh.astype(F32)).astype(BF16)
    br = jnp.pad(jnp.concatenate([b_re[l], b_rg[l]]), (0, ROUTE_LANES - N_EXPERTS - N_GROUPS))
    return dict(
        g1=norm1_g[l][None, :], wq=wq.astype(BF16), wk=wk.astype(BF16), wv=wv.astype(BF16),
        wcq=wcq.astype(BF16), wckv=wckv.astype(BF16), wkr=wkr_pad.astype(BF16), wg=wg.astype(BF16),
        qn=q_norm_g[l][None, :], kvn=kv_norm_g[l][None, :],
        wuq=_pad_heads(w_uq[l], MLA_QK).astype(BF16), wuk=_pad_heads(w_uk[l], MLA_NOPE).astype(BF16),
        wuv=_pad_heads(w_uv[l], MLA_V).astype(BF16),
        wpa=w_pa[l].astype(BF16), wpb=w_pb[l].astype(BF16), wo=w_o[l].astype(BF16),
        g2=norm2_g[l][None, :], wrh=wrh, wrl=wrl, br=br[None, :].astype(F32),
        weg=w_e_gate, weu=w_e_up, wed=w_e_down)


def _trunk(x3, layers, final_g, tm=512):
    batch, seq, _ = x3.shape
    T = batch * seq
    x = x3.reshape(T, D_MODEL)
    tabs = _rope_tables(seq, ROT_DIM, HEAD_DIM, 0) + _rope_tables(seq, MLA_ROPE, MLA_SLOT, MLA_NOPE)
    fg = final_g[None, :]
    for l, lw in enumerate(layers):
        outs = _inproj(x, lw, tabs, seq, tm)
        n_g = len(DIL_GROUPS)
        qd, kd, vd = outs[0:n_g], outs[n_g:2 * n_g], outs[2 * n_g:3 * n_g]
        qm, km, vm, gates = outs[3 * n_g:]
        dil = []
        for g, (_, d) in enumerate(DIL_GROUPS):
            dil.extend(_dilated_group(qd[g], kd[g], vd[g], batch, seq, g, d))
        ob = _mla(qm, km, vm, batch, seq)
        xm, h2, route, cnt = _outproj(x, dil, ob, gates, lw, seq, tm)
        slot_tok, blk_expert, blk_valid, blk_first, blk_io, comb_idx = _dispatch_plan(route, cnt, T, tm)
        xs = _gather_rows(h2, slot_tok)
        ys = _experts(xs, blk_expert, blk_valid, blk_first, blk_io, lw['weg'], lw['weu'], lw['wed'], l)
        yg = _gather_rows(ys, comb_idx)
        x = _combine(xm, yg, route, fg, l == len(layers) - 1, tm)
    return x.reshape(batch, seq, D_MODEL)


def kernel(x_prompt, x_sample, norm1_g, w_in, q_norm_g, w_uq, kv_norm_g, w_uk, w_uv, w_pa, w_pb, w_o,
           norm2_g, w_rg, b_rg, w_re, b_re, w_e_gate, w_e_up, w_e_down, final_g):
    layers = [_prep_layer(l, norm1_g, w_in, q_norm_g, w_uq, kv_norm_g, w_uk, w_uv, w_pa, w_pb, w_o,
                          norm2_g, w_rg, b_rg, w_re, b_re, w_e_gate, w_e_up, w_e_down)
              for l in range(DEPTH)]
    return (_trunk(x_prompt, layers, final_g), _trunk(x_sample, layers, final_g))
```

```python
import functools

import jax
import jax.numpy as jnp
from jax import lax
from jax.experimental import pallas as pl
from jax.experimental.pallas import tpu as pltpu
from jax.experimental.pallas import tpu_sc as plsc

D_MODEL = 1024
DEPTH = 2
HEAD_DIM = 64
DIL_GROUPS = ((128, 1), (512, 4), (2048, 16))
DIL_HEADS_PER_GROUP = 4
DIL_WIDTH = 768
DIL_OUT = 256
DIL_SCALE = HEAD_DIM ** -0.5
ROT_DIM = 16
ROPE_THETA = 500000.0
DIL_HALF = 64

MLA_HEADS = 8
MLA_NOPE = 64
MLA_ROPE = 32
MLA_QK = 96
MLA_V = 64
MLA_OUT = 512
MLA_SCALE = MLA_QK ** -0.5
Q_LORA = 256
KV_LORA = 128

N_GROUPS = 8
EXPERTS_PER_GROUP = 8
N_EXPERTS = 64
TOP_K = 2
D_EXPERT = 512
EPS = 1e-6

LANES = 128
MLA_SLOT = LANES
MLA_PAD = MLA_HEADS * MLA_SLOT
ROUTE_LANES = LANES
VMEM_LIMIT = 56 * 1024 * 1024
MLA_KV_DOUBLE_BUFFER_MAX = 2 * 1024 * 1024

BF16 = jnp.bfloat16
F32 = jnp.float32
NEG = -1e30
LOG2E = 1.4426950408889634


def _cparams(sem):
    return pltpu.CompilerParams(dimension_semantics=sem, vmem_limit_bytes=VMEM_LIMIT)


def _rms(t, g):
    return t * lax.rsqrt(jnp.mean(t * t, axis=-1, keepdims=True) + EPS) * g


def _rope_chunk(t, c, sa, sb, shift):
    return t * c + pltpu.roll(t, LANES - shift, 1) * sa + pltpu.roll(t, shift, 1) * sb


def _inproj_kernel(x_ref, g1_ref, wq_ref, wk_ref, wv_ref, wcq_ref, wckv_ref, wkr_ref, wg_ref,
                   qn_ref, kvn_ref, wuq_ref, wuk_ref, wuv_ref,
                   ca_ref, saa_ref, sab_ref, cb_ref, sba_ref, sbb_ref,
                   q0_ref, q1_ref, q2_ref, k0_ref, k1_ref, k2_ref, v0_ref, v1_ref, v2_ref,
                   qm_ref, km_ref, vm_ref, gate_ref, q_sc, k_sc, v_sc):
    x = x_ref[...]
    h = _rms(x, g1_ref[...]).astype(BF16)
    tm = x.shape[0]

    ca, saa, sab = ca_ref[...], saa_ref[...], sab_ref[...]
    q = jnp.dot(h, wq_ref[...], preferred_element_type=F32)
    k = jnp.dot(h, wk_ref[...], preferred_element_type=F32)
    v = jnp.dot(h, wv_ref[...], preferred_element_type=F32)
    for j in range(DIL_WIDTH // LANES):
        sl = slice(j * LANES, (j + 1) * LANES)
        q_sc[j] = _rope_chunk(q[:, sl], ca, saa, sab, ROT_DIM // 2) * DIL_SCALE
        k_sc[j] = _rope_chunk(k[:, sl], ca, saa, sab, ROT_DIM // 2)
        v_sc[j] = v[:, sl]
    halves = DIL_OUT // LANES
    for src, outs in ((q_sc, (q0_ref, q1_ref, q2_ref)), (k_sc, (k0_ref, k1_ref, k2_ref)),
                      (v_sc, (v0_ref, v1_ref, v2_ref))):
        for g, (_, dil) in enumerate(DIL_GROUPS):
            for hf in range(halves):
                for r in range(dil):
                    rows = pl.ds(r, tm // dil, stride=dil) if dil > 1 else slice(None)
                    outs[g][0, r, :, hf * LANES:(hf + 1) * LANES] = src[g * halves + hf, rows, :].astype(BF16)

    cb, sba, sbb = cb_ref[...], sba_ref[...], sbb_ref[...]
    cq = jnp.dot(h, wcq_ref[...], preferred_element_type=F32)
    cqn = _rms(cq, qn_ref[...]).astype(BF16)
    qm = jnp.dot(cqn, wuq_ref[...], preferred_element_type=F32) * (MLA_SCALE * LOG2E)
    ckv = jnp.dot(h, wckv_ref[...], preferred_element_type=F32)
    c = _rms(ckv, kvn_ref[...]).astype(BF16)
    kn = jnp.dot(c, wuk_ref[...], preferred_element_type=F32)
    vv = jnp.dot(c, wuv_ref[...], preferred_element_type=F32)
    kr = jnp.dot(h, wkr_ref[...], preferred_element_type=F32)
    kr = _rope_chunk(kr, cb, sba, sbb, MLA_ROPE // 2)
    lane = lax.broadcasted_iota(jnp.int32, (1, LANES), 1)
    ones_col = jnp.where(lane == MLA_V, 1.0, 0.0).astype(F32)
    for j in range(MLA_HEADS):
        sl = slice(j * LANES, (j + 1) * LANES)
        qm_ref[:, sl] = _rope_chunk(qm[:, sl], cb, sba, sbb, MLA_ROPE // 2).astype(BF16)
        km_ref[:, sl] = (kn[:, sl] + kr).astype(BF16)
        vm_ref[:, sl] = (vv[:, sl] + ones_col).astype(BF16)

    gate_ref[...] = jax.nn.sigmoid(jnp.dot(h, wg_ref[...], preferred_element_type=F32))


def _inproj(x, lw, tabs, seq, tm):
    T = x.shape[0]
    nt = seq // tm
    row = lambda i: (i, 0)
    const = lambda i: (0, 0)
    tab = lambda i: (i % nt, 0)

    def wspec(a):
        return pl.BlockSpec(a.shape, const, pipeline_mode=pl.Buffered(1))

    weights = [lw['g1'], lw['wq'], lw['wk'], lw['wv'], lw['wcq'], lw['wckv'], lw['wkr'], lw['wg'],
               lw['qn'], lw['kvn'], lw['wuq'], lw['wuk'], lw['wuv']]
    in_specs = ([pl.BlockSpec((tm, D_MODEL), row)] + [wspec(a) for a in weights]
                + [pl.BlockSpec((tm, LANES), tab)] * 6)
    batch = T // seq
    res_map = lambda i: (i // nt, 0, i % nt, 0)
    dil_shape = [jax.ShapeDtypeStruct((batch, d, seq // d, DIL_OUT), BF16) for _, d in DIL_GROUPS] * 3
    dil_specs = [pl.BlockSpec((1, d, tm // d, DIL_OUT), res_map) for _, d in DIL_GROUPS] * 3
    out_shape = dil_shape + [jax.ShapeDtypeStruct((T, MLA_PAD), BF16)] * 3 \
        + [jax.ShapeDtypeStruct((T, 2 * D_MODEL), F32)]
    out_specs = dil_specs + [pl.BlockSpec((tm, MLA_PAD), row)] * 3 + [pl.BlockSpec((tm, 2 * D_MODEL), row)]
    return pl.pallas_call(
        _inproj_kernel, grid=(T // tm,), in_specs=in_specs, out_specs=out_specs, out_shape=out_shape,
        scratch_shapes=[pltpu.VMEM((DIL_WIDTH // LANES, tm, LANES), F32)] * 3,
        compiler_params=_cparams(("parallel",)), name="inproj",
    )(x, *weights, *tabs)


DIL_STEP_TOKENS = 2048


def _dil_kernel(q_ref, kp_ref, kc_ref, kn_ref, vp_ref, vc_ref, vn_ref, o_ref, lse_ref, *, tq, dil, sub_len):
    i = pl.program_id(1)
    qb = LANES
    nh = DIL_HEADS_PER_GROUP
    head = lax.broadcasted_iota(jnp.int32, (1, DIL_OUT), 1) // HEAD_DIM
    qloc = lax.broadcasted_iota(jnp.int32, (nh * qb, 2 * qb), 0) % qb
    rel = lax.broadcasted_iota(jnp.int32, (nh * qb, 2 * qb), 1) - DIL_HALF - qloc
    band = jnp.abs(rel) <= DIL_HALF
    kcol = lax.broadcasted_iota(jnp.int32, (1, 2 * qb), 1) - DIL_HALF

    def window(p_ref, c_ref, n_ref, r, j):
        lo, hi = j * qb - DIL_HALF, j * qb + 2 * qb - DIL_HALF
        parts = []
        if lo < 0:
            parts.append(p_ref[0, r, qb + lo:qb, :])
            lo = 0
        parts.append(c_ref[0, r, lo:min(hi, tq), :])
        if hi > tq:
            parts.append(n_ref[0, r, 0:hi - tq, :])
        return jnp.concatenate(parts, axis=0) if len(parts) > 1 else parts[0]

    for r in range(dil):
        for j in range(tq // qb):
            q = q_ref[0, r, j * qb:(j + 1) * qb, :]
            k = window(kp_ref, kc_ref, kn_ref, r, j)
            v = window(vp_ref, vc_ref, vn_ref, r, j)
            qs = jnp.concatenate([jnp.where(head == hd, q, jnp.zeros_like(q)) for hd in range(nh)], axis=0)
            s = lax.dot_general(qs, k, (((1,), (1,)), ((), ())), preferred_element_type=F32)
            kpos = kcol + (i * tq + j * qb)
            s = jnp.where(band & (kpos >= 0) & (kpos < sub_len), s, NEG)
            m = jnp.max(s, axis=-1, keepdims=True)
            p = jnp.exp(s - m)
            den = jnp.sum(p, axis=-1, keepdims=True)
            oh = jnp.dot(p.astype(BF16), v, preferred_element_type=F32) * (1.0 / den)
            lse = m + jnp.log(den)
            o_acc = oh[0:qb]
            l_acc = jnp.broadcast_to(lse[0:qb], (qb, DIL_OUT))
            for hd in range(1, nh):
                o_acc = jnp.where(head == hd, oh[hd * qb:(hd + 1) * qb], o_acc)
                l_acc = jnp.where(head == hd, lse[hd * qb:(hd + 1) * qb], l_acc)
            rows = pl.ds(j * qb * dil + r, qb, stride=dil) if dil > 1 else slice(j * qb, (j + 1) * qb)
            for hf in range(DIL_OUT // LANES):
                o_ref[0, hf, rows, :] = o_acc[:, hf * LANES:(hf + 1) * LANES]
                lse_ref[0, hf, rows, :] = l_acc[:, hf * LANES:(hf + 1) * LANES]


def _dilated_group(qd, kd, vd, batch, seq, g, dil):
    L = seq // dil
    tq = min(L, DIL_STEP_TOKENS // dil)
    nq = L // tq
    hb = tq // LANES
    nhb = L // LANES
    cur = lambda b, i: (b, 0, i, 0)
    prev = lambda b, i: (b, 0, jnp.maximum(i * hb - 1, 0), 0)
    nxt = lambda b, i: (b, 0, jnp.minimum((i + 1) * hb, nhb - 1), 0)
    blk = (1, dil, tq, DIL_OUT)
    halo = (1, dil, LANES, DIL_OUT)
    halves = DIL_OUT // LANES
    out_blk = (1, halves, tq * dil, LANES)
    out_map = lambda b, i: (b, 0, i, 0)
    o, lse = pl.pallas_call(
        functools.partial(_dil_kernel, tq=tq, dil=dil, sub_len=L),
        grid=(batch, nq),
        in_specs=[pl.BlockSpec(blk, cur),
                  pl.BlockSpec(halo, prev), pl.BlockSpec(blk, cur), pl.BlockSpec(halo, nxt),
                  pl.BlockSpec(halo, prev), pl.BlockSpec(blk, cur), pl.BlockSpec(halo, nxt)],
        out_specs=[pl.BlockSpec(out_blk, out_map), pl.BlockSpec(out_blk, out_map)],
        out_shape=[jax.ShapeDtypeStruct((batch, halves, seq, LANES), F32)] * 2,
        compiler_params=_cparams(("parallel", "parallel")), name=f"dilated_g{g}",
    )(qd, kd, kd, kd, vd, vd, vd)
    return o, lse


def _mla_kernel(q_ref, k_ref, v_ref, o_ref, m_sc, acc_sc, *, tk, nk, unroll):
    nc = tk // LANES
    m_sc[...] = jnp.full(m_sc.shape, NEG, F32)
    acc_sc[...] = jnp.zeros(acc_sc.shape, F32)

    def step(off, hh):
        sl = slice(hh * LANES, (hh + 1) * LANES)
        q = q_ref[0, :, sl]
        k = k_ref[0, pl.ds(off, tk), sl]
        v = v_ref[0, pl.ds(off, tk), sl]
        s = lax.dot_general(q, k, (((1,), (1,)), ((), ())), preferred_element_type=F32)
        cols = [s[:, c * LANES:(c + 1) * LANES] for c in range(nc)]
        m_old = m_sc[hh]
        m_new = jnp.maximum(m_old, jnp.max(functools.reduce(jnp.maximum, cols), axis=-1, keepdims=True))
        p = jnp.concatenate([jnp.exp2(c - m_new) for c in cols], axis=1).astype(BF16)
        acc_sc[hh] = acc_sc[hh] * jnp.exp2(m_old - m_new) + jnp.dot(p, v, preferred_element_type=F32)
        m_sc[hh] = m_new

    def body(j, carry):
        for u in range(unroll):
            off = pl.multiple_of((j * unroll + u) * tk, tk)
            for hh in range(2):
                step(off, hh)
        return carry

    lax.fori_loop(0, nk // unroll, body, 0)
    outs = []
    for hh in range(2):
        acc = acc_sc[hh]
        outs.append(acc * (1.0 / acc[:, MLA_V:MLA_V + 1]))
    lane = lax.broadcasted_iota(jnp.int32, (1, LANES), 1)
    o_ref[0] = jnp.where(lane < MLA_V, outs[0], pltpu.roll(outs[1], MLA_V, 1)).astype(o_ref.dtype)


def _mla(qm, km, vm, batch, seq, tq=1024, tk=2048, unroll=1):
    tq = min(seq, tq)
    tk = min(seq, tk)
    nk = seq // tk
    unroll = unroll if nk % unroll == 0 else 1
    q3 = qm.reshape(batch, seq, MLA_PAD)
    k3 = km.reshape(batch, seq, MLA_PAD)
    v3 = vm.reshape(batch, seq, MLA_PAD)
    resident = lambda b, h, i: (b, 0, h)
    kv_bytes = seq * 2 * LANES * 2
    kv_mode = dict(pipeline_mode=pl.Buffered(1)) if kv_bytes > MLA_KV_DOUBLE_BUFFER_MAX else {}
    o = pl.pallas_call(
        functools.partial(_mla_kernel, tk=tk, nk=nk, unroll=unroll),
        grid=(batch, MLA_HEADS // 2, seq // tq),
        in_specs=[pl.BlockSpec((1, tq, 2 * LANES), lambda b, h, i: (b, i, h)),
                  pl.BlockSpec((1, seq, 2 * LANES), resident, **kv_mode),
                  pl.BlockSpec((1, seq, 2 * LANES), resident, **kv_mode)],
        out_specs=pl.BlockSpec((1, tq, LANES), lambda b, h, i: (b, i, h)),
        out_shape=jax.ShapeDtypeStruct((batch, seq, MLA_OUT), BF16),
        scratch_shapes=[pltpu.VMEM((2, tq, LANES), F32), pltpu.VMEM((2, tq, LANES), F32)],
        compiler_params=_cparams(("parallel", "parallel", "arbitrary")), name="mla",
    )(q3, k3, v3)
    return o.reshape(batch * seq, MLA_OUT)


def _outproj_kernel(x_ref, o0_ref, l0_ref, o1_ref, l1_ref, o2_ref, l2_ref, ob_ref, gate_ref,
                    wpa_ref, wpb_ref, wo_ref, g2_ref, wrh_ref, wrl_ref, br_ref,
                    xm_ref, route_ref, cnt_ref, cnt_sc):
    def halves(ref):
        return jnp.concatenate([ref[0, hf] for hf in range(ref.shape[1])], axis=1)

    l0, l1, l2 = halves(l0_ref), halves(l1_ref), halves(l2_ref)
    m = jnp.maximum(jnp.maximum(l0, l1), l2)
    w0, w1, w2 = jnp.exp(l0 - m), jnp.exp(l1 - m), jnp.exp(l2 - m)
    oa = (w0 * halves(o0_ref) + w1 * halves(o1_ref) + w2 * halves(o2_ref)) / (w0 + w1 + w2)
    pa = jnp.dot(oa.astype(BF16), wpa_ref[...], preferred_element_type=F32)
    pb = jnp.dot(ob_ref[...], wpb_ref[...], preferred_element_type=F32)
    merged = gate_ref[:, :D_MODEL] * pa + gate_ref[:, D_MODEL:] * pb
    xm = x_ref[...] + jnp.dot(merged.astype(BF16), wo_ref[...], preferred_element_type=F32)
    xm_ref[...] = xm
    h2 = _rms(xm, g2_ref[...])

    hi = h2.astype(BF16)
    lo = (h2 - hi.astype(F32)).astype(BF16)
    lg = (jnp.dot(hi, wrh_ref[...], preferred_element_type=F32)
          + jnp.dot(lo, wrh_ref[...], preferred_element_type=F32)
          + jnp.dot(hi, wrl_ref[...], preferred_element_type=F32)) + br_ref[...]
    lane = lax.broadcasted_iota(jnp.int32, lg.shape, 1)
    lanef = lane.astype(F32)
    big = float(ROUTE_LANES)
    gmask = (lane >= N_EXPERTS) & (lane < N_EXPERTS + N_GROUPS)
    gl = jnp.where(gmask, lg, NEG)
    gmax = jnp.max(gl, axis=-1, keepdims=True)
    gidx = jnp.min(jnp.where(gl == gmax, lanef, big), axis=-1, keepdims=True) - float(N_EXPERTS)
    p_grp = 1.0 / jnp.sum(jnp.where(gmask, jnp.exp(gl - gmax), 0.0), axis=-1, keepdims=True)
    emask = (lane // EXPERTS_PER_GROUP).astype(F32) == gidx
    el = jnp.where(emask, lg, NEG)
    m1 = jnp.max(el, axis=-1, keepdims=True)
    i1 = jnp.min(jnp.where(el == m1, lanef, big), axis=-1, keepdims=True)
    el2 = jnp.where(lanef == i1, NEG, el)
    m2 = jnp.max(el2, axis=-1, keepdims=True)
    i2 = jnp.min(jnp.where(el2 == m2, lanef, big), axis=-1, keepdims=True)
    r = jnp.exp(m2 - m1)
    wa = p_grp / (1.0 + r)
    wb = p_grp * r / (1.0 + r)
    @pl.when(pl.program_id(0) == 0)
    def _():
        cnt_sc[...] = jnp.zeros(cnt_sc.shape, F32)

    oh1 = lanef == i1
    oh2 = lanef == i2
    oh = jnp.where(oh1 | oh2, 1.0, 0.0)
    tm = lg.shape[0]
    earlier = (lax.broadcasted_iota(jnp.int32, (tm, tm), 0) > lax.broadcasted_iota(jnp.int32, (tm, tm), 1))
    before = jnp.dot(jnp.where(earlier, 1.0, 0.0).astype(BF16), oh.astype(BF16),
                     preferred_element_type=F32) + cnt_sc[0:1, :]
    r1 = jnp.sum(jnp.where(oh1, before, 0.0), axis=-1, keepdims=True)
    r2 = jnp.sum(jnp.where(oh2, before, 0.0), axis=-1, keepdims=True)
    cnt = cnt_sc[...] + jnp.sum(oh, axis=0, keepdims=True)
    cnt_sc[...] = cnt
    cnt_ref[...] = cnt
    route_ref[...] = jnp.where(lane == 0, i1, jnp.where(lane == 1, i2, jnp.where(lane == 2, wa,
                               jnp.where(lane == 3, wb, jnp.where(lane == 4, r1,
                                                                  jnp.where(lane == 5, r2, 0.0))))))


def _outproj(x, dil_outs, ob, gates, lw, seq, tm):
    T = x.shape[0]
    nt = seq // tm
    row = lambda i: (i, 0)
    const = lambda i: (0, 0)
    weights = [lw['wpa'], lw['wpb'], lw['wo'], lw['g2'], lw['wrh'], lw['wrl'], lw['br']]
    acts = [x] + list(dil_outs) + [ob, gates]
    dil_spec = pl.BlockSpec((1, DIL_OUT // LANES, tm, LANES), lambda i: (i // nt, 0, i % nt, 0))
    in_specs = [pl.BlockSpec((tm, D_MODEL), row)] + [dil_spec] * len(dil_outs) \
        + [pl.BlockSpec((tm, ob.shape[1]), row), pl.BlockSpec((tm, gates.shape[1]), row)] \
        + [pl.BlockSpec(a.shape, const, pipeline_mode=pl.Buffered(1)) for a in weights]
    return pl.pallas_call(
        _outproj_kernel, grid=(T // tm,), in_specs=in_specs,
        out_specs=[pl.BlockSpec((tm, D_MODEL), row),
                   pl.BlockSpec((tm, ROUTE_LANES), row), pl.BlockSpec((8, ROUTE_LANES), const)],
        out_shape=[jax.ShapeDtypeStruct((T, D_MODEL), F32),
                   jax.ShapeDtypeStruct((T, ROUTE_LANES), F32), jax.ShapeDtypeStruct((8, ROUTE_LANES), F32)],
        scratch_shapes=[pltpu.VMEM((8, ROUTE_LANES), F32)],
        compiler_params=_cparams(("arbitrary",)), name="outproj_router",
    )(*acts, *weights)


GATHER_WINDOW = 32


def _gather_rows(src, idx):
    m = idx.shape[0]
    width = src.shape[1]
    info = plsc.get_sparse_core_info()
    n_workers = info.num_cores * info.num_subcores
    per_w = m // n_workers
    n_pairs = per_w // (2 * GATHER_WINDOW)
    assert n_pairs * 2 * GATHER_WINDOW * n_workers == m
    mesh = plsc.VectorSubcoreMesh(core_axis_name="core", subcore_axis_name="subcore")

    @functools.partial(
        pl.kernel, out_type=jax.ShapeDtypeStruct((m, width), src.dtype), mesh=mesh, name="gather_rows",
        scratch_types=[pltpu.VMEM((per_w,), jnp.int32),
                       pltpu.VMEM((2, GATHER_WINDOW, width), src.dtype),
                       pltpu.SemaphoreType.DMA((2,))])
    def gather(src_hbm, idx_hbm, out_hbm, idx_v, rows_v, sems):
        wid = lax.axis_index("subcore") * info.num_cores + lax.axis_index("core")
        base = wid * per_w
        pltpu.sync_copy(idx_hbm.at[pl.ds(base, per_w)], idx_v)

        def fetch(c, b):
            return pltpu.make_async_copy(src_hbm.at[idx_v.at[pl.ds(c * GATHER_WINDOW, GATHER_WINDOW)]],
                                         rows_v.at[b], sems.at[b])

        def flush(c, b):
            pltpu.sync_copy(rows_v.at[b], out_hbm.at[pl.ds(base + c * GATHER_WINDOW, GATHER_WINDOW)])

        fetch(0, 0).start()

        @pl.loop(0, n_pairs)
        def _(p):
            c = 2 * p
            fetch(c + 1, 1).start()
            fetch(c, 0).wait()
            flush(c, 0)

            @pl.when(p + 1 < n_pairs)
            def _():
                fetch(c + 2, 0).start()

            fetch(c + 1, 1).wait()
            flush(c + 1, 1)

    return gather(src, idx)


EXPERT_BLOCK = 512


def _expert_kernel(be_ref, bv_ref, bf_ref, bio_ref, xs_ref, g2_ref, wg_ref, wu_ref, wd_ref, y_ref,
                   wg_sc, wu_sc, wd_sc):
    i = pl.program_id(0)

    @pl.when(bf_ref[i] > 0)
    def _():
        wg_sc[...] = wg_ref[0].astype(BF16)
        wu_sc[...] = wu_ref[0].astype(BF16)
        wd_sc[...] = wd_ref[0].astype(BF16)

    @pl.when(bv_ref[i] > 0)
    def _():
        xb = _rms(xs_ref[...], g2_ref[...]).astype(BF16)
        g = jnp.dot(xb, wg_sc[...], preferred_element_type=F32)
        u = jnp.dot(xb, wu_sc[...], preferred_element_type=F32)
        hb = (g * jax.nn.sigmoid(g) * u).astype(BF16)
        y_ref[...] = jnp.dot(hb, wd_sc[...], preferred_element_type=F32)


def _experts(xs, g2, blk_expert, blk_valid, blk_first, blk_io, wg, wu, wd, layer):
    n_slots = xs.shape[0]
    nb = n_slots // EXPERT_BLOCK
    grid_spec = pltpu.PrefetchScalarGridSpec(
        num_scalar_prefetch=4, grid=(nb,),
        in_specs=[pl.BlockSpec((EXPERT_BLOCK, D_MODEL), lambda i, be, bv, bf, bio: (bio[i], 0)),
                  pl.BlockSpec((1, D_MODEL), lambda i, be, bv, bf, bio: (0, 0)),
                  pl.BlockSpec((None, 1, D_MODEL, D_EXPERT), lambda i, be, bv, bf, bio: (layer, be[i], 0, 0)),
                  pl.BlockSpec((None, 1, D_MODEL, D_EXPERT), lambda i, be, bv, bf, bio: (layer, be[i], 0, 0)),
                  pl.BlockSpec((None, 1, D_EXPERT, D_MODEL), lambda i, be, bv, bf, bio: (layer, be[i], 0, 0))],
        out_specs=pl.BlockSpec((EXPERT_BLOCK, D_MODEL), lambda i, be, bv, bf, bio: (bio[i], 0)),
        scratch_shapes=[pltpu.VMEM((D_MODEL, D_EXPERT), BF16), pltpu.VMEM((D_MODEL, D_EXPERT), BF16),
                        pltpu.VMEM((D_EXPERT, D_MODEL), BF16)])
    return pl.pallas_call(
        _expert_kernel, grid_spec=grid_spec,
        out_shape=jax.ShapeDtypeStruct((n_slots, D_MODEL), F32),
        compiler_params=_cparams(("arbitrary",)), name="experts",
    )(blk_expert, blk_valid, blk_first, blk_io, xs, g2, wg, wu, wd)


def _combine_kernel(x_ref, ya_ref, yb_ref, route_ref, g_ref, o_ref, *, final):
    route = route_ref[...]
    wa = route[:, 2:3]
    wb = route[:, 3:4]
    x = x_ref[...] + (ya_ref[...] * wa + yb_ref[...] * wb)
    if final:
        x = _rms(x, g_ref[...])
    o_ref[...] = x


def _combine(x, yg, route, g, final, tm):
    T = x.shape[0]
    nt = T // tm
    return pl.pallas_call(
        functools.partial(_combine_kernel, final=final), grid=(nt,),
        in_specs=[pl.BlockSpec((tm, D_MODEL), lambda i: (i, 0)),
                  pl.BlockSpec((tm, D_MODEL), lambda i: (i, 0)),
                  pl.BlockSpec((tm, D_MODEL), lambda i: (i + nt, 0)),
                  pl.BlockSpec((tm, ROUTE_LANES), lambda i: (i, 0)),
                  pl.BlockSpec((1, D_MODEL), lambda i: (0, 0))],
        out_specs=pl.BlockSpec((tm, D_MODEL), lambda i: (i, 0)),
        out_shape=jax.ShapeDtypeStruct((T, D_MODEL), F32),
        compiler_params=_cparams(("parallel",)), name="combine",
    )(x, yg, yg, route, g)


def _slots_kernel(route_ref, start_ref, slot_ref):
    route = route_ref[...]
    lanef = lax.broadcasted_iota(jnp.int32, route.shape, 1).astype(F32)
    lane = lax.broadcasted_iota(jnp.int32, route.shape, 1)
    start = start_ref[...]

    def slot(k):
        first = jnp.sum(jnp.where(lanef == route[:, k:k + 1], start, 0.0), axis=-1, keepdims=True)
        return first + route[:, 4 + k:5 + k]

    slab = jnp.where(lane == 0, slot(0), jnp.where(lane == 1, slot(1), 0.0))
    slot_ref[...] = jnp.transpose(slab)[0:8, :]


def _slots(route, pad_start, tm):
    T = route.shape[0]
    start = jnp.pad(pad_start.astype(F32), (0, ROUTE_LANES - N_EXPERTS))[None, :]
    out = pl.pallas_call(
        _slots_kernel, grid=(T // tm,),
        in_specs=[pl.BlockSpec((tm, ROUTE_LANES), lambda i: (i, 0)),
                  pl.BlockSpec((1, ROUTE_LANES), lambda i: (0, 0))],
        out_specs=pl.BlockSpec((8, tm), lambda i: (0, i)),
        out_shape=jax.ShapeDtypeStruct((8, T), F32),
        compiler_params=_cparams(("parallel",)), name="slots",
    )(route, start)
    return out[0:TOP_K].astype(jnp.int32)


def _dispatch_plan(route, cnt, T, tm):
    A = T * TOP_K
    counts = cnt[0, :N_EXPERTS].astype(jnp.int32)
    padded = (counts + EXPERT_BLOCK - 1) // EXPERT_BLOCK * EXPERT_BLOCK
    pad_end = jnp.cumsum(padded)
    pad_start = pad_end - padded
    slot = _slots(route, pad_start, tm)
    n_blocks = A // EXPERT_BLOCK + N_EXPERTS
    n_slots = n_blocks * EXPERT_BLOCK
    tok = jnp.broadcast_to(jnp.arange(T, dtype=jnp.int32)[None, :], (TOP_K, T))
    blk_start = jnp.arange(n_blocks, dtype=jnp.int32) * EXPERT_BLOCK
    blk_valid = (blk_start < pad_end[-1]).astype(jnp.int32)
    blk_io = jnp.minimum(jnp.arange(n_blocks, dtype=jnp.int32), jnp.sum(blk_valid) - 1)
    blk_expert = jnp.minimum(jnp.sum((blk_start[:, None] >= pad_end[None, :]).astype(jnp.int32), axis=1),
                             N_EXPERTS - 1)[blk_io]
    blk_first = jnp.concatenate([jnp.ones((1,), jnp.int32),
                                 (blk_expert[1:] != blk_expert[:-1]).astype(jnp.int32)])
    _, tok_sorted = lax.sort_key_val(slot.reshape(A), tok.reshape(A))
    shift = pad_start - (jnp.cumsum(counts) - counts)
    src = (blk_start - shift[blk_expert])[:, None] + jnp.arange(EXPERT_BLOCK, dtype=jnp.int32)[None, :]
    slot_tok = tok_sorted[src % A].reshape(n_slots)
    comb_idx = slot.reshape(A)
    return slot_tok, blk_expert, blk_valid, blk_first, blk_io, comb_idx


def _rope_tables(seq, dim, period, first):
    half = dim // 2
    inv = ROPE_THETA ** (-jnp.arange(0, dim, 2, dtype=F32) / dim)
    ang = jnp.arange(seq, dtype=F32)[:, None] * inv[None, :]
    cos, sin = jnp.cos(ang), jnp.sin(ang)
    d = jnp.arange(LANES) % period - first
    in_a = (d >= 0) & (d < half)
    in_b = (d >= half) & (d < dim)
    idx = jnp.clip(jnp.where(in_b, d - half, d), 0, half - 1)
    c = jnp.where((in_a | in_b)[None, :], cos[:, idx], 1.0)
    sa = jnp.where(in_a[None, :], -sin[:, idx], 0.0)
    sb = jnp.where(in_b[None, :], sin[:, idx], 0.0)
    return c, sa, sb


def _pad_heads(w, width):
    k = w.shape[0]
    w = w.reshape(k, MLA_HEADS, width)
    return jnp.pad(w, ((0, 0), (0, 0), (0, MLA_SLOT - width))).reshape(k, MLA_PAD)


def _prep_layer(l, norm1_g, w_in, q_norm_g, w_uq, kv_norm_g, w_uk, w_uv, w_pa, w_pb, w_o,
                norm2_g, w_rg, b_rg, w_re, b_re, w_e_gate, w_e_up, w_e_down):
    w = w_in[l]
    o = 0
    parts = []
    for n in (DIL_WIDTH, DIL_WIDTH, DIL_WIDTH, Q_LORA, KV_LORA, MLA_ROPE, 2 * D_MODEL):
        parts.append(w[:, o:o + n])
        o += n
    wq, wk, wv, wcq, wckv, wkr, wg = parts
    wkr_pad = jnp.pad(wkr, ((0, 0), (MLA_NOPE, MLA_SLOT - MLA_NOPE - MLA_ROPE)))
    wr = jnp.pad(jnp.concatenate([w_re[l], w_rg[l]], axis=1),
                 ((0, 0), (0, ROUTE_LANES - N_EXPERTS - N_GROUPS)))
    wrh = wr.astype(BF16)
    wrl = (wr - wrh.astype(F32)).astype(BF16)
    br = jnp.pad(jnp.concatenate([b_re[l], b_rg[l]]), (0, ROUTE_LANES - N_EXPERTS - N_GROUPS))
    return dict(
        g1=norm1_g[l][None, :], wq=wq.astype(BF16), wk=wk.astype(BF16), wv=wv.astype(BF16),
        wcq=wcq.astype(BF16), wckv=wckv.astype(BF16), wkr=wkr_pad.astype(BF16), wg=wg.astype(BF16),
        qn=q_norm_g[l][None, :], kvn=kv_norm_g[l][None, :],
        wuq=_pad_heads(w_uq[l], MLA_QK).astype(BF16), wuk=_pad_heads(w_uk[l], MLA_NOPE).astype(BF16),
        wuv=_pad_heads(w_uv[l], MLA_V).astype(BF16),
        wpa=w_pa[l].astype(BF16), wpb=w_pb[l].astype(BF16), wo=w_o[l].astype(BF16),
        g2=norm2_g[l][None, :], wrh=wrh, wrl=wrl, br=br[None, :].astype(F32),
        weg=w_e_gate, weu=w_e_up, wed=w_e_down)


def _trunk(x3, layers, final_g, tm=512):
    batch, seq, _ = x3.shape
    T = batch * seq
    x = x3.reshape(T, D_MODEL)
    tabs = _rope_tables(seq, ROT_DIM, HEAD_DIM, 0) + _rope_tables(seq, MLA_ROPE, MLA_SLOT, MLA_NOPE)
    fg = final_g[None, :]
    for l, lw in enumerate(layers):
        outs = _inproj(x, lw, tabs, seq, tm)
        n_g = len(DIL_GROUPS)
        qd, kd, vd = outs[0:n_g], outs[n_g:2 * n_g], outs[2 * n_g:3 * n_g]
        qm, km, vm, gates = outs[3 * n_g:]
        dil = []
        for g, (_, d) in enumerate(DIL_GROUPS):
            dil.extend(_dilated_group(qd[g], kd[g], vd[g], batch, seq, g, d))
        ob = _mla(qm, km, vm, batch, seq)
        xm, route, cnt = _outproj(x, dil, ob, gates, lw, seq, tm)
        slot_tok, blk_expert, blk_valid, blk_first, blk_io, comb_idx = _dispatch_plan(route, cnt, T, tm)
        xs = _gather_rows(xm, slot_tok)
        ys = _experts(xs, lw['g2'], blk_expert, blk_valid, blk_first, blk_io,
                      lw['weg'], lw['weu'], lw['wed'], l)
        yg = _gather_rows(ys, comb_idx)
        x = _combine(xm, yg, route, fg, l == len(layers) - 1, tm)
    return x.reshape(batch, seq, D_MODEL)


def kernel(x_prompt, x_sample, norm1_g, w_in, q_norm_g, w_uq, kv_norm_g, w_uk, w_uv, w_pa, w_pb, w_o,
           norm2_g, w_rg, b_rg, w_re, b_re, w_e_gate, w_e_up, w_e_down, final_g):
    layers = [_prep_layer(l, norm1_g, w_in, q_norm_g, w_uq, kv_norm_g, w_uk, w_uv, w_pa, w_pb, w_o,
                          norm2_g, w_rg, b_rg, w_re, b_re, w_e_gate, w_e_up, w_e_down)
              for l in range(DEPTH)]
    return (_trunk(x_prompt, layers, final_g), _trunk(x_sample, layers, final_g))
```
